```python
import math, functools
import jax, jax.numpy as jnp
from jax import lax
import numpy as np

D_MODEL = 1024
BATCH = 2
SEQ = 16384
DEPTH = 2
DEC_BATCH = 16
DEC_SEQ = 32
PAST_LEN = 4096

CHUNK = 64
Q_BLOCK = 128
N_MIXERS = 2
N_FOX = (DEPTH + 1) // 2
N_DIFF = DEPTH // 2
FOX_HEADS = 16
FOX_HD = D_MODEL // FOX_HEADS
DIFF_HEADS = 8
DIFF_HD = D_MODEL // DIFF_HEADS // 2
D_FF = -(-8 * D_MODEL // (3 * 256)) * 256
DEEPNORM_ALPHA = (2.0 * DEPTH) ** 0.25
DEEPNORM_BETA = (8.0 * DEPTH) ** -0.25
LN_EPS = 1e-5
NEG_INF = -1e30

kernel_name = 'fox_diffattn_deepnorm_streaming_step'


def _layernorm(x, g, b):
    xf = x.astype(jnp.float32)
    mu = jnp.mean(xf, axis=-1, keepdims=True)
    var = jnp.mean(jnp.square(xf - mu), axis=-1, keepdims=True)
    return ((xf - mu) * lax.rsqrt(var + LN_EPS) * g.astype(jnp.float32) + b.astype(jnp.float32)).astype(x.dtype)


def _sweep_queries(block_fn, q_pos, q_args, kv_args):
    t_q = q_pos.shape[0]
    if t_q <= Q_BLOCK:
        return block_fn(q_pos, *q_args, *kv_args)
    nb = t_q // Q_BLOCK

    def split(a):
        return jnp.moveaxis(a.reshape(a.shape[0], nb, Q_BLOCK, *a.shape[2:]), 1, 0)

    xs = (q_pos.reshape(nb, Q_BLOCK),) + tuple(split(a) for a in q_args)
    out = lax.map(lambda xb: block_fn(xb[0], *xb[1:], *kv_args), xs)
    out = jnp.moveaxis(out, 0, 1)
    return out.reshape(out.shape[0], t_q, *out.shape[3:])


def _fox_block(q_pos, q, f_q, k_pos, k, v, f_k):
    s = jnp.einsum('bqhd,bshd->bhqs', q, k, preferred_element_type=jnp.float32) * (FOX_HD ** -0.5)
    decay = jnp.transpose(f_q, (0, 2, 1))[:, :, :, None] - jnp.transpose(f_k, (0, 2, 1))[:, :, None, :]
    causal = k_pos[None, :] <= q_pos[:, None]
    s = jnp.where(causal, s + decay, NEG_INF)
    p = jax.nn.softmax(s, axis=-1)
    return jnp.einsum('bhqs,bshd->bqhd', p.astype(v.dtype), v)


def _fox_mixer(x, w_in, b_f, w_out, past_k=None, past_v=None, past_logf=None):
    b, t, _ = x.shape
    proj = x @ w_in
    q, k, v, fl = jnp.split(proj, [D_MODEL, 2 * D_MODEL, 3 * D_MODEL], axis=-1)
    q = q.reshape(b, t, FOX_HEADS, FOX_HD)
    k = k.reshape(b, t, FOX_HEADS, FOX_HD)
    v = v.reshape(b, t, FOX_HEADS, FOX_HD)
    logf = jax.nn.log_sigmoid((fl + b_f).astype(jnp.float32))
    if past_k is None:
        k_all, v_all, logf_all, offset = k, v, logf, 0
    else:
        offset = past_k.shape[1]
        k_all = jnp.concatenate([past_k, k], axis=1)
        v_all = jnp.concatenate([past_v, v], axis=1)
        logf_all = jnp.concatenate([past_logf.astype(jnp.float32), logf], axis=1)
    f_cum = jnp.cumsum(logf_all, axis=1)
    k_pos = jnp.arange(k_all.shape[1])
    q_pos = offset + jnp.arange(t)
    o = _sweep_queries(_fox_block, q_pos, (q, f_cum[:, offset:]), (k_pos, k_all, v_all, f_cum))
    y = o.reshape(b, t, D_MODEL) @ w_out
    return y, k, v, logf


def _diff_block(lam, q_pos, q, k_pos, k, v):
    q1, q2 = jnp.split(q, 2, axis=-1)
    k1, k2 = jnp.split(k, 2, axis=-1)
    slopes = jnp.exp2(-8.0 * jnp.arange(1, DIFF_HEADS + 1, dtype=jnp.float32) / DIFF_HEADS)
    dist = jnp.abs(q_pos[:, None] - k_pos[None, :]).astype(jnp.float32)
    alibi = -slopes[:, None, None] * dist[None]
    mask = (k_pos[None, :] // CHUNK) <= (q_pos[:, None] // CHUNK)

    def attn_map(qa, ka):
        s = jnp.einsum('bqhd,bshd->bhqs', qa, ka, preferred_element_type=jnp.float32) * (DIFF_HD ** -0.5)
        return jax.nn.softmax(jnp.where(mask, s + alibi, NEG_INF), axis=-1)

    a = attn_map(q1, k1) - lam * attn_map(q2, k2)
    return jnp.einsum('bhqs,bshd->bqhd', a.astype(v.dtype), v)


def _diff_mixer(x, layer_idx, w_in, lam_params, subln_g, w_out, past_k=None, past_v=None):
    b, t, _ = x.shape
    proj = x @ w_in
    q, k, v = jnp.split(proj, 3, axis=-1)
    q = q.reshape(b, t, DIFF_HEADS, 2 * DIFF_HD)
    k = k.reshape(b, t, DIFF_HEADS, 2 * DIFF_HD)
    v = v.reshape(b, t, DIFF_HEADS, 2 * DIFF_HD)
    lam_init = 0.8 - 0.6 * math.exp(-0.3 * layer_idx)
    lp = lam_params.astype(jnp.float32)
    lam = jnp.exp(jnp.sum(lp[0] * lp[1])) - jnp.exp(jnp.sum(lp[2] * lp[3])) + lam_init
    if past_k is None:
        k_all, v_all, offset = k, v, 0
    else:
        offset = past_k.shape[1]
        k_all = jnp.concatenate([past_k, k], axis=1)
        v_all = jnp.concatenate([past_v, v], axis=1)
    k_pos = jnp.arange(k_all.shape[1])
    q_pos = offset + jnp.arange(t)
    o = _sweep_queries(functools.partial(_diff_block, lam), q_pos, (q,), (k_pos, k_all, v_all))
    of = o.astype(jnp.float32)
    of = of * lax.rsqrt(jnp.mean(jnp.square(of), axis=-1, keepdims=True) + LN_EPS)
    of = of * subln_g.astype(jnp.float32) * (1.0 - lam_init)
    y = of.astype(x.dtype).reshape(b, t, D_MODEL) @ w_out
    return y, k, v


def _swiglu(x, w_in, w_out):
    g, u = jnp.split(x @ w_in, 2, axis=-1)
    return (jax.nn.silu(g) * u) @ w_out


def setup_inputs(seed: int = 0) -> dict:
    key = jax.random.key(seed)
    ks = jax.random.split(key, 20)
    f32 = jnp.float32
    s_d = D_MODEL ** -0.5
    fox_w_in = jax.random.normal(ks[7], (N_FOX, D_MODEL, 3 * D_MODEL + FOX_HEADS), f32) * s_d
    fox_w_in = fox_w_in.at[:, :, 2 * D_MODEL:3 * D_MODEL].multiply(DEEPNORM_BETA)
    diff_w_in = jax.random.normal(ks[10], (N_DIFF, D_MODEL, 3 * D_MODEL), f32) * s_d
    diff_w_in = diff_w_in.at[:, :, 2 * D_MODEL:].multiply(DEEPNORM_BETA)
    return {
        'x_prompt': jax.random.normal(ks[0], (BATCH, SEQ, D_MODEL), f32),
        'x_sample': jax.random.normal(ks[1], (DEC_BATCH, DEC_SEQ, D_MODEL), f32),
        'cache_fox_k': jax.random.normal(ks[2], (N_FOX, DEC_BATCH, PAST_LEN, FOX_HEADS, FOX_HD), f32),
        'cache_fox_v': jax.random.normal(ks[3], (N_FOX, DEC_BATCH, PAST_LEN, FOX_HEADS, FOX_HD), f32),
        'cache_fox_logf': jax.nn.log_sigmoid(3.0 + jax.random.normal(ks[4], (N_FOX, DEC_BATCH, PAST_LEN, FOX_HEADS), f32)),
        'cache_diff_k': jax.random.normal(ks[5], (N_DIFF, DEC_BATCH, PAST_LEN, DIFF_HEADS, 2 * DIFF_HD), f32),
        'cache_diff_v': jax.random.normal(ks[6], (N_DIFF, DEC_BATCH, PAST_LEN, DIFF_HEADS, 2 * DIFF_HD), f32),
        'fox_w_in': fox_w_in,
        'fox_b_f': 1.0 + 4.0 * jax.random.uniform(ks[8], (N_FOX, FOX_HEADS), f32),
        'fox_w_out': jax.random.normal(ks[9], (N_FOX, D_MODEL, D_MODEL), f32) * s_d * DEEPNORM_BETA,
        'diff_w_in': diff_w_in,
        'diff_lambda': 0.1 * jax.random.normal(ks[11], (N_DIFF, 4, DIFF_HD), f32),
        'diff_subln_g': 1.0 + 0.01 * jax.random.normal(ks[12], (N_DIFF, 2 * DIFF_HD), f32),
        'diff_w_out': jax.random.normal(ks[13], (N_DIFF, D_MODEL, D_MODEL), f32) * s_d * DEEPNORM_BETA,
        'ffn_w_in': jax.random.normal(ks[14], (DEPTH, D_MODEL, 2 * D_FF), f32) * s_d,
        'ffn_w_out': jax.random.normal(ks[15], (DEPTH, D_FF, D_MODEL), f32) * (D_FF ** -0.5) * DEEPNORM_BETA,
        'ln_g': 1.0 + 0.01 * jax.random.normal(ks[16], (DEPTH, 2, D_MODEL), f32),
        'ln_b': 0.01 * jax.random.normal(ks[17], (DEPTH, 2, D_MODEL), f32),
    }


def reference(x_prompt, x_sample, cache_fox_k, cache_fox_v, cache_fox_logf, cache_diff_k, cache_diff_v,
              fox_w_in, fox_b_f, fox_w_out, diff_w_in, diff_lambda, diff_subln_g, diff_w_out,
              ffn_w_in, ffn_w_out, ln_g, ln_b):
    def run(x, past):
        fk, fv, ff, dk, dv = [], [], [], [], []
        for i in range(DEPTH):
            j = i // N_MIXERS
            if i % N_MIXERS == 0:
                pk, pv, pf = (None, None, None) if past is None else (past[0][j], past[1][j], past[2][j])
                y, k, v, lf = _fox_mixer(x, fox_w_in[j], fox_b_f[j], fox_w_out[j], pk, pv, pf)
                fk.append(k)
                fv.append(v)
                ff.append(lf)
            else:
                pk, pv = (None, None) if past is None else (past[3][j], past[4][j])
                y, k, v = _diff_mixer(x, i, diff_w_in[j], diff_lambda[j], diff_subln_g[j], diff_w_out[j], pk, pv)
                dk.append(k)
                dv.append(v)
            x = _layernorm(DEEPNORM_ALPHA * x + y, ln_g[i, 0], ln_b[i, 0])
            x = _layernorm(DEEPNORM_ALPHA * x + _swiglu(x, ffn_w_in[i], ffn_w_out[i]), ln_g[i, 1], ln_b[i, 1])
        return x, jnp.stack(fk), jnp.stack(fv), jnp.stack(ff), jnp.stack(dk), jnp.stack(dv)

    y_prompt, fox_k_p, fox_v_p, fox_logf_p, diff_k_p, diff_v_p = run(x_prompt, None)
    past = (cache_fox_k, cache_fox_v, cache_fox_logf, cache_diff_k, cache_diff_v)
    y_sample, fox_k_s, fox_v_s, fox_logf_s, diff_k_s, diff_v_s = run(x_sample, past)
    return (y_prompt, y_sample, fox_k_p, fox_v_p, fox_logf_p, diff_k_p, diff_v_p,
            fox_k_s, fox_v_s, fox_logf_s, diff_k_s, diff_v_s)
```

```python
import functools
import math

import jax
import jax.numpy as jnp
from jax import lax
from jax.experimental import pallas as pl
from jax.experimental.pallas import tpu as pltpu

F32 = jnp.float32
BF16 = jnp.bfloat16

D_MODEL = 1024
DEPTH = 2
FOX_HEADS = 16
FOX_HD = 64
DIFF_HEADS = 8
DIFF_HD = 64
D_FF = 2816
CHUNK = 64
DEEPNORM_ALPHA = (2.0 * DEPTH) ** 0.25
LN_EPS = 1e-5
NEG_INF = -1e30
LOG2E = 1.4426950408889634

LANES = 128
HEAD_PAIRS = D_MODEL // LANES
VMEM_LIMIT = 56 * 1024 * 1024


def _cparams(n_axes):
    return pltpu.CompilerParams(dimension_semantics=("arbitrary",) * n_axes,
                                vmem_limit_bytes=VMEM_LIMIT)


def _log_sigmoid(z):
    return jnp.minimum(z, 0.0) - jnp.log1p(jnp.exp(-jnp.abs(z)))


def _layernorm(z, g, b):
    mu = jnp.mean(z, axis=-1, keepdims=True)
    zc = z - mu
    var = jnp.mean(zc * zc, axis=-1, keepdims=True)
    return zc * lax.rsqrt(var + LN_EPS) * g + b


def _dot(a, b):
    return jnp.dot(a, b, preferred_element_type=F32)


def _dot_nt(a, b):
    return lax.dot_general(a, b, (((1,), (1,)), ((), ())), preferred_element_type=F32)


def _proj_body(*refs, q_scale, with_forget):
    if with_forget:
        x_ref, w_ref, wf_ref, wft_ref, bf_ref, bft_ref, qkv_ref, k_ref, v_ref, lf_ref, lft_ref = refs
    else:
        x_ref, w_ref, qkv_ref, k_ref, v_ref = refs
    j = pl.program_id(1)
    xb = x_ref[...].astype(BF16)
    acc = _dot(xb, w_ref[...])

    def store_heads(val):
        vb = val.astype(BF16)
        for p in range(HEAD_PAIRS):
            qkv_ref[0, p] = vb[:, p * LANES:(p + 1) * LANES]

    @pl.when(j == 0)
    def _():
        store_heads(acc * q_scale)
        if with_forget:
            fl = _dot(xb, wf_ref[...])[:, :FOX_HEADS] + bf_ref[...]
            lf_ref[...] = _log_sigmoid(fl)
            flt = _dot_nt(wft_ref[...], xb) + bft_ref[...]
            lft_ref[...] = _log_sigmoid(flt)

    @pl.when(j == 1)
    def _():
        store_heads(acc)
        k_ref[...] = acc

    @pl.when(j == 2)
    def _():
        store_heads(acc)
        v_ref[...] = acc


def _qkv_proj(x2d, w_qkv, q_scale, forget=None, tm=512):
    m = x2d.shape[0]
    tm = min(tm, m)
    grid = (m // tm, 3)
    in_specs = [pl.BlockSpec((tm, D_MODEL), lambda i, j: (i, 0)),
                pl.BlockSpec((D_MODEL, D_MODEL), lambda i, j: (0, j))]
    out_shape = [jax.ShapeDtypeStruct((3, HEAD_PAIRS, m, LANES), BF16),
                 jax.ShapeDtypeStruct((m, D_MODEL), F32),
                 jax.ShapeDtypeStruct((m, D_MODEL), F32)]
    out_specs = [pl.BlockSpec((1, HEAD_PAIRS, tm, LANES), lambda i, j: (j, 0, i, 0)),
                 pl.BlockSpec((tm, D_MODEL), lambda i, j: (i, 0)),
                 pl.BlockSpec((tm, D_MODEL), lambda i, j: (i, 0))]
    args = [x2d, w_qkv]
    if forget is not None:
        wf, wft, bf, bft = forget
        in_specs += [pl.BlockSpec((D_MODEL, LANES), lambda i, j: (0, 0)),
                     pl.BlockSpec((FOX_HEADS, D_MODEL), lambda i, j: (0, 0)),
                     pl.BlockSpec((1, FOX_HEADS), lambda i, j: (0, 0)),
                     pl.BlockSpec((FOX_HEADS, 1), lambda i, j: (0, 0))]
        out_shape += [jax.ShapeDtypeStruct((m, FOX_HEADS), F32),
                      jax.ShapeDtypeStruct((FOX_HEADS, m), F32)]
        out_specs += [pl.BlockSpec((tm, FOX_HEADS), lambda i, j: (i, 0)),
                      pl.BlockSpec((FOX_HEADS, tm), lambda i, j: (0, i))]
        args += [wf, wft, bf, bft]
    return pl.pallas_call(
        functools.partial(_proj_body, q_scale=q_scale, with_forget=forget is not None),
        grid=grid, in_specs=in_specs, out_specs=out_specs, out_shape=out_shape,
        compiler_params=_cparams(2), name="qkv_proj_f" if forget is not None else "qkv_proj",
    )(*args)


def _cumsum_body(x_ref, o_ref, carry_ref, *, blk):
    i = pl.program_id(1)

    @pl.when(i == 0)
    def _():
        carry_ref[...] = jnp.zeros_like(carry_ref)

    x = x_ref[...]
    hi = x.astype(BF16)
    r1 = x - hi.astype(F32)
    mid = r1.astype(BF16)
    lo = (r1 - mid.astype(F32)).astype(BF16)
    row = lax.broadcasted_iota(jnp.int32, (blk, blk), 0)
    col = lax.broadcasted_iota(jnp.int32, (blk, blk), 1)
    tri = (row <= col).astype(BF16)
    cs = _dot(hi, tri) + _dot(mid, tri) + _dot(lo, tri) + carry_ref[...]
    o_ref[0] = cs
    carry_ref[...] = cs[:, blk - 1:blk]


def _cumsum_time(lft, n_seq, blk):
    h, total = lft.shape
    nblk = total // n_seq // blk
    return pl.pallas_call(
        functools.partial(_cumsum_body, blk=blk),
        grid=(n_seq, nblk),
        in_specs=[pl.BlockSpec((h, blk), lambda b, i: (0, b * nblk + i))],
        out_specs=pl.BlockSpec((1, h, blk), lambda b, i: (b * nblk + i, 0, 0)),
        out_shape=jax.ShapeDtypeStruct((n_seq * nblk, h, blk), F32),
        scratch_shapes=[pltpu.VMEM((h, 1), F32)],
        compiler_params=_cparams(2), name="cumsum_time",
    )(lft)


def _fox_attn_body(q_ref, k_ref, v_ref, f_ref, o_ref, acc_ref, m_ref, l_ref, *, blk):
    qi = pl.program_id(2)
    q = q_ref[0, 0]
    lane = lax.broadcasted_iota(jnp.int32, (blk, LANES), 1)
    lo_mask = lane < FOX_HD
    zero = jnp.zeros_like(q)
    q_heads = (jnp.where(lo_mask, q, zero), jnp.where(lo_mask, zero, q))
    acc_ref[...] = jnp.zeros_like(acc_ref)
    m_ref[...] = jnp.full_like(m_ref, NEG_INF)
    l_ref[...] = jnp.zeros_like(l_ref)
    f_first = f_ref[qi, 0][:, 0:1]

    def block(kj, diagonal):
        start = pl.multiple_of(kj * blk, blk)
        k = k_ref[0, 0, pl.ds(start, blk), :]
        v = v_ref[0, 0, pl.ds(start, blk), :]
        bias = (f_first - f_ref[kj, 0]) * LOG2E
        if diagonal:
            row = lax.broadcasted_iota(jnp.int32, (blk, blk), 0)
            col = lax.broadcasted_iota(jnp.int32, (blk, blk), 1)
            causal = col <= row
        pv, alpha = [], []
        for h in range(2):
            s = _dot_nt(q_heads[h], k) + bias[h:h + 1, :]
            if diagonal:
                s = jnp.where(causal, s, NEG_INF)
            m_old = m_ref[h]
            m_new = jnp.maximum(m_old, jnp.max(s, axis=1, keepdims=True))
            p = jnp.exp2(s - m_new)
            a = jnp.exp2(m_old - m_new)
            l_ref[h] = a * l_ref[h] + jnp.sum(p, axis=1, keepdims=True)
            m_ref[h] = m_new
            pv.append(_dot(p.astype(BF16), v))
            alpha.append(a)
        acc = acc_ref[...]
        acc_ref[...] = jnp.where(lo_mask, alpha[0] * acc + pv[0], alpha[1] * acc + pv[1])

    def loop_body(kj, carry):
        block(kj, False)
        return carry

    lax.fori_loop(0, qi, loop_body, 0)
    block(qi, True)
    o_ref[...] = (acc_ref[...] / jnp.where(lo_mask, l_ref[0], l_ref[1])).astype(BF16)


def _fox_attention(qkvh, fcum, n_seq, t, blk):
    m = n_seq * t
    nq = t // blk
    f4 = fcum.reshape(n_seq * nq, HEAD_PAIRS, 2, blk)
    return pl.pallas_call(
        functools.partial(_fox_attn_body, blk=blk),
        grid=(n_seq, HEAD_PAIRS, nq),
        in_specs=[pl.BlockSpec((1, 1, blk, LANES), lambda b, p, i: (0, p, b * nq + i, 0)),
                  pl.BlockSpec((1, 1, t, LANES), lambda b, p, i: (1, p, b, 0)),
                  pl.BlockSpec((1, 1, t, LANES), lambda b, p, i: (2, p, b, 0)),
                  pl.BlockSpec((nq, 1, 2, blk), lambda b, p, i: (b, p, 0, 0))],
        out_specs=pl.BlockSpec((blk, LANES), lambda b, p, i: (b * nq + i, p)),
        out_shape=jax.ShapeDtypeStruct((m, D_MODEL), BF16),
        scratch_shapes=[pltpu.VMEM((blk, LANES), F32),
                        pltpu.VMEM((2, blk, 1), F32),
                        pltpu.VMEM((2, blk, 1), F32)],
        compiler_params=_cparams(3), name="fox_attention",
    )(qkvh, qkvh, qkvh, f4)


def _diff_lambda(lam_ref, lam_init):
    lp = lam_ref[...]
    a = jnp.sum(lp[0:1] * lp[1:2], axis=1, keepdims=True)
    b = jnp.sum(lp[2:3] * lp[3:4], axis=1, keepdims=True)
    return jnp.exp(a) - jnp.exp(b) + lam_init


def _subln(o, g, lam_init):
    o = o * lax.rsqrt(jnp.mean(o * o, axis=-1, keepdims=True) + LN_EPS)
    return o * g * (1.0 - lam_init)


def _diff_attn_body(q_ref, k_ref, v_ref, lam_ref, g_ref, o_ref, acc_ref, m_ref, l_ref, *, blk, lam_init):
    head = pl.program_id(1)
    qi = pl.program_id(2)
    q = q_ref[0, 0]
    lane = lax.broadcasted_iota(jnp.int32, (blk, LANES), 1)
    lo_mask = lane < DIFF_HD
    zero = jnp.zeros_like(q)
    q_maps = (jnp.where(lo_mask, q, zero), jnp.where(lo_mask, zero, q))
    acc_ref[...] = jnp.zeros_like(acc_ref)
    m_ref[...] = jnp.full_like(m_ref, NEG_INF)
    l_ref[...] = jnp.zeros_like(l_ref)
    slope = jnp.exp2(-jnp.full((1, 1), head + 1, F32)) * LOG2E

    def block(kj, diagonal):
        start = pl.multiple_of(kj * blk, blk)
        k = k_ref[0, 0, pl.ds(start, blk), :]
        v = v_ref[0, 0, pl.ds(start, blk), :]
        if diagonal:
            row = lax.broadcasted_iota(jnp.int32, (blk, blk), 0)
            col = lax.broadcasted_iota(jnp.int32, (blk, blk), 1)
            bias = slope * (row - jnp.abs(row - col)).astype(F32)
            visible = (col // CHUNK) <= (row // CHUNK)
        else:
            col = lax.broadcasted_iota(jnp.int32, (1, blk), 1)
            bias = slope * (col + (kj - qi) * blk).astype(F32)
        for u in range(2):
            s = _dot_nt(q_maps[u], k) + bias
            if diagonal:
                s = jnp.where(visible, s, NEG_INF)
            m_old = m_ref[u]
            m_new = jnp.maximum(m_old, jnp.max(s, axis=1, keepdims=True))
            p = jnp.exp2(s - m_new)
            a = jnp.exp2(m_old - m_new)
            l_ref[u] = a * l_ref[u] + jnp.sum(p, axis=1, keepdims=True)
            m_ref[u] = m_new
            acc_ref[u] = a * acc_ref[u] + _dot(p.astype(BF16), v)

    def loop_body(kj, carry):
        block(kj, False)
        return carry

    lax.fori_loop(0, qi, loop_body, 0)
    block(qi, True)
    lam = _diff_lambda(lam_ref, lam_init)
    o = acc_ref[0] / l_ref[0] - lam * (acc_ref[1] / l_ref[1])
    o_ref[...] = _subln(o, g_ref[...], lam_init).astype(BF16)


def _diff_attention(qkvh, lam_params, subln_g, n_seq, t, blk, lam_init):
    m = n_seq * t
    nq = t // blk
    return pl.pallas_call(
        functools.partial(_diff_attn_body, blk=blk, lam_init=lam_init),
        grid=(n_seq, DIFF_HEADS, nq),
        in_specs=[pl.BlockSpec((1, 1, blk, LANES), lambda b, p, i: (0, p, b * nq + i, 0)),
                  pl.BlockSpec((1, 1, t, LANES), lambda b, p, i: (1, p, b, 0)),
                  pl.BlockSpec((1, 1, t, LANES), lambda b, p, i: (2, p, b, 0)),
                  pl.BlockSpec((4, DIFF_HD), lambda b, p, i: (0, 0)),
                  pl.BlockSpec((1, LANES), lambda b, p, i: (0, 0))],
        out_specs=pl.BlockSpec((blk, LANES), lambda b, p, i: (b * nq + i, p)),
        out_shape=jax.ShapeDtypeStruct((m, D_MODEL), BF16),
        scratch_shapes=[pltpu.VMEM((2, blk, LANES), F32),
                        pltpu.VMEM((2, blk, 1), F32),
                        pltpu.VMEM((2, blk, 1), F32)],
        compiler_params=_cparams(3), name="diff_attention",
    )(qkvh, qkvh, qkvh, lam_params, subln_g)


def _decode_attn_body(*refs, mode, t_new, past_len, blk, lam_init):
    if mode == "fox":
        q_ref, kn_ref, vn_ref, kc_ref, vc_ref, f_ref, o_ref, qbd_ref, acc_ref, m_ref, l_ref = refs
    else:
        q_ref, kn_ref, vn_ref, kc_ref, vc_ref, lam_ref, g_ref, o_ref, qbd_ref, acc_ref, m_ref, l_ref = refs
    kj = pl.program_id(1)
    nk = pl.num_programs(1)
    groups = D_MODEL // FOX_HD
    rows = groups * t_new
    row1 = lax.broadcasted_iota(jnp.int32, (rows, 1), 0)

    def gather_pairs(ref):
        return jnp.concatenate([ref[0, p] for p in range(HEAD_PAIRS)], axis=1)

    @pl.when(kj == 0)
    def _():
        q = gather_pairs(q_ref)
        qt = jnp.concatenate([q] * groups, axis=0)
        r = lax.broadcasted_iota(jnp.int32, (rows, D_MODEL), 0)
        c = lax.broadcasted_iota(jnp.int32, (rows, D_MODEL), 1)
        qbd_ref[...] = jnp.where((r // t_new) == (c // FOX_HD), qt, jnp.zeros_like(qt))
        acc_ref[...] = jnp.zeros_like(acc_ref)
        m_ref[...] = jnp.full_like(m_ref, NEG_INF)
        l_ref[...] = jnp.zeros_like(l_ref)

    if mode == "fox":
        f_last = f_ref[nk - 1][:, blk - 1:blk]

        def head_rows(x):
            n = x.shape[1]
            return jnp.concatenate(
                [jnp.broadcast_to(x[h:h + 1, :], (t_new, n)) for h in range(FOX_HEADS)], axis=0)
    else:
        slope = jnp.exp2(-(row1 // (2 * t_new) + 1).astype(F32)) * LOG2E

    def update(s, v):
        m_old = m_ref[...]
        m_new = jnp.maximum(m_old, jnp.max(s, axis=1, keepdims=True))
        p = jnp.exp2(s - m_new)
        a = jnp.exp2(m_old - m_new)
        l_ref[...] = a * l_ref[...] + jnp.sum(p, axis=1, keepdims=True)
        m_ref[...] = m_new
        acc_ref[...] = a * acc_ref[...] + _dot(p.astype(BF16), v)

    kc = kc_ref[0].astype(BF16)
    vc = vc_ref[0].astype(BF16)
    s = _dot_nt(qbd_ref[...], kc)
    if mode == "fox":
        s = s + head_rows((f_last - f_ref[kj]) * LOG2E)
    else:
        col = lax.broadcasted_iota(jnp.int32, (1, blk), 1)
        s = s + slope * (col + (kj * blk - past_len)).astype(F32)
    update(s, vc)

    @pl.when(kj == nk - 1)
    def _():
        kn = gather_pairs(kn_ref)
        vn = gather_pairs(vn_ref)
        s = _dot_nt(qbd_ref[...], kn)
        tq = lax.broadcasted_iota(jnp.int32, (rows, t_new), 0) % t_new
        tk = lax.broadcasted_iota(jnp.int32, (rows, t_new), 1)
        if mode == "fox":
            s = s + head_rows((f_last - f_ref[nk][:, :t_new]) * LOG2E)
            visible = tk <= tq
        else:
            s = s + slope * (tq - jnp.abs(tq - tk)).astype(F32)
            visible = ((tk + past_len) // CHUNK) <= ((tq + past_len) // CHUNK)
        update(jnp.where(visible, s, NEG_INF), vn)

        r = lax.broadcasted_iota(jnp.int32, (rows, D_MODEL), 0)
        c = lax.broadcasted_iota(jnp.int32, (rows, D_MODEL), 1)
        inv_l = 1.0 / l_ref[...]
        if mode == "fox":
            own = (r // t_new) == (c // FOX_HD)
            coef = inv_l
        else:
            own = (r // (2 * t_new)) == (c // LANES)
            lam = _diff_lambda(lam_ref, lam_init)
            coef = jnp.where((row1 // t_new) % 2 == 0, inv_l, -lam * inv_l)
        w = jnp.where(own, acc_ref[...] * coef, 0.0)
        o = w[0:t_new]
        for gi in range(1, groups):
            o = o + w[gi * t_new:(gi + 1) * t_new]
        if mode == "diff":
            g = g_ref[...]
            o = jnp.concatenate(
                [_subln(o[:, h * LANES:(h + 1) * LANES], g, lam_init) for h in range(DIFF_HEADS)], axis=1)
        o_ref[0] = o.astype(BF16)


def _decode_attention(qkvh, cache_k, cache_v, extra, mode, n_seq, t_new, blk, lam_init=0.0):
    past_len = cache_k.shape[1]
    nk = past_len // blk
    rows = (D_MODEL // FOX_HD) * t_new
    new_spec = lambda s: pl.BlockSpec((1, HEAD_PAIRS, t_new, LANES), lambda b, j: (s, 0, b, 0))
    cache_spec = pl.BlockSpec((1, blk, D_MODEL), lambda b, j: (b, j, 0))
    in_specs = [new_spec(0), new_spec(1), new_spec(2), cache_spec, cache_spec]
    if mode == "fox":
        (fcum,) = extra
        in_specs.append(pl.BlockSpec((nk + 1, FOX_HEADS, blk), lambda b, j: (b, 0, 0)))
    else:
        in_specs += [pl.BlockSpec((4, DIFF_HD), lambda b, j: (0, 0)),
                     pl.BlockSpec((1, LANES), lambda b, j: (0, 0))]
    out = pl.pallas_call(
        functools.partial(_decode_attn_body, mode=mode, t_new=t_new, past_len=past_len, blk=blk,
                          lam_init=lam_init),
        grid=(n_seq, nk),
        in_specs=in_specs,
        out_specs=pl.BlockSpec((1, t_new, D_MODEL), lambda b, j: (b, 0, 0)),
        out_shape=jax.ShapeDtypeStruct((n_seq, t_new, D_MODEL), BF16),
        scratch_shapes=[pltpu.VMEM((rows, D_MODEL), BF16),
                        pltpu.VMEM((rows, D_MODEL), F32),
                        pltpu.VMEM((rows, 1), F32),
                        pltpu.VMEM((rows, 1), F32)],
        compiler_params=_cparams(2), name="decode_attention_" + mode,
    )(qkvh, qkvh, qkvh, cache_k, cache_v, *extra)
    return out.reshape(n_seq * t_new, D_MODEL)


def _oproj_ln_body(x_ref, o_ref, w_ref, g_ref, b_ref, y_ref):
    y = _dot(o_ref[...], w_ref[...])
    y_ref[...] = _layernorm(DEEPNORM_ALPHA * x_ref[...] + y, g_ref[...], b_ref[...])


def _oproj_ln(x2d, o2d, w, g, b, tm=512):
    m = x2d.shape[0]
    tm = min(tm, m)
    row = pl.BlockSpec((tm, D_MODEL), lambda i: (i, 0))
    vec = pl.BlockSpec((1, D_MODEL), lambda i: (0, 0))
    return pl.pallas_call(
        _oproj_ln_body, grid=(m // tm,),
        in_specs=[row, row, pl.BlockSpec((D_MODEL, D_MODEL), lambda i: (0, 0)), vec, vec],
        out_specs=row, out_shape=jax.ShapeDtypeStruct((m, D_MODEL), F32),
        compiler_params=_cparams(1), name="oproj_ln",
    )(x2d, o2d, w, g, b)


def _ffn_ln_body(x_ref, win_ref, wout_ref, g_ref, b_ref, y_ref):
    x = x_ref[...]
    h = _dot(x.astype(BF16), win_ref[...])
    gate = h[:, :D_FF]
    up = h[:, D_FF:]
    act = (gate * jax.nn.sigmoid(gate) * up).astype(BF16)
    y = _dot(act, wout_ref[...])
    y_ref[...] = _layernorm(DEEPNORM_ALPHA * x + y, g_ref[...], b_ref[...])


def _ffn_ln(x2d, w_in, w_out, g, b, tm=256):
    m = x2d.shape[0]
    tm = min(tm, m)
    row = pl.BlockSpec((tm, D_MODEL), lambda i: (i, 0))
    vec = pl.BlockSpec((1, D_MODEL), lambda i: (0, 0))
    return pl.pallas_call(
        _ffn_ln_body, grid=(m // tm,),
        in_specs=[row,
                  pl.BlockSpec((D_MODEL, 2 * D_FF), lambda i: (0, 0)),
                  pl.BlockSpec((D_FF, D_MODEL), lambda i: (0, 0)),
                  vec, vec],
        out_specs=row, out_shape=jax.ShapeDtypeStruct((m, D_MODEL), F32),
        compiler_params=_cparams(1), name="ffn_ln",
    )(x2d, w_in, w_out, g, b)


def _run_stream(x, past, wts, attn_blk):
    n_seq, t, _ = x.shape
    m = n_seq * t
    x2d = x.reshape(m, D_MODEL)
    lam_init = 0.8 - 0.6 * math.exp(-0.3 * 1)

    qkvh, fk, fv, logf, logft = _qkv_proj(x2d, wts["fox_wqkv"], FOX_HD ** -0.5 * LOG2E, forget=wts["fox_forget"])
    if past is None:
        fcum = _cumsum_time(logft, n_seq, attn_blk)
        o = _fox_attention(qkvh, fcum, n_seq, t, attn_blk)
    else:
        past_k, past_v, past_lf = past[0], past[1], past[2]
        past_len = past_k.shape[1]
        lf_all = jnp.concatenate(
            [jnp.transpose(past_lf, (2, 0, 1)),
             logft.reshape(FOX_HEADS, n_seq, t),
             jnp.zeros((FOX_HEADS, n_seq, attn_blk - t), F32)], axis=2)
        fcum = _cumsum_time(lf_all.reshape(FOX_HEADS, n_seq * (past_len + attn_blk)), n_seq, attn_blk)
        o = _decode_attention(qkvh, past_k.reshape(n_seq, past_len, D_MODEL),
                              past_v.reshape(n_seq, past_len, D_MODEL), (fcum,), "fox", n_seq, t, attn_blk)
    x2d = _oproj_ln(x2d, o, wts["fox_wout"], wts["ln_g"][0][0], wts["ln_b"][0][0])
    x2d = _ffn_ln(x2d, wts["ffn_win"][0], wts["ffn_wout"][0], wts["ln_g"][0][1], wts["ln_b"][0][1])

    qkvh, dk, dv = _qkv_proj(x2d, wts["diff_wqkv"], DIFF_HD ** -0.5 * LOG2E)
    if past is None:
        o = _diff_attention(qkvh, wts["diff_lambda"], wts["diff_g"], n_seq, t, attn_blk, lam_init)
    else:
        past_k, past_v = past[3], past[4]
        past_len = past_k.shape[1]
        o = _decode_attention(qkvh, past_k.reshape(n_seq, past_len, D_MODEL),
                              past_v.reshape(n_seq, past_len, D_MODEL),
                              (wts["diff_lambda"], wts["diff_g"]), "diff", n_seq, t, attn_blk, lam_init)
    x2d = _oproj_ln(x2d, o, wts["diff_wout"], wts["ln_g"][1][0], wts["ln_b"][1][0])
    x2d = _ffn_ln(x2d, wts["ffn_win"][1], wts["ffn_wout"][1], wts["ln_g"][1][1], wts["ln_b"][1][1])

    return (x2d.reshape(n_seq, t, D_MODEL),
            fk.reshape(1, n_seq, t, FOX_HEADS, FOX_HD), fv.reshape(1, n_seq, t, FOX_HEADS, FOX_HD),
            logf.reshape(1, n_seq, t, FOX_HEADS),
            dk.reshape(1, n_seq, t, DIFF_HEADS, 2 * DIFF_HD), dv.reshape(1, n_seq, t, DIFF_HEADS, 2 * DIFF_HD))


def _prepare_weights(fox_w_in, fox_b_f, fox_w_out, diff_w_in, diff_lambda, diff_subln_g, diff_w_out,
                     ffn_w_in, ffn_w_out, ln_g, ln_b):
    wf = fox_w_in[0][:, 3 * D_MODEL:].astype(BF16)
    wf_pad = jnp.pad(wf, ((0, 0), (0, LANES - FOX_HEADS)))
    return {
        "fox_wqkv": fox_w_in[0][:, :3 * D_MODEL].astype(BF16),
        "fox_forget": (wf_pad, wf.T, fox_b_f[0].reshape(1, FOX_HEADS), fox_b_f[0].reshape(FOX_HEADS, 1)),
        "fox_wout": fox_w_out[0].astype(BF16),
        "diff_wqkv": diff_w_in[0].astype(BF16),
        "diff_lambda": diff_lambda[0],
        "diff_g": diff_subln_g[0].reshape(1, 2 * DIFF_HD),
        "diff_wout": diff_w_out[0].astype(BF16),
        "ffn_win": [ffn_w_in[i].astype(BF16) for i in range(DEPTH)],
        "ffn_wout": [ffn_w_out[i].astype(BF16) for i in range(DEPTH)],
        "ln_g": [[ln_g[i, j].reshape(1, D_MODEL) for j in range(2)] for i in range(DEPTH)],
        "ln_b": [[ln_b[i, j].reshape(1, D_MODEL) for j in range(2)] for i in range(DEPTH)],
    }


def kernel(x_prompt, x_sample, cache_fox_k, cache_fox_v, cache_fox_logf, cache_diff_k, cache_diff_v, fox_w_in, fox_b_f, fox_w_out, diff_w_in, diff_lambda, diff_subln_g, diff_w_out, ffn_w_in, ffn_w_out, ln_g, ln_b):
    wts = _prepare_weights(fox_w_in, fox_b_f, fox_w_out, diff_w_in, diff_lambda, diff_subln_g, diff_w_out,
                           ffn_w_in, ffn_w_out, ln_g, ln_b)
    y_p, fk_p, fv_p, lf_p, dk_p, dv_p = _run_stream(x_prompt, None, wts, attn_blk=512)
    past = (cache_fox_k[0], cache_fox_v[0], cache_fox_logf[0], cache_diff_k[0], cache_diff_v[0])
    y_s, fk_s, fv_s, lf_s, dk_s, dv_s = _run_stream(x_sample, past, wts, attn_blk=512)
    return (y_p, y_s, fk_p, fv_p, lf_p, dk_p, dv_p, fk_s, fv_s, lf_s, dk_s, dv_s)
```

```python
import functools
import math

import jax
import jax.numpy as jnp
from jax import lax
from jax.experimental import pallas as pl
from jax.experimental.pallas import tpu as pltpu

F32 = jnp.float32
BF16 = jnp.bfloat16

D_MODEL = 1024
DEPTH = 2
FOX_HEADS = 16
FOX_HD = 64
DIFF_HEADS = 8
DIFF_HD = 64
D_FF = 2816
CHUNK = 64
DEEPNORM_ALPHA = (2.0 * DEPTH) ** 0.25
LN_EPS = 1e-5
NEG_INF = -1e30
LOG2E = 1.4426950408889634

LANES = 128
HEAD_PAIRS = D_MODEL // LANES
ATTN_BLK = 512
VMEM_LIMIT = 56 * 1024 * 1024


def _cparams(n_axes):
    return pltpu.CompilerParams(dimension_semantics=("arbitrary",) * n_axes,
                                vmem_limit_bytes=VMEM_LIMIT)


def _log_sigmoid(z):
    return jnp.minimum(z, 0.0) - jnp.log1p(jnp.exp(-jnp.abs(z)))


def _layernorm(z, g, b):
    mu = jnp.mean(z, axis=-1, keepdims=True)
    zc = z - mu
    var = jnp.mean(zc * zc, axis=-1, keepdims=True)
    return zc * lax.rsqrt(var + LN_EPS) * g + b


def _dot(a, b):
    return jnp.dot(a, b, preferred_element_type=F32)


def _dot_nt(a, b):
    return lax.dot_general(a, b, (((1,), (1,)), ((), ())), preferred_element_type=F32)


def _split3(x):
    hi = x.astype(BF16).astype(F32)
    r = x - hi
    mid = r.astype(BF16).astype(F32)
    lo = (r - mid).astype(BF16).astype(F32)
    return hi, mid, lo


def _proj_body(*refs, q_scale, with_forget):
    if with_forget:
        (x_ref, w_ref, wvt_ref, wf_ref, wft_ref, bf_ref, bft_ref,
         qkv_ref, k_ref, v_ref, vt_ref, lf_ref, lfp_ref, lft_ref) = refs
    else:
        x_ref, w_ref, wvt_ref, qkv_ref, k_ref, v_ref, vt_ref, ka_ref = refs
    j = pl.program_id(1)
    tm = x_ref.shape[0]
    xb = x_ref[...].astype(BF16)
    acc = _dot(xb, w_ref[...])

    def store_heads(val):
        vb = val.astype(BF16)
        for p in range(HEAD_PAIRS):
            qkv_ref[0, p] = vb[:, p * LANES:(p + 1) * LANES]
        return vb

    @pl.when(j == 0)
    def _():
        store_heads(acc * q_scale)
        if with_forget:
            lfp = _log_sigmoid(_dot(xb, wf_ref[...]) + bf_ref[...])
            lfp_ref[...] = lfp
            lf_ref[...] = lfp[:, :FOX_HEADS]
            lft_ref[...] = _log_sigmoid(_dot_nt(wft_ref[...], xb) + bft_ref[...])

    @pl.when(j == 1)
    def _():
        kb = store_heads(acc)
        k_ref[...] = acc
        if not with_forget:
            lane = lax.broadcasted_iota(jnp.int32, (tm, LANES), 1)
            r_local = lax.broadcasted_iota(jnp.int32, (tm, LANES), 0).astype(F32)
            for h in range(DIFF_HEADS):
                hi, mid, lo = _split3(r_local * (2.0 ** -(h + 1) * LOG2E))
                cols = jnp.where(lane == 0, hi, jnp.where(lane == 1, mid, jnp.where(lane == 2, lo, 0.0)))
                ka_ref[h, :, 0:LANES] = kb[:, h * LANES:(h + 1) * LANES]
                ka_ref[h, :, LANES:2 * LANES] = cols.astype(BF16)

    @pl.when(j == 2)
    def _():
        store_heads(acc)
        v_ref[...] = acc
        vt = _dot_nt(wvt_ref[...], xb).astype(BF16)
        for p in range(HEAD_PAIRS):
            vt_ref[p, 0] = vt[p * LANES:(p + 1) * LANES, :]


def _qkv_proj(x2d, w_qkv, w_vt, q_scale, forget=None):
    m = x2d.shape[0]
    tm = min(ATTN_BLK, m)
    nblk = m // tm
    row = pl.BlockSpec((tm, D_MODEL), lambda i, j: (i, 0))
    const = lambda shape: pl.BlockSpec(shape, lambda i, j: (0,) * len(shape))
    in_specs = [row, pl.BlockSpec((D_MODEL, D_MODEL), lambda i, j: (0, j)), const((D_MODEL, D_MODEL))]
    out_shape = [jax.ShapeDtypeStruct((3, HEAD_PAIRS, m, LANES), BF16),
                 jax.ShapeDtypeStruct((m, D_MODEL), F32),
                 jax.ShapeDtypeStruct((m, D_MODEL), F32),
                 jax.ShapeDtypeStruct((HEAD_PAIRS, nblk, LANES, tm), BF16)]
    out_specs = [pl.BlockSpec((1, HEAD_PAIRS, tm, LANES), lambda i, j: (j, 0, i, 0)),
                 row, row,
                 pl.BlockSpec((HEAD_PAIRS, 1, LANES, tm), lambda i, j: (0, i, 0, 0))]
    args = [x2d, w_qkv, w_vt]
    if forget is not None:
        wf, wft, bf, bft = forget
        in_specs += [const((D_MODEL, LANES)), const((FOX_HEADS, D_MODEL)), const((1, LANES)), const((FOX_HEADS, 1))]
        out_shape += [jax.ShapeDtypeStruct((m, FOX_HEADS), F32),
                      jax.ShapeDtypeStruct((m, LANES), F32),
                      jax.ShapeDtypeStruct((FOX_HEADS, m), F32)]
        out_specs += [pl.BlockSpec((tm, FOX_HEADS), lambda i, j: (i, 0)),
                      pl.BlockSpec((tm, LANES), lambda i, j: (i, 0)),
                      pl.BlockSpec((FOX_HEADS, tm), lambda i, j: (0, i))]
        args += [wf, wft, bf, bft]
    else:
        out_shape.append(jax.ShapeDtypeStruct((DIFF_HEADS, m, 2 * LANES), BF16))
        out_specs.append(pl.BlockSpec((DIFF_HEADS, tm, 2 * LANES), lambda i, j: (0, i, 0)))
    return pl.pallas_call(
        functools.partial(_proj_body, q_scale=q_scale, with_forget=forget is not None),
        grid=(nblk, 3), in_specs=in_specs, out_specs=out_specs, out_shape=out_shape,
        compiler_params=_cparams(2), name="qkv_proj_f" if forget is not None else "qkv_proj",
    )(*args)


def _fox_keys_body(lf_ref, k_ref, ka_ref, fs_ref, carry_ref, *, blk):
    i = pl.program_id(1)

    @pl.when(i == 0)
    def _():
        carry_ref[...] = jnp.zeros_like(carry_ref)

    row = lax.broadcasted_iota(jnp.int32, (blk, blk), 0)
    col = lax.broadcasted_iota(jnp.int32, (blk, blk), 1)
    tri = (col <= row).astype(BF16)
    hi, mid, lo = _split3(lf_ref[...])
    f_loc = _dot(tri, hi.astype(BF16)) + _dot(tri, mid.astype(BF16)) + _dot(tri, lo.astype(BF16))
    fs_ref[0] = carry_ref[...]
    carry_ref[...] = carry_ref[...] + f_loc[blk - 1:blk, :]

    bhi, bmid, blo = (t.astype(BF16) for t in _split3(f_loc * (-LOG2E)))
    sr = lax.broadcasted_iota(jnp.int32, (LANES, LANES), 0)
    sc = lax.broadcasted_iota(jnp.int32, (LANES, LANES), 1)
    for p in range(HEAD_PAIRS):
        def pick(term):
            return (((sr == 2 * p) & (sc == term)) | ((sr == 2 * p + 1) & (sc == 3 + term))).astype(BF16)
        cols = _dot(bhi, pick(0)) + _dot(bmid, pick(1)) + _dot(blo, pick(2))
        ka_ref[p, :, 0:LANES] = k_ref[0, p]
        ka_ref[p, :, LANES:2 * LANES] = cols.astype(BF16)


def _fox_keys(lf_pad, qkvh, n_seq, t, blk):
    m = n_seq * t
    nblk = t // blk
    return pl.pallas_call(
        functools.partial(_fox_keys_body, blk=blk),
        grid=(n_seq, nblk),
        in_specs=[pl.BlockSpec((blk, LANES), lambda b, i: (b * nblk + i, 0)),
                  pl.BlockSpec((1, HEAD_PAIRS, blk, LANES), lambda b, i: (1, 0, b * nblk + i, 0))],
        out_specs=[pl.BlockSpec((HEAD_PAIRS, blk, 2 * LANES), lambda b, i: (0, b * nblk + i, 0)),
                   pl.BlockSpec((1, 1, LANES), lambda b, i: (b * nblk + i, 0, 0))],
        out_shape=[jax.ShapeDtypeStruct((HEAD_PAIRS, m, 2 * LANES), BF16),
                   jax.ShapeDtypeStruct((n_seq * nblk, 1, LANES), F32)],
        scratch_shapes=[pltpu.VMEM((1, LANES), F32)],
        compiler_params=_cparams(2), name="fox_keys",
    )(lf_pad, qkvh)


def _cumsum_body(x_ref, o_ref, carry_ref, *, blk):
    i = pl.program_id(1)

    @pl.when(i == 0)
    def _():
        carry_ref[...] = jnp.zeros_like(carry_ref)

    hi, mid, lo = _split3(x_ref[...])
    row = lax.broadcasted_iota(jnp.int32, (blk, blk), 0)
    col = lax.broadcasted_iota(jnp.int32, (blk, blk), 1)
    tri = (row <= col).astype(BF16)
    cs = _dot(hi.astype(BF16), tri) + _dot(mid.astype(BF16), tri) + _dot(lo.astype(BF16), tri) + carry_ref[...]
    o_ref[0] = cs
    carry_ref[...] = cs[:, blk - 1:blk]


def _cumsum_time(lft, n_seq, blk):
    h, total = lft.shape
    nblk = total // n_seq // blk
    return pl.pallas_call(
        functools.partial(_cumsum_body, blk=blk),
        grid=(n_seq, nblk),
        in_specs=[pl.BlockSpec((h, blk), lambda b, i: (0, b * nblk + i))],
        out_specs=pl.BlockSpec((1, h, blk), lambda b, i: (b * nblk + i, 0, 0)),
        out_shape=jax.ShapeDtypeStruct((n_seq * nblk, h, blk), F32),
        scratch_shapes=[pltpu.VMEM((h, 1), F32)],
        compiler_params=_cparams(2), name="cumsum_time",
    )(lft)


def _augment_queries(q, blk, ones_first_lane):
    lane = lax.broadcasted_iota(jnp.int32, (blk, LANES), 1)
    lo_mask = lane < LANES // 2
    zero = jnp.zeros_like(q)
    out = []
    for u, part in enumerate((jnp.where(lo_mask, q, zero), jnp.where(lo_mask, zero, q))):
        first = ones_first_lane[u]
        ones = jnp.where((lane >= first) & (lane < first + 3), 1.0, 0.0).astype(BF16)
        out.append(jnp.concatenate([part, ones], axis=1))
    return out


def _softmax_step(s, c, vt, m_ref, l_ref, u):
    m_old = m_ref[u]
    m_new = jnp.maximum(m_old, jnp.max(s, axis=0, keepdims=True) + c)
    p = jnp.exp2(s - (m_new - c))
    alpha = jnp.exp2(m_old - m_new)
    l_ref[u] = alpha * l_ref[u] + jnp.sum(p, axis=0, keepdims=True)
    m_ref[u] = m_new
    return alpha, _dot(vt, p.astype(BF16))


def _fox_attn_body(fs_ref, q_ref, ka_ref, vt_ref, o_ref, acc_ref, m_ref, l_ref, *, blk, nq):
    seq = pl.program_id(0)
    pair = pl.program_id(1)
    qi = pl.program_id(2)
    qa = _augment_queries(q_ref[0, 0], blk, (0, 3))
    acc_ref[...] = jnp.zeros_like(acc_ref)
    m_ref[...] = jnp.full_like(m_ref, NEG_INF)
    l_ref[...] = jnp.zeros_like(l_ref)
    half = FOX_HD

    def block(kj, diagonal):
        start = pl.multiple_of(kj * blk, blk)
        ka = ka_ref[0, pl.ds(start, blk), :]
        vt = vt_ref[0, kj]
        if diagonal:
            key = lax.broadcasted_iota(jnp.int32, (blk, blk), 0)
            query = lax.broadcasted_iota(jnp.int32, (blk, blk), 1)
            causal = key <= query
        for h in range(2):
            head = 2 * pair + h
            f_q = jnp.full((1, blk), fs_ref[(seq * nq + qi) * FOX_HEADS + head], F32)
            f_k = jnp.full((1, blk), fs_ref[(seq * nq + kj) * FOX_HEADS + head], F32)
            c = (f_q - f_k) * LOG2E
            s = _dot_nt(ka, qa[h])
            if diagonal:
                s = jnp.where(causal, s, NEG_INF)
            alpha, pv = _softmax_step(s, c, vt, m_ref, l_ref, h)
            rows = slice(h * half, (h + 1) * half)
            acc_ref[rows, :] = alpha * acc_ref[rows, :] + pv[rows, :]

    def loop_body(kj, carry):
        block(kj, False)
        return carry

    lax.fori_loop(0, qi, loop_body, 0)
    block(qi, True)
    sub = lax.broadcasted_iota(jnp.int32, (LANES, blk), 0)
    o_t = acc_ref[...] / jnp.where(sub < half, l_ref[0], l_ref[1])
    o_ref[...] = o_t.T.astype(BF16)


def _fox_attention(qkvh, kaug, vt, f_start, n_seq, t, blk):
    m = n_seq * t
    nq = t // blk
    grid_spec = pltpu.PrefetchScalarGridSpec(
        num_scalar_prefetch=1,
        grid=(n_seq, HEAD_PAIRS, nq),
        in_specs=[pl.BlockSpec((1, 1, blk, LANES), lambda b, p, i, fs: (0, p, b * nq + i, 0)),
                  pl.BlockSpec((1, t, 2 * LANES), lambda b, p, i, fs: (p, b, 0)),
                  pl.BlockSpec((1, nq, LANES, blk), lambda b, p, i, fs: (p, b, 0, 0))],
        out_specs=pl.BlockSpec((blk, LANES), lambda b, p, i, fs: (b * nq + i, p)),
        scratch_shapes=[pltpu.VMEM((LANES, blk), F32),
                        pltpu.VMEM((2, 1, blk), F32),
                        pltpu.VMEM((2, 1, blk), F32)])
    return pl.pallas_call(
        functools.partial(_fox_attn_body, blk=blk, nq=nq),
        grid_spec=grid_spec,
        out_shape=jax.ShapeDtypeStruct((m, D_MODEL), BF16),
        compiler_params=_cparams(3), name="fox_attention",
    )(f_start, qkvh, kaug, vt)


def _diff_lambda(lam_ref, lam_init):
    lp = lam_ref[...]
    a = jnp.sum(lp[0:1] * lp[1:2], axis=1, keepdims=True)
    b = jnp.sum(lp[2:3] * lp[3:4], axis=1, keepdims=True)
    return jnp.exp(a) - jnp.exp(b) + lam_init


def _subln(o, g, lam_init):
    o = o * lax.rsqrt(jnp.mean(o * o, axis=-1, keepdims=True) + LN_EPS)
    return o * g * (1.0 - lam_init)


def _diff_attn_body(q_ref, ka_ref, vt_ref, lam_ref, g_ref, o_ref, acc_ref, m_ref, l_ref, *, blk, lam_init):
    head = pl.program_id(1)
    qi = pl.program_id(2)
    qa = _augment_queries(q_ref[0, 0], blk, (0, 0))
    acc_ref[...] = jnp.zeros_like(acc_ref)
    m_ref[...] = jnp.full_like(m_ref, NEG_INF)
    l_ref[...] = jnp.zeros_like(l_ref)
    slope = lax.bitcast_convert_type(jnp.full((1, blk), 126 - head, jnp.int32) << 23, F32) * LOG2E

    def block(kj, diagonal):
        start = pl.multiple_of(kj * blk, blk)
        ka = ka_ref[0, pl.ds(start, blk), :]
        vt = vt_ref[0, kj]
        c = slope * jnp.full((1, blk), (kj - qi) * blk, jnp.int32).astype(F32)
        if diagonal:
            key = lax.broadcasted_iota(jnp.int32, (blk, blk), 0)
            query = lax.broadcasted_iota(jnp.int32, (blk, blk), 1)
            visible = (key // CHUNK) <= (query // CHUNK)
            ahead = slope * (-2.0 * jnp.maximum(key - query, 0).astype(F32))
        for u in range(2):
            s = _dot_nt(ka, qa[u])
            if diagonal:
                s = jnp.where(visible, s + ahead, NEG_INF)
            alpha, pv = _softmax_step(s, c, vt, m_ref, l_ref, u)
            acc_ref[u] = alpha * acc_ref[u] + pv

    def loop_body(kj, carry):
        block(kj, False)
        return carry

    lax.fori_loop(0, qi, loop_body, 0)
    block(qi, True)
    lam = _diff_lambda(lam_ref, lam_init)
    o_t = acc_ref[0] / l_ref[0] - lam * (acc_ref[1] / l_ref[1])
    o_ref[...] = _subln(o_t.T, g_ref[...], lam_init).astype(BF16)


def _diff_attention(qkvh, kaug, vt, lam_params, subln_g, n_seq, t, blk, lam_init):
    m = n_seq * t
    nq = t // blk
    return pl.pallas_call(
        functools.partial(_diff_attn_body, blk=blk, lam_init=lam_init),
        grid=(n_seq, DIFF_HEADS, nq),
        in_specs=[pl.BlockSpec((1, 1, blk, LANES), lambda b, p, i: (0, p, b * nq + i, 0)),
                  pl.BlockSpec((1, t, 2 * LANES), lambda b, p, i: (p, b, 0)),
                  pl.BlockSpec((1, nq, LANES, blk), lambda b, p, i: (p, b, 0, 0)),
                  pl.BlockSpec((4, DIFF_HD), lambda b, p, i: (0, 0)),
                  pl.BlockSpec((1, LANES), lambda b, p, i: (0, 0))],
        out_specs=pl.BlockSpec((blk, LANES), lambda b, p, i: (b * nq + i, p)),
        out_shape=jax.ShapeDtypeStruct((m, D_MODEL), BF16),
        scratch_shapes=[pltpu.VMEM((2, LANES, blk), F32),
                        pltpu.VMEM((2, 1, blk), F32),
                        pltpu.VMEM((2, 1, blk), F32)],
        compiler_params=_cparams(3), name="diff_attention",
    )(qkvh, kaug, vt, lam_params, subln_g)


def _decode_attn_body(*refs, mode, t_new, past_len, blk, nk, lam_init):
    if mode == "fox":
        q_ref, kn_ref, vn_ref, kc_ref, vc_ref, f_ref, o_ref, qbd_ref, acc_ref, m_ref, l_ref = refs
    else:
        q_ref, kn_ref, vn_ref, kc_ref, vc_ref, lam_ref, g_ref, o_ref, qbd_ref, acc_ref, m_ref, l_ref = refs
    kj = pl.program_id(1)
    groups = D_MODEL // FOX_HD
    rows = groups * t_new
    row1 = lax.broadcasted_iota(jnp.int32, (rows, 1), 0)

    def gather_pairs(ref):
        return jnp.concatenate([ref[0, p] for p in range(HEAD_PAIRS)], axis=1)

    @pl.when(kj == 0)
    def _():
        q = gather_pairs(q_ref)
        qt = jnp.concatenate([q] * groups, axis=0)
        r = lax.broadcasted_iota(jnp.int32, (rows, D_MODEL), 0)
        c = lax.broadcasted_iota(jnp.int32, (rows, D_MODEL), 1)
        qbd_ref[...] = jnp.where((r // t_new) == (c // FOX_HD), qt, jnp.zeros_like(qt))
        acc_ref[...] = jnp.zeros_like(acc_ref)
        m_ref[...] = jnp.full_like(m_ref, NEG_INF)
        l_ref[...] = jnp.zeros_like(l_ref)

    if mode == "fox":
        f_last = f_ref[nk - 1][:, blk - 1:blk]

        def head_rows(x):
            n = x.shape[1]
            return jnp.concatenate(
                [jnp.broadcast_to(x[h:h + 1, :], (t_new, n)) for h in range(FOX_HEADS)], axis=0)
    else:
        slope = jnp.exp2(-(row1 // (2 * t_new) + 1).astype(F32)) * LOG2E

    def update(s, v):
        m_old = m_ref[...]
        m_new = jnp.maximum(m_old, jnp.max(s, axis=1, keepdims=True))
        p = jnp.exp2(s - m_new)
        a = jnp.exp2(m_old - m_new)
        l_ref[...] = a * l_ref[...] + jnp.sum(p, axis=1, keepdims=True)
        m_ref[...] = m_new
        acc_ref[...] = a * acc_ref[...] + _dot(p.astype(BF16), v)

    kc = kc_ref[0].astype(BF16)
    vc = vc_ref[0].astype(BF16)
    s = _dot_nt(qbd_ref[...], kc)
    if mode == "fox":
        s = s + head_rows((f_last - f_ref[kj]) * LOG2E)
    else:
        col = lax.broadcasted_iota(jnp.int32, (1, blk), 1)
        s = s + slope * (col + (kj * blk - past_len)).astype(F32)
    update(s, vc)

    @pl.when(kj == nk - 1)
    def _():
        kn = gather_pairs(kn_ref)
        vn = gather_pairs(vn_ref)
        s = _dot_nt(qbd_ref[...], kn)
        tq = lax.broadcasted_iota(jnp.int32, (rows, t_new), 0) % t_new
        tk = lax.broadcasted_iota(jnp.int32, (rows, t_new), 1)
        if mode == "fox":
            s = s + head_rows((f_last - f_ref[nk][:, :t_new]) * LOG2E)
            visible = tk <= tq
        else:
            s = s + slope * (tq - jnp.abs(tq - tk)).astype(F32)
            visible = ((tk + past_len) // CHUNK) <= ((tq + past_len) // CHUNK)
        update(jnp.where(visible, s, NEG_INF), vn)

        r = lax.broadcasted_iota(jnp.int32, (rows, D_MODEL), 0)
        c = lax.broadcasted_iota(jnp.int32, (rows, D_MODEL), 1)
        inv_l = 1.0 / l_ref[...]
        if mode == "fox":
            own = (r // t_new) == (c // FOX_HD)
            coef = inv_l
        else:
            own = (r // (2 * t_new)) == (c // LANES)
            lam = _diff_lambda(lam_ref, lam_init)
            coef = jnp.where((row1 // t_new) % 2 == 0, inv_l, -lam * inv_l)
        w = jnp.where(own, acc_ref[...] * coef, 0.0)
        o = w[0:t_new]
        for gi in range(1, groups):
            o = o + w[gi * t_new:(gi + 1) * t_new]
        if mode == "diff":
            g = g_ref[...]
            o = jnp.concatenate(
                [_subln(o[:, h * LANES:(h + 1) * LANES], g, lam_init) for h in range(DIFF_HEADS)], axis=1)
        o_ref[0] = o.astype(BF16)


def _decode_attention(qkvh, cache_k, cache_v, extra, mode, n_seq, t_new, blk, lam_init=0.0):
    past_len = cache_k.shape[1]
    nk = past_len // blk
    rows = (D_MODEL // FOX_HD) * t_new
    new_spec = lambda s: pl.BlockSpec((1, HEAD_PAIRS, t_new, LANES), lambda b, j: (s, 0, b, 0))
    cache_spec = pl.BlockSpec((1, blk, D_MODEL), lambda b, j: (b, j, 0))
    in_specs = [new_spec(0), new_spec(1), new_spec(2), cache_spec, cache_spec]
    if mode == "fox":
        in_specs.append(pl.BlockSpec((nk + 1, FOX_HEADS, blk), lambda b, j: (b, 0, 0)))
    else:
        in_specs += [pl.BlockSpec((4, DIFF_HD), lambda b, j: (0, 0)),
                     pl.BlockSpec((1, LANES), lambda b, j: (0, 0))]
    out = pl.pallas_call(
        functools.partial(_decode_attn_body, mode=mode, t_new=t_new, past_len=past_len, blk=blk, nk=nk,
                          lam_init=lam_init),
        grid=(n_seq, nk),
        in_specs=in_specs,
        out_specs=pl.BlockSpec((1, t_new, D_MODEL), lambda b, j: (b, 0, 0)),
        out_shape=jax.ShapeDtypeStruct((n_seq, t_new, D_MODEL), BF16),
        scratch_shapes=[pltpu.VMEM((rows, D_MODEL), BF16),
                        pltpu.VMEM((rows, D_MODEL), F32),
                        pltpu.VMEM((rows, 1), F32),
                        pltpu.VMEM((rows, 1), F32)],
        compiler_params=_cparams(2), name="decode_attention_" + mode,
    )(qkvh, qkvh, qkvh, cache_k, cache_v, *extra)
    return out.reshape(n_seq * t_new, D_MODEL)


def _oproj_ln_body(x_ref, o_ref, w_ref, g_ref, b_ref, y_ref):
    y = _dot(o_ref[...], w_ref[...])
    y_ref[...] = _layernorm(DEEPNORM_ALPHA * x_ref[...] + y, g_ref[...], b_ref[...])


def _oproj_ln(x2d, o2d, w, g, b, tm=512):
    m = x2d.shape[0]
    tm = min(tm, m)
    row = pl.BlockSpec((tm, D_MODEL), lambda i: (i, 0))
    vec = pl.BlockSpec((1, D_MODEL), lambda i: (0, 0))
    return pl.pallas_call(
        _oproj_ln_body, grid=(m // tm,),
        in_specs=[row, row, pl.BlockSpec((D_MODEL, D_MODEL), lambda i: (0, 0)), vec, vec],
        out_specs=row, out_shape=jax.ShapeDtypeStruct((m, D_MODEL), F32),
        compiler_params=_cparams(1), name="oproj_ln",
    )(x2d, o2d, w, g, b)


def _ffn_ln_body(x_ref, win_ref, wout_ref, g_ref, b_ref, y_ref):
    x = x_ref[...]
    h = _dot(x.astype(BF16), win_ref[...])
    gate = h[:, :D_FF]
    up = h[:, D_FF:]
    act = (gate * jax.nn.sigmoid(gate) * up).astype(BF16)
    y = _dot(act, wout_ref[...])
    y_ref[...] = _layernorm(DEEPNORM_ALPHA * x + y, g_ref[...], b_ref[...])


def _ffn_ln(x2d, w_in, w_out, g, b, tm=256):
    m = x2d.shape[0]
    tm = min(tm, m)
    row = pl.BlockSpec((tm, D_MODEL), lambda i: (i, 0))
    vec = pl.BlockSpec((1, D_MODEL), lambda i: (0, 0))
    return pl.pallas_call(
        _ffn_ln_body, grid=(m // tm,),
        in_specs=[row,
                  pl.BlockSpec((D_MODEL, 2 * D_FF), lambda i: (0, 0)),
                  pl.BlockSpec((D_FF, D_MODEL), lambda i: (0, 0)),
                  vec, vec],
        out_specs=row, out_shape=jax.ShapeDtypeStruct((m, D_MODEL), F32),
        compiler_params=_cparams(1), name="ffn_ln",
    )(x2d, w_in, w_out, g, b)


def _run_stream(x, past, wts):
    n_seq, t, _ = x.shape
    m = n_seq * t
    blk = ATTN_BLK
    x2d = x.reshape(m, D_MODEL)
    lam_init = 0.8 - 0.6 * math.exp(-0.3 * 1)

    qkvh, fk, fv, vt, logf, logf_pad, logft = _qkv_proj(
        x2d, wts["fox_wqkv"], wts["fox_wvt"], FOX_HD ** -0.5 * LOG2E, forget=wts["fox_forget"])
    if past is None:
        kaug, f_start = _fox_keys(logf_pad, qkvh, n_seq, t, blk)
        o = _fox_attention(qkvh, kaug, vt, f_start[:, 0, :FOX_HEADS].reshape(-1), n_seq, t, blk)
    else:
        past_k, past_v, past_lf = past[0], past[1], past[2]
        past_len = past_k.shape[1]
        lf_all = jnp.concatenate(
            [jnp.transpose(past_lf, (2, 0, 1)),
             logft.reshape(FOX_HEADS, n_seq, t),
             jnp.zeros((FOX_HEADS, n_seq, blk - t), F32)], axis=2)
        fcum = _cumsum_time(lf_all.reshape(FOX_HEADS, n_seq * (past_len + blk)), n_seq, blk)
        o = _decode_attention(qkvh, past_k.reshape(n_seq, past_len, D_MODEL),
                              past_v.reshape(n_seq, past_len, D_MODEL), (fcum,), "fox", n_seq, t, blk)
    x2d = _oproj_ln(x2d, o, wts["fox_wout"], wts["ln_g"][0][0], wts["ln_b"][0][0])
    x2d = _ffn_ln(x2d, wts["ffn_win"][0], wts["ffn_wout"][0], wts["ln_g"][0][1], wts["ln_b"][0][1])

    qkvh, dk, dv, vt, kaug = _qkv_proj(x2d, wts["diff_wqkv"], wts["diff_wvt"], DIFF_HD ** -0.5 * LOG2E)
    if past is None:
        o = _diff_attention(qkvh, kaug, vt, wts["diff_lambda"], wts["diff_g"], n_seq, t, blk, lam_init)
    else:
        past_k, past_v = past[3], past[4]
        past_len = past_k.shape[1]
        o = _decode_attention(qkvh, past_k.reshape(n_seq, past_len, D_MODEL),
                              past_v.reshape(n_seq, past_len, D_MODEL),
                              (wts["diff_lambda"], wts["diff_g"]), "diff", n_seq, t, blk, lam_init)
    x2d = _oproj_ln(x2d, o, wts["diff_wout"], wts["ln_g"][1][0], wts["ln_b"][1][0])
    x2d = _ffn_ln(x2d, wts["ffn_win"][1], wts["ffn_wout"][1], wts["ln_g"][1][1], wts["ln_b"][1][1])

    return (x2d.reshape(n_seq, t, D_MODEL),
            fk.reshape(1, n_seq, t, FOX_HEADS, FOX_HD), fv.reshape(1, n_seq, t, FOX_HEADS, FOX_HD),
            logf.reshape(1, n_seq, t, FOX_HEADS),
            dk.reshape(1, n_seq, t, DIFF_HEADS, 2 * DIFF_HD), dv.reshape(1, n_seq, t, DIFF_HEADS, 2 * DIFF_HD))


def _prepare_weights(fox_w_in, fox_b_f, fox_w_out, diff_w_in, diff_lambda, diff_subln_g, diff_w_out,
                     ffn_w_in, ffn_w_out, ln_g, ln_b):
    wf = fox_w_in[0][:, 3 * D_MODEL:].astype(BF16)
    wf_pad = jnp.pad(wf, ((0, 0), (0, LANES - FOX_HEADS)))
    bf_pad = jnp.pad(fox_b_f[0].reshape(1, FOX_HEADS), ((0, 0), (0, LANES - FOX_HEADS)))
    return {
        "fox_wqkv": fox_w_in[0][:, :3 * D_MODEL].astype(BF16),
        "fox_wvt": fox_w_in[0][:, 2 * D_MODEL:3 * D_MODEL].T.astype(BF16),
        "fox_forget": (wf_pad, wf.T, bf_pad, fox_b_f[0].reshape(FOX_HEADS, 1)),
        "fox_wout": fox_w_out[0].astype(BF16),
        "diff_wqkv": diff_w_in[0].astype(BF16),
        "diff_wvt": diff_w_in[0][:, 2 * D_MODEL:].T.astype(BF16),
        "diff_lambda": diff_lambda[0],
        "diff_g": diff_subln_g[0].reshape(1, 2 * DIFF_HD),
        "diff_wout": diff_w_out[0].astype(BF16),
        "ffn_win": [ffn_w_in[i].astype(BF16) for i in range(DEPTH)],
        "ffn_wout": [ffn_w_out[i].astype(BF16) for i in range(DEPTH)],
        "ln_g": [[ln_g[i, j].reshape(1, D_MODEL) for j in range(2)] for i in range(DEPTH)],
        "ln_b": [[ln_b[i, j].reshape(1, D_MODEL) for j in range(2)] for i in range(DEPTH)],
    }


def kernel(x_prompt, x_sample, cache_fox_k, cache_fox_v, cache_fox_logf, cache_diff_k, cache_diff_v, fox_w_in, fox_b_f, fox_w_out, diff_w_in, diff_lambda, diff_subln_g, diff_w_out, ffn_w_in, ffn_w_out, ln_g, ln_b):
    wts = _prepare_weights(fox_w_in, fox_b_f, fox_w_out, diff_w_in, diff_lambda, diff_subln_g, diff_w_out,
                           ffn_w_in, ffn_w_out, ln_g, ln_b)
    y_p, fk_p, fv_p, lf_p, dk_p, dv_p = _run_stream(x_prompt, None, wts)
    past = (cache_fox_k[0], cache_fox_v[0], cache_fox_logf[0], cache_diff_k[0], cache_diff_v[0])
    y_s, fk_s, fv_s, lf_s, dk_s, dv_s = _run_stream(x_sample, past, wts)
    return (y_p, y_s, fk_p, fv_p, lf_p, dk_p, dv_p, fk_s, fv_s, lf_s, dk_s, dv_s)
```

```python
import functools
import math

import jax
import jax.numpy as jnp
from jax import lax
from jax.experimental import pallas as pl
from jax.experimental.pallas import tpu as pltpu

F32 = jnp.float32
BF16 = jnp.bfloat16

D_MODEL = 1024
DEPTH = 2
FOX_HEADS = 16
FOX_HD = 64
DIFF_HEADS = 8
DIFF_HD = 64
D_FF = 2816
CHUNK = 64
DEEPNORM_ALPHA = (2.0 * DEPTH) ** 0.25
LN_EPS = 1e-5
NEG_INF = -1e30
LOG2E = 1.4426950408889634

LANES = 128
HEAD_PAIRS = D_MODEL // LANES
ATTN_BLK = 512
VMEM_LIMIT = 56 * 1024 * 1024


def _cparams(n_axes):
    return pltpu.CompilerParams(dimension_semantics=("arbitrary",) * n_axes,
                                vmem_limit_bytes=VMEM_LIMIT)


def _log_sigmoid(z):
    return jnp.minimum(z, 0.0) - jnp.log1p(jnp.exp(-jnp.abs(z)))


def _layernorm(z, g, b):
    mu = jnp.mean(z, axis=-1, keepdims=True)
    zc = z - mu
    var = jnp.mean(zc * zc, axis=-1, keepdims=True)
    return zc * lax.rsqrt(var + LN_EPS) * g + b


def _dot(a, b):
    return jnp.dot(a, b, preferred_element_type=F32)


def _dot_nt(a, b):
    return lax.dot_general(a, b, (((1,), (1,)), ((), ())), preferred_element_type=F32)


def _split3(x):
    hi = x.astype(BF16).astype(F32)
    r = x - hi
    mid = r.astype(BF16).astype(F32)
    lo = (r - mid).astype(BF16).astype(F32)
    return hi, mid, lo


def _proj_body(*refs, q_scale, with_forget):
    if with_forget:
        (x_ref, w_ref, wvt_ref, wf_ref, wft_ref, bf_ref, bft_ref,
         qkv_ref, k_ref, v_ref, vt_ref, lf_ref, lfp_ref, lft_ref) = refs
    else:
        x_ref, w_ref, wvt_ref, qkv_ref, k_ref, v_ref, vt_ref, ka_ref = refs
    j = pl.program_id(1)
    tm = x_ref.shape[0]
    xb = x_ref[...].astype(BF16)
    acc = _dot(xb, w_ref[...])

    def store_heads(val):
        vb = val.astype(BF16)
        for p in range(HEAD_PAIRS):
            qkv_ref[0, p] = vb[:, p * LANES:(p + 1) * LANES]
        return vb

    @pl.when(j == 0)
    def _():
        store_heads(acc * q_scale)
        if with_forget:
            lfp = _log_sigmoid(_dot(xb, wf_ref[...]) + bf_ref[...])
            lfp_ref[...] = lfp
            lf_ref[...] = lfp[:, :FOX_HEADS]
            lft_ref[...] = _log_sigmoid(_dot_nt(wft_ref[...], xb) + bft_ref[...])

    @pl.when(j == 1)
    def _():
        kb = store_heads(acc)
        k_ref[...] = acc
        if not with_forget:
            lane = lax.broadcasted_iota(jnp.int32, (tm, LANES), 1)
            r_local = lax.broadcasted_iota(jnp.int32, (tm, LANES), 0).astype(F32)
            for h in range(DIFF_HEADS):
                hi, mid, lo = _split3(r_local * (2.0 ** -(h + 1) * LOG2E))
                cols = jnp.where(lane == 0, hi, jnp.where(lane == 1, mid, jnp.where(lane == 2, lo, 0.0)))
                ka_ref[h, :, 0:LANES] = kb[:, h * LANES:(h + 1) * LANES]
                ka_ref[h, :, LANES:2 * LANES] = cols.astype(BF16)

    @pl.when(j == 2)
    def _():
        store_heads(acc)
        v_ref[...] = acc
        vt = _dot_nt(wvt_ref[...], xb).astype(BF16)
        for p in range(HEAD_PAIRS):
            vt_ref[p, 0] = vt[p * LANES:(p + 1) * LANES, :]


def _qkv_proj(x2d, w_qkv, w_vt, q_scale, forget=None):
    m = x2d.shape[0]
    tm = min(ATTN_BLK, m)
    nblk = m // tm
    row = pl.BlockSpec((tm, D_MODEL), lambda i, j: (i, 0))
    const = lambda shape: pl.BlockSpec(shape, lambda i, j: (0,) * len(shape))
    in_specs = [row, pl.BlockSpec((D_MODEL, D_MODEL), lambda i, j: (0, j)), const((D_MODEL, D_MODEL))]
    out_shape = [jax.ShapeDtypeStruct((3, HEAD_PAIRS, m, LANES), BF16),
                 jax.ShapeDtypeStruct((m, D_MODEL), F32),
                 jax.ShapeDtypeStruct((m, D_MODEL), F32),
                 jax.ShapeDtypeStruct((HEAD_PAIRS, nblk, LANES, tm), BF16)]
    out_specs = [pl.BlockSpec((1, HEAD_PAIRS, tm, LANES), lambda i, j: (j, 0, i, 0)),
                 row, row,
                 pl.BlockSpec((HEAD_PAIRS, 1, LANES, tm), lambda i, j: (0, i, 0, 0))]
    args = [x2d, w_qkv, w_vt]
    if forget is not None:
        wf, wft, bf, bft = forget
        in_specs += [const((D_MODEL, LANES)), const((FOX_HEADS, D_MODEL)), const((1, LANES)), const((FOX_HEADS, 1))]
        out_shape += [jax.ShapeDtypeStruct((m, FOX_HEADS), F32),
                      jax.ShapeDtypeStruct((m, LANES), F32),
                      jax.ShapeDtypeStruct((FOX_HEADS, m), F32)]
        out_specs += [pl.BlockSpec((tm, FOX_HEADS), lambda i, j: (i, 0)),
                      pl.BlockSpec((tm, LANES), lambda i, j: (i, 0)),
                      pl.BlockSpec((FOX_HEADS, tm), lambda i, j: (0, i))]
        args += [wf, wft, bf, bft]
    else:
        out_shape.append(jax.ShapeDtypeStruct((DIFF_HEADS, m, 2 * LANES), BF16))
        out_specs.append(pl.BlockSpec((DIFF_HEADS, tm, 2 * LANES), lambda i, j: (0, i, 0)))
    return pl.pallas_call(
        functools.partial(_proj_body, q_scale=q_scale, with_forget=forget is not None),
        grid=(nblk, 3), in_specs=in_specs, out_specs=out_specs, out_shape=out_shape,
        compiler_params=_cparams(2), name="qkv_proj_f" if forget is not None else "qkv_proj",
    )(*args)


def _fox_keys_body(lf_ref, k_ref, ka_ref, fs_ref, carry_ref, *, blk):
    i = pl.program_id(1)

    @pl.when(i == 0)
    def _():
        carry_ref[...] = jnp.zeros_like(carry_ref)

    row = lax.broadcasted_iota(jnp.int32, (blk, blk), 0)
    col = lax.broadcasted_iota(jnp.int32, (blk, blk), 1)
    tri = (col <= row).astype(BF16)
    hi, mid, lo = _split3(lf_ref[...])
    f_loc = _dot(tri, hi.astype(BF16)) + _dot(tri, mid.astype(BF16)) + _dot(tri, lo.astype(BF16))
    fs_ref[0] = carry_ref[...]
    carry_ref[...] = carry_ref[...] + f_loc[blk - 1:blk, :]

    bhi, bmid, blo = (t.astype(BF16) for t in _split3(f_loc * (-LOG2E)))
    sr = lax.broadcasted_iota(jnp.int32, (LANES, LANES), 0)
    sc = lax.broadcasted_iota(jnp.int32, (LANES, LANES), 1)
    for p in range(HEAD_PAIRS):
        def pick(term):
            return (((sr == 2 * p) & (sc == term)) | ((sr == 2 * p + 1) & (sc == 3 + term))).astype(BF16)
        cols = _dot(bhi, pick(0)) + _dot(bmid, pick(1)) + _dot(blo, pick(2))
        ka_ref[p, :, 0:LANES] = k_ref[0, p]
        ka_ref[p, :, LANES:2 * LANES] = cols.astype(BF16)


def _fox_keys(lf_pad, qkvh, n_seq, t, blk):
    m = n_seq * t
    nblk = t // blk
    return pl.pallas_call(
        functools.partial(_fox_keys_body, blk=blk),
        grid=(n_seq, nblk),
        in_specs=[pl.BlockSpec((blk, LANES), lambda b, i: (b * nblk + i, 0)),
                  pl.BlockSpec((1, HEAD_PAIRS, blk, LANES), lambda b, i: (1, 0, b * nblk + i, 0))],
        out_specs=[pl.BlockSpec((HEAD_PAIRS, blk, 2 * LANES), lambda b, i: (0, b * nblk + i, 0)),
                   pl.BlockSpec((1, 1, LANES), lambda b, i: (b * nblk + i, 0, 0))],
        out_shape=[jax.ShapeDtypeStruct((HEAD_PAIRS, m, 2 * LANES), BF16),
                   jax.ShapeDtypeStruct((n_seq * nblk, 1, LANES), F32)],
        scratch_shapes=[pltpu.VMEM((1, LANES), F32)],
        compiler_params=_cparams(2), name="fox_keys",
    )(lf_pad, qkvh)


def _cumsum_body(x_ref, o_ref, carry_ref, *, blk):
    i = pl.program_id(1)

    @pl.when(i == 0)
    def _():
        carry_ref[...] = jnp.zeros_like(carry_ref)

    hi, mid, lo = _split3(x_ref[...])
    row = lax.broadcasted_iota(jnp.int32, (blk, blk), 0)
    col = lax.broadcasted_iota(jnp.int32, (blk, blk), 1)
    tri = (row <= col).astype(BF16)
    cs = _dot(hi.astype(BF16), tri) + _dot(mid.astype(BF16), tri) + _dot(lo.astype(BF16), tri) + carry_ref[...]
    o_ref[0] = cs
    carry_ref[...] = cs[:, blk - 1:blk]


def _cumsum_time(lft, n_seq, blk):
    h, total = lft.shape
    nblk = total // n_seq // blk
    return pl.pallas_call(
        functools.partial(_cumsum_body, blk=blk),
        grid=(n_seq, nblk),
        in_specs=[pl.BlockSpec((h, blk), lambda b, i: (0, b * nblk + i))],
        out_specs=pl.BlockSpec((1, h, blk), lambda b, i: (b * nblk + i, 0, 0)),
        out_shape=jax.ShapeDtypeStruct((n_seq * nblk, h, blk), F32),
        scratch_shapes=[pltpu.VMEM((h, 1), F32)],
        compiler_params=_cparams(2), name="cumsum_time",
    )(lft)


def _augment_queries(q, blk, ones_first_lane):
    lane = lax.broadcasted_iota(jnp.int32, (blk, LANES), 1)
    lo_mask = lane < LANES // 2
    zero = jnp.zeros_like(q)
    out = []
    for u, part in enumerate((jnp.where(lo_mask, q, zero), jnp.where(lo_mask, zero, q))):
        first = ones_first_lane[u]
        ones = jnp.where((lane >= first) & (lane < first + 3), 1.0, 0.0).astype(BF16)
        out.append(jnp.concatenate([part, ones], axis=1))
    return out


def _softmax_step(s, bmax, c, vt, m_ref, l_ref, u):
    m_old = m_ref[u]
    m_new = jnp.maximum(m_old, bmax + c)
    p = jnp.exp2(s - (m_new - c))
    alpha = jnp.exp2(m_old - m_new)
    l_ref[u] = alpha * l_ref[u] + jnp.sum(p, axis=0, keepdims=True)
    m_ref[u] = m_new
    return alpha, _dot(vt, p.astype(BF16))


def _sweep_key_tiles(qi, produce, consume):
    @pl.when(qi == 0)
    def _():
        produce(0, 0, True)

    @pl.when(qi > 0)
    def _():
        n_pairs = lax.shift_right_logical(qi - 1, 1)
        odd = (qi - 1) & 1

        @pl.when(odd == 1)
        def _():
            produce(0, 0, False)
            produce(1, 1, False)
            consume(0, 0)

        @pl.when(odd == 0)
        def _():
            produce(0, 1, False)

        def body(i, carry):
            j = odd + 2 * i
            produce(j + 1, 0, False)
            consume(j, 1)
            produce(j + 2, 1, False)
            consume(j + 1, 0)
            return carry

        lax.fori_loop(0, n_pairs, body, 0)
        produce(qi, 0, True)
        consume(qi - 1, 1)

    consume(qi, 0)


def _fox_attn_body(fs_ref, q_ref, ka_ref, vt_ref, o_ref, acc_ref, m_ref, l_ref, s_ref, bm_ref, *, blk, nq):
    seq = pl.program_id(0)
    pair = pl.program_id(1)
    qi = pl.program_id(2)
    qa = _augment_queries(q_ref[0, 0], blk, (0, 3))
    acc_ref[...] = jnp.zeros_like(acc_ref)
    m_ref[...] = jnp.full_like(m_ref, NEG_INF)
    l_ref[...] = jnp.zeros_like(l_ref)
    half = FOX_HD

    def produce(kj, slot, diagonal):
        start = pl.multiple_of(kj * blk, blk)
        ka = ka_ref[0, pl.ds(start, blk), :]
        if diagonal:
            key = lax.broadcasted_iota(jnp.int32, (blk, blk), 0)
            query = lax.broadcasted_iota(jnp.int32, (blk, blk), 1)
            causal = key <= query
        for h in range(2):
            s = _dot_nt(ka, qa[h])
            if diagonal:
                s = jnp.where(causal, s, NEG_INF)
            s_ref[slot, h] = s
            bm_ref[slot, h] = jnp.max(s, axis=0, keepdims=True)

    def consume(kj, slot):
        vt = vt_ref[0, kj]
        for h in range(2):
            head = 2 * pair + h
            f_q = jnp.full((1, blk), fs_ref[(seq * nq + qi) * FOX_HEADS + head], F32)
            f_k = jnp.full((1, blk), fs_ref[(seq * nq + kj) * FOX_HEADS + head], F32)
            c = (f_q - f_k) * LOG2E
            alpha, pv = _softmax_step(s_ref[slot, h], bm_ref[slot, h], c, vt, m_ref, l_ref, h)
            rows = slice(h * half, (h + 1) * half)
            acc_ref[rows, :] = alpha * acc_ref[rows, :] + pv[rows, :]

    _sweep_key_tiles(qi, produce, consume)
    sub = lax.broadcasted_iota(jnp.int32, (LANES, blk), 0)
    o_t = acc_ref[...] / jnp.where(sub < half, l_ref[0], l_ref[1])
    o_ref[...] = o_t.T.astype(BF16)


def _fox_attention(qkvh, kaug, vt, f_start, n_seq, t, blk):
    m = n_seq * t
    nq = t // blk
    grid_spec = pltpu.PrefetchScalarGridSpec(
        num_scalar_prefetch=1,
        grid=(n_seq, HEAD_PAIRS, nq),
        in_specs=[pl.BlockSpec((1, 1, blk, LANES), lambda b, p, i, fs: (0, p, b * nq + i, 0)),
                  pl.BlockSpec((1, t, 2 * LANES), lambda b, p, i, fs: (p, b, 0)),
                  pl.BlockSpec((1, nq, LANES, blk), lambda b, p, i, fs: (p, b, 0, 0))],
        out_specs=pl.BlockSpec((blk, LANES), lambda b, p, i, fs: (b * nq + i, p)),
        scratch_shapes=[pltpu.VMEM((LANES, blk), F32),
                        pltpu.VMEM((2, 1, blk), F32),
                        pltpu.VMEM((2, 1, blk), F32),
                        pltpu.VMEM((2, 2, blk, blk), F32),
                        pltpu.VMEM((2, 2, 1, blk), F32)])
    return pl.pallas_call(
        functools.partial(_fox_attn_body, blk=blk, nq=nq),
        grid_spec=grid_spec,
        out_shape=jax.ShapeDtypeStruct((m, D_MODEL), BF16),
        compiler_params=_cparams(3), name="fox_attention",
    )(f_start, qkvh, kaug, vt)


def _diff_lambda(lam_ref, lam_init):
    lp = lam_ref[...]
    a = jnp.sum(lp[0:1] * lp[1:2], axis=1, keepdims=True)
    b = jnp.sum(lp[2:3] * lp[3:4], axis=1, keepdims=True)
    return jnp.exp(a) - jnp.exp(b) + lam_init


def _subln(o, g, lam_init):
    o = o * lax.rsqrt(jnp.mean(o * o, axis=-1, keepdims=True) + LN_EPS)
    return o * g * (1.0 - lam_init)


def _diff_attn_body(q_ref, ka_ref, vt_ref, lam_ref, g_ref, o_ref, acc_ref, m_ref, l_ref, s_ref, bm_ref, *,
                    blk, lam_init):
    head = pl.program_id(1)
    qi = pl.program_id(2)
    qa = _augment_queries(q_ref[0, 0], blk, (0, 0))
    acc_ref[...] = jnp.zeros_like(acc_ref)
    m_ref[...] = jnp.full_like(m_ref, NEG_INF)
    l_ref[...] = jnp.zeros_like(l_ref)
    slope = lax.bitcast_convert_type(jnp.full((1, blk), 126 - head, jnp.int32) << 23, F32) * LOG2E

    def produce(kj, slot, diagonal):
        start = pl.multiple_of(kj * blk, blk)
        ka = ka_ref[0, pl.ds(start, blk), :]
        if diagonal:
            key = lax.broadcasted_iota(jnp.int32, (blk, blk), 0)
            query = lax.broadcasted_iota(jnp.int32, (blk, blk), 1)
            visible = (key // CHUNK) <= (query // CHUNK)
            ahead = slope * (-2.0 * jnp.maximum(key - query, 0).astype(F32))
        for u in range(2):
            s = _dot_nt(ka, qa[u])
            if diagonal:
                s = jnp.where(visible, s + ahead, NEG_INF)
            s_ref[slot, u] = s
            bm_ref[slot, u] = jnp.max(s, axis=0, keepdims=True)

    def consume(kj, slot):
        vt = vt_ref[0, kj]
        c = slope * jnp.full((1, blk), (kj - qi) * blk, jnp.int32).astype(F32)
        for u in range(2):
            alpha, pv = _softmax_step(s_ref[slot, u], bm_ref[slot, u], c, vt, m_ref, l_ref, u)
            acc_ref[u] = alpha * acc_ref[u] + pv

    _sweep_key_tiles(qi, produce, consume)
    lam = _diff_lambda(lam_ref, lam_init)
    o_t = acc_ref[0] / l_ref[0] - lam * (acc_ref[1] / l_ref[1])
    o_ref[...] = _subln(o_t.T, g_ref[...], lam_init).astype(BF16)


def _diff_attention(qkvh, kaug, vt, lam_params, subln_g, n_seq, t, blk, lam_init):
    m = n_seq * t
    nq = t // blk
    return pl.pallas_call(
        functools.partial(_diff_attn_body, blk=blk, lam_init=lam_init),
        grid=(n_seq, DIFF_HEADS, nq),
        in_specs=[pl.BlockSpec((1, 1, blk, LANES), lambda b, p, i: (0, p, b * nq + i, 0)),
                  pl.BlockSpec((1, t, 2 * LANES), lambda b, p, i: (p, b, 0)),
                  pl.BlockSpec((1, nq, LANES, blk), lambda b, p, i: (p, b, 0, 0)),
                  pl.BlockSpec((4, DIFF_HD), lambda b, p, i: (0, 0)),
                  pl.BlockSpec((1, LANES), lambda b, p, i: (0, 0))],
        out_specs=pl.BlockSpec((blk, LANES), lambda b, p, i: (b * nq + i, p)),
        out_shape=jax.ShapeDtypeStruct((m, D_MODEL), BF16),
        scratch_shapes=[pltpu.VMEM((2, LANES, blk), F32),
                        pltpu.VMEM((2, 1, blk), F32),
                        pltpu.VMEM((2, 1, blk), F32),
                        pltpu.VMEM((2, 2, blk, blk), F32),
                        pltpu.VMEM((2, 2, 1, blk), F32)],
        compiler_params=_cparams(3), name="diff_attention",
    )(qkvh, kaug, vt, lam_params, subln_g)


def _decode_attn_body(*refs, mode, t_new, past_len, blk, nk, lam_init):
    if mode == "fox":
        q_ref, kn_ref, vn_ref, kc_ref, vc_ref, f_ref, o_ref, qbd_ref, acc_ref, m_ref, l_ref = refs
    else:
        q_ref, kn_ref, vn_ref, kc_ref, vc_ref, lam_ref, g_ref, o_ref, qbd_ref, acc_ref, m_ref, l_ref = refs
    kj = pl.program_id(1)
    groups = D_MODEL // FOX_HD
    rows = groups * t_new
    row1 = lax.broadcasted_iota(jnp.int32, (rows, 1), 0)

    def gather_pairs(ref):
        return jnp.concatenate([ref[0, p] for p in range(HEAD_PAIRS)], axis=1)

    @pl.when(kj == 0)
    def _():
        q = gather_pairs(q_ref)
        qt = jnp.concatenate([q] * groups, axis=0)
        r = lax.broadcasted_iota(jnp.int32, (rows, D_MODEL), 0)
        c = lax.broadcasted_iota(jnp.int32, (rows, D_MODEL), 1)
        qbd_ref[...] = jnp.where((r // t_new) == (c // FOX_HD), qt, jnp.zeros_like(qt))
        acc_ref[...] = jnp.zeros_like(acc_ref)
        m_ref[...] = jnp.full_like(m_ref, NEG_INF)
        l_ref[...] = jnp.zeros_like(l_ref)

    if mode == "fox":
        f_last = f_ref[nk - 1][:, blk - 1:blk]

        def head_rows(x):
            n = x.shape[1]
            return jnp.concatenate(
                [jnp.broadcast_to(x[h:h + 1, :], (t_new, n)) for h in range(FOX_HEADS)], axis=0)
    else:
        slope = jnp.exp2(-(row1 // (2 * t_new) + 1).astype(F32)) * LOG2E

    def update(s, v):
        m_old = m_ref[...]
        m_new = jnp.maximum(m_old, jnp.max(s, axis=1, keepdims=True))
        p = jnp.exp2(s - m_new)
        a = jnp.exp2(m_old - m_new)
        l_ref[...] = a * l_ref[...] + jnp.sum(p, axis=1, keepdims=True)
        m_ref[...] = m_new
        acc_ref[...] = a * acc_ref[...] + _dot(p.astype(BF16), v)

    kc = kc_ref[0].astype(BF16)
    vc = vc_ref[0].astype(BF16)
    s = _dot_nt(qbd_ref[...], kc)
    if mode == "fox":
        s = s + head_rows((f_last - f_ref[kj]) * LOG2E)
    else:
        col = lax.broadcasted_iota(jnp.int32, (1, blk), 1)
        s = s + slope * (col + (kj * blk - past_len)).astype(F32)
    update(s, vc)

    @pl.when(kj == nk - 1)
    def _():
        kn = gather_pairs(kn_ref)
        vn = gather_pairs(vn_ref)
        s = _dot_nt(qbd_ref[...], kn)
        tq = lax.broadcasted_iota(jnp.int32, (rows, t_new), 0) % t_new
        tk = lax.broadcasted_iota(jnp.int32, (rows, t_new), 1)
        if mode == "fox":
            s = s + head_rows((f_last - f_ref[nk][:, :t_new]) * LOG2E)
            visible = tk <= tq
        else:
            s = s + slope * (tq - jnp.abs(tq - tk)).astype(F32)
            visible = ((tk + past_len) // CHUNK) <= ((tq + past_len) // CHUNK)
        update(jnp.where(visible, s, NEG_INF), vn)

        r = lax.broadcasted_iota(jnp.int32, (rows, D_MODEL), 0)
        c = lax.broadcasted_iota(jnp.int32, (rows, D_MODEL), 1)
        inv_l = 1.0 / l_ref[...]
        if mode == "fox":
            own = (r // t_new) == (c // FOX_HD)
            coef = inv_l
        else:
            own = (r // (2 * t_new)) == (c // LANES)
            lam = _diff_lambda(lam_ref, lam_init)
            coef = jnp.where((row1 // t_new) % 2 == 0, inv_l, -lam * inv_l)
        w = jnp.where(own, acc_ref[...] * coef, 0.0)
        o = w[0:t_new]
        for gi in range(1, groups):
            o = o + w[gi * t_new:(gi + 1) * t_new]
        if mode == "diff":
            g = g_ref[...]
            o = jnp.concatenate(
                [_subln(o[:, h * LANES:(h + 1) * LANES], g, lam_init) for h in range(DIFF_HEADS)], axis=1)
        o_ref[0] = o.astype(BF16)


def _decode_attention(qkvh, cache_k, cache_v, extra, mode, n_seq, t_new, blk, lam_init=0.0):
    past_len = cache_k.shape[1]
    nk = past_len // blk
    rows = (D_MODEL // FOX_HD) * t_new
    new_spec = lambda s: pl.BlockSpec((1, HEAD_PAIRS, t_new, LANES), lambda b, j: (s, 0, b, 0))
    cache_spec = pl.BlockSpec((1, blk, D_MODEL), lambda b, j: (b, j, 0))
    in_specs = [new_spec(0), new_spec(1), new_spec(2), cache_spec, cache_spec]
    if mode == "fox":
        in_specs.append(pl.BlockSpec((nk + 1, FOX_HEADS, blk), lambda b, j: (b, 0, 0)))
    else:
        in_specs += [pl.BlockSpec((4, DIFF_HD), lambda b, j: (0, 0)),
                     pl.BlockSpec((1, LANES), lambda b, j: (0, 0))]
    out = pl.pallas_call(
        functools.partial(_decode_attn_body, mode=mode, t_new=t_new, past_len=past_len, blk=blk, nk=nk,
                          lam_init=lam_init),
        grid=(n_seq, nk),
        in_specs=in_specs,
        out_specs=pl.BlockSpec((1, t_new, D_MODEL), lambda b, j: (b, 0, 0)),
        out_shape=jax.ShapeDtypeStruct((n_seq, t_new, D_MODEL), BF16),
        scratch_shapes=[pltpu.VMEM((rows, D_MODEL), BF16),
                        pltpu.VMEM((rows, D_MODEL), F32),
                        pltpu.VMEM((rows, 1), F32),
                        pltpu.VMEM((rows, 1), F32)],
        compiler_params=_cparams(2), name="decode_attention_" + mode,
    )(qkvh, qkvh, qkvh, cache_k, cache_v, *extra)
    return out.reshape(n_seq * t_new, D_MODEL)


def _oproj_ln_body(x_ref, o_ref, w_ref, g_ref, b_ref, y_ref):
    y = _dot(o_ref[...], w_ref[...])
    y_ref[...] = _layernorm(DEEPNORM_ALPHA * x_ref[...] + y, g_ref[...], b_ref[...])


def _oproj_ln(x2d, o2d, w, g, b, tm=512):
    m = x2d.shape[0]
    tm = min(tm, m)
    row = pl.BlockSpec((tm, D_MODEL), lambda i: (i, 0))
    vec = pl.BlockSpec((1, D_MODEL), lambda i: (0, 0))
    return pl.pallas_call(
        _oproj_ln_body, grid=(m // tm,),
        in_specs=[row, row, pl.BlockSpec((D_MODEL, D_MODEL), lambda i: (0, 0)), vec, vec],
        out_specs=row, out_shape=jax.ShapeDtypeStruct((m, D_MODEL), F32),
        compiler_params=_cparams(1), name="oproj_ln",
    )(x2d, o2d, w, g, b)


def _ffn_ln_body(x_ref, win_ref, wout_ref, g_ref, b_ref, y_ref):
    x = x_ref[...]
    h = _dot(x.astype(BF16), win_ref[...])
    gate = h[:, :D_FF]
    up = h[:, D_FF:]
    act = (gate * jax.nn.sigmoid(gate) * up).astype(BF16)
    y = _dot(act, wout_ref[...])
    y_ref[...] = _layernorm(DEEPNORM_ALPHA * x + y, g_ref[...], b_ref[...])


def _ffn_ln(x2d, w_in, w_out, g, b, tm=256):
    m = x2d.shape[0]
    tm = min(tm, m)
    row = pl.BlockSpec((tm, D_MODEL), lambda i: (i, 0))
    vec = pl.BlockSpec((1, D_MODEL), lambda i: (0, 0))
    return pl.pallas_call(
        _ffn_ln_body, grid=(m // tm,),
        in_specs=[row,
                  pl.BlockSpec((D_MODEL, 2 * D_FF), lambda i: (0, 0)),
                  pl.BlockSpec((D_FF, D_MODEL), lambda i: (0, 0)),
                  vec, vec],
        out_specs=row, out_shape=jax.ShapeDtypeStruct((m, D_MODEL), F32),
        compiler_params=_cparams(1), name="ffn_ln",
    )(x2d, w_in, w_out, g, b)


def _run_stream(x, past, wts):
    n_seq, t, _ = x.shape
    m = n_seq * t
    blk = ATTN_BLK
    x2d = x.reshape(m, D_MODEL)
    lam_init = 0.8 - 0.6 * math.exp(-0.3 * 1)

    qkvh, fk, fv, vt, logf, logf_pad, logft = _qkv_proj(
        x2d, wts["fox_wqkv"], wts["fox_wvt"], FOX_HD ** -0.5 * LOG2E, forget=wts["fox_forget"])
    if past is None:
        kaug, f_start = _fox_keys(logf_pad, qkvh, n_seq, t, blk)
        o = _fox_attention(qkvh, kaug, vt, f_start[:, 0, :FOX_HEADS].reshape(-1), n_seq, t, blk)
    else:
        past_k, past_v, past_lf = past[0], past[1], past[2]
        past_len = past_k.shape[1]
        lf_all = jnp.concatenate(
            [jnp.transpose(past_lf, (2, 0, 1)),
             logft.reshape(FOX_HEADS, n_seq, t),
             jnp.zeros((FOX_HEADS, n_seq, blk - t), F32)], axis=2)
        fcum = _cumsum_time(lf_all.reshape(FOX_HEADS, n_seq * (past_len + blk)), n_seq, blk)
        o = _decode_attention(qkvh, past_k.reshape(n_seq, past_len, D_MODEL),
                              past_v.reshape(n_seq, past_len, D_MODEL), (fcum,), "fox", n_seq, t, blk)
    x2d = _oproj_ln(x2d, o, wts["fox_wout"], wts["ln_g"][0][0], wts["ln_b"][0][0])
    x2d = _ffn_ln(x2d, wts["ffn_win"][0], wts["ffn_wout"][0], wts["ln_g"][0][1], wts["ln_b"][0][1])

    qkvh, dk, dv, vt, kaug = _qkv_proj(x2d, wts["diff_wqkv"], wts["diff_wvt"], DIFF_HD ** -0.5 * LOG2E)
    if past is None:
        o = _diff_attention(qkvh, kaug, vt, wts["diff_lambda"], wts["diff_g"], n_seq, t, blk, lam_init)
    else:
        past_k, past_v = past[3], past[4]
        past_len = past_k.shape[1]
        o = _decode_attention(qkvh, past_k.reshape(n_seq, past_len, D_MODEL),
                              past_v.reshape(n_seq, past_len, D_MODEL),
                              (wts["diff_lambda"], wts["diff_g"]), "diff", n_seq, t, blk, lam_init)
    x2d = _oproj_ln(x2d, o, wts["diff_wout"], wts["ln_g"][1][0], wts["ln_b"][1][0])
    x2d = _ffn_ln(x2d, wts["ffn_win"][1], wts["ffn_wout"][1], wts["ln_g"][1][1], wts["ln_b"][1][1])

    return (x2d.reshape(n_seq, t, D_MODEL),
            fk.reshape(1, n_seq, t, FOX_HEADS, FOX_HD), fv.reshape(1, n_seq, t, FOX_HEADS, FOX_HD),
            logf.reshape(1, n_seq, t, FOX_HEADS),
            dk.reshape(1, n_seq, t, DIFF_HEADS, 2 * DIFF_HD), dv.reshape(1, n_seq, t, DIFF_HEADS, 2 * DIFF_HD))


def _prepare_weights(fox_w_in, fox_b_f, fox_w_out, diff_w_in, diff_lambda, diff_subln_g, diff_w_out,
                     ffn_w_in, ffn_w_out, ln_g, ln_b):
    wf = fox_w_in[0][:, 3 * D_MODEL:].astype(BF16)
    wf_pad = jnp.pad(wf, ((0, 0), (0, LANES - FOX_HEADS)))
    bf_pad = jnp.pad(fox_b_f[0].reshape(1, FOX_HEADS), ((0, 0), (0, LANES - FOX_HEADS)))
    return {
        "fox_wqkv": fox_w_in[0][:, :3 * D_MODEL].astype(BF16),
        "fox_wvt": fox_w_in[0][:, 2 * D_MODEL:3 * D_MODEL].T.astype(BF16),
        "fox_forget": (wf_pad, wf.T, bf_pad, fox_b_f[0].reshape(FOX_HEADS, 1)),
        "fox_wout": fox_w_out[0].astype(BF16),
        "diff_wqkv": diff_w_in[0].astype(BF16),
        "diff_wvt": diff_w_in[0][:, 2 * D_MODEL:].T.astype(BF16),
        "diff_lambda": diff_lambda[0],
        "diff_g": diff_subln_g[0].reshape(1, 2 * DIFF_HD),
        "diff_wout": diff_w_out[0].astype(BF16),
        "ffn_win": [ffn_w_in[i].astype(BF16) for i in range(DEPTH)],
        "ffn_wout": [ffn_w_out[i].astype(BF16) for i in range(DEPTH)],
        "ln_g": [[ln_g[i, j].reshape(1, D_MODEL) for j in range(2)] for i in range(DEPTH)],
        "ln_b": [[ln_b[i, j].reshape(1, D_MODEL) for j in range(2)] for i in range(DEPTH)],
    }


def kernel(x_prompt, x_sample, cache_fox_k, cache_fox_v, cache_fox_logf, cache_diff_k, cache_diff_v, fox_w_in, fox_b_f, fox_w_out, diff_w_in, diff_lambda, diff_subln_g, diff_w_out, ffn_w_in, ffn_w_out, ln_g, ln_b):
    wts = _prepare_weights(fox_w_in, fox_b_f, fox_w_out, diff_w_in, diff_lambda, diff_subln_g, diff_w_out,
                           ffn_w_in, ffn_w_out, ln_g, ln_b)
    y_p, fk_p, fv_p, lf_p, dk_p, dv_p = _run_stream(x_prompt, None, wts)
    past = (cache_fox_k[0], cache_fox_v[0], cache_fox_logf[0], cache_diff_k[0], cache_diff_v[0])
    y_s, fk_s, fv_s, lf_s, dk_s, dv_s = _run_stream(x_sample, past, wts)
    return (y_p, y_s, fk_p, fv_p, lf_p, dk_p, dv_p, fk_s, fv_s, lf_s, dk_s, dv_s)
```

```python
import functools
import math

import jax
import jax.numpy as jnp
from jax import lax
from jax.experimental import pallas as pl
from jax.experimental.pallas import tpu as pltpu

F32 = jnp.float32
BF16 = jnp.bfloat16

D_MODEL = 1024
DEPTH = 2
FOX_HEADS = 16
FOX_HD = 64
DIFF_HEADS = 8
DIFF_HD = 64
D_FF = 2816
CHUNK = 64
DEEPNORM_ALPHA = (2.0 * DEPTH) ** 0.25
LN_EPS = 1e-5
NEG_INF = -1e30
LOG2E = 1.4426950408889634

LANES = 128
HEAD_PAIRS = D_MODEL // LANES
ATTN_BLK = 512
ONES_ROWS = 16
VMEM_LIMIT = 56 * 1024 * 1024


def _cparams(n_axes):
    return pltpu.CompilerParams(dimension_semantics=("arbitrary",) * n_axes,
                                vmem_limit_bytes=VMEM_LIMIT)


def _log_sigmoid(z):
    return jnp.minimum(z, 0.0) - jnp.log1p(jnp.exp(-jnp.abs(z)))


def _layernorm(z, g, b):
    mu = jnp.mean(z, axis=-1, keepdims=True)
    zc = z - mu
    var = jnp.mean(zc * zc, axis=-1, keepdims=True)
    return zc * lax.rsqrt(var + LN_EPS) * g + b


def _dot(a, b):
    return jnp.dot(a, b, preferred_element_type=F32)


def _dot_nt(a, b):
    return lax.dot_general(a, b, (((1,), (1,)), ((), ())), preferred_element_type=F32)


def _vt_rows(group):
    return (LANES // group) * (group + ONES_ROWS)


def _split3(x):
    hi = x.astype(BF16).astype(F32)
    r = x - hi
    mid = r.astype(BF16).astype(F32)
    lo = (r - mid).astype(BF16).astype(F32)
    return hi, mid, lo


def _proj_body(*refs, q_scale, with_forget):
    if with_forget:
        (x_ref, w_ref, wvt_ref, wf_ref, wft_ref, bf_ref, bft_ref,
         qkv_ref, k_ref, v_ref, vt_ref, lf_ref, lfp_ref, lft_ref) = refs
    else:
        x_ref, w_ref, wvt_ref, qkv_ref, k_ref, v_ref, vt_ref, ka_ref = refs
    j = pl.program_id(1)
    tm = x_ref.shape[0]
    xb = x_ref[...].astype(BF16)
    acc = _dot(xb, w_ref[...])

    def store_heads(val):
        vb = val.astype(BF16)
        for p in range(HEAD_PAIRS):
            qkv_ref[0, p] = vb[:, p * LANES:(p + 1) * LANES]
        return vb

    @pl.when(j == 0)
    def _():
        store_heads(acc * q_scale)
        if with_forget:
            lfp = _log_sigmoid(_dot(xb, wf_ref[...]) + bf_ref[...])
            lfp_ref[...] = lfp
            lf_ref[...] = lfp[:, :FOX_HEADS]
            lft_ref[...] = _log_sigmoid(_dot_nt(wft_ref[...], xb) + bft_ref[...])

    @pl.when(j == 1)
    def _():
        kb = store_heads(acc)
        k_ref[...] = acc
        if not with_forget:
            lane = lax.broadcasted_iota(jnp.int32, (tm, LANES), 1)
            r_local = lax.broadcasted_iota(jnp.int32, (tm, LANES), 0).astype(F32)
            for h in range(DIFF_HEADS):
                hi, mid, lo = _split3(r_local * (2.0 ** -(h + 1) * LOG2E))
                cols = jnp.where(lane == 0, hi, jnp.where(lane == 1, mid, jnp.where(lane == 2, lo, 0.0)))
                ka_ref[h, :, 0:LANES] = kb[:, h * LANES:(h + 1) * LANES]
                ka_ref[h, :, LANES:2 * LANES] = cols.astype(BF16)

    @pl.when(j == 2)
    def _():
        store_heads(acc)
        v_ref[...] = acc
        vt = _dot_nt(wvt_ref[...], xb).astype(BF16)
        ones = jnp.ones((ONES_ROWS, tm), BF16)
        group = FOX_HD if with_forget else LANES
        for p in range(HEAD_PAIRS):
            for gi in range(LANES // group):
                src = p * LANES + gi * group
                dst = gi * (group + ONES_ROWS)
                vt_ref[p, 0, dst:dst + group, :] = vt[src:src + group, :]
                vt_ref[p, 0, dst + group:dst + group + ONES_ROWS, :] = ones


def _qkv_proj(x2d, w_qkv, w_vt, q_scale, forget=None):
    m = x2d.shape[0]
    tm = min(ATTN_BLK, m)
    nblk = m // tm
    vt_rows = _vt_rows(FOX_HD if forget is not None else LANES)
    row = pl.BlockSpec((tm, D_MODEL), lambda i, j: (i, 0))
    const = lambda shape: pl.BlockSpec(shape, lambda i, j: (0,) * len(shape))
    in_specs = [row, pl.BlockSpec((D_MODEL, D_MODEL), lambda i, j: (0, j)), const((D_MODEL, D_MODEL))]
    out_shape = [jax.ShapeDtypeStruct((3, HEAD_PAIRS, m, LANES), BF16),
                 jax.ShapeDtypeStruct((m, D_MODEL), F32),
                 jax.ShapeDtypeStruct((m, D_MODEL), F32),
                 jax.ShapeDtypeStruct((HEAD_PAIRS, nblk, vt_rows, tm), BF16)]
    out_specs = [pl.BlockSpec((1, HEAD_PAIRS, tm, LANES), lambda i, j: (j, 0, i, 0)),
                 row, row,
                 pl.BlockSpec((HEAD_PAIRS, 1, vt_rows, tm), lambda i, j: (0, i, 0, 0))]
    args = [x2d, w_qkv, w_vt]
    if forget is not None:
        wf, wft, bf, bft = forget
        in_specs += [const((D_MODEL, LANES)), const((FOX_HEADS, D_MODEL)), const((1, LANES)), const((FOX_HEADS, 1))]
        out_shape += [jax.ShapeDtypeStruct((m, FOX_HEADS), F32),
                      jax.ShapeDtypeStruct((m, LANES), F32),
                      jax.ShapeDtypeStruct((FOX_HEADS, m), F32)]
        out_specs += [pl.BlockSpec((tm, FOX_HEADS), lambda i, j: (i, 0)),
                      pl.BlockSpec((tm, LANES), lambda i, j: (i, 0)),
                      pl.BlockSpec((FOX_HEADS, tm), lambda i, j: (0, i))]
        args += [wf, wft, bf, bft]
    else:
        out_shape.append(jax.ShapeDtypeStruct((DIFF_HEADS, m, 2 * LANES), BF16))
        out_specs.append(pl.BlockSpec((DIFF_HEADS, tm, 2 * LANES), lambda i, j: (0, i, 0)))
    return pl.pallas_call(
        functools.partial(_proj_body, q_scale=q_scale, with_forget=forget is not None),
        grid=(nblk, 3), in_specs=in_specs, out_specs=out_specs, out_shape=out_shape,
        compiler_params=_cparams(2), name="qkv_proj_f" if forget is not None else "qkv_proj",
    )(*args)


def _fox_keys_body(lf_ref, k_ref, ka_ref, fs_ref, carry_ref, *, blk):
    i = pl.program_id(1)

    @pl.when(i == 0)
    def _():
        carry_ref[...] = jnp.zeros_like(carry_ref)

    row = lax.broadcasted_iota(jnp.int32, (blk, blk), 0)
    col = lax.broadcasted_iota(jnp.int32, (blk, blk), 1)
    tri = (col <= row).astype(BF16)
    hi, mid, lo = _split3(lf_ref[...])
    f_loc = _dot(tri, hi.astype(BF16)) + _dot(tri, mid.astype(BF16)) + _dot(tri, lo.astype(BF16))
    fs_ref[0] = carry_ref[...]
    carry_ref[...] = carry_ref[...] + f_loc[blk - 1:blk, :]

    bhi, bmid, blo = (t.astype(BF16) for t in _split3(f_loc * (-LOG2E)))
    sr = lax.broadcasted_iota(jnp.int32, (LANES, LANES), 0)
    sc = lax.broadcasted_iota(jnp.int32, (LANES, LANES), 1)
    for p in range(HEAD_PAIRS):
        def pick(term):
            return (((sr == 2 * p) & (sc == term)) | ((sr == 2 * p + 1) & (sc == 3 + term))).astype(BF16)
        cols = _dot(bhi, pick(0)) + _dot(bmid, pick(1)) + _dot(blo, pick(2))
        ka_ref[p, :, 0:LANES] = k_ref[0, p]
        ka_ref[p, :, LANES:2 * LANES] = cols.astype(BF16)


def _fox_keys(lf_pad, qkvh, n_seq, t, blk):
    m = n_seq * t
    nblk = t // blk
    return pl.pallas_call(
        functools.partial(_fox_keys_body, blk=blk),
        grid=(n_seq, nblk),
        in_specs=[pl.BlockSpec((blk, LANES), lambda b, i: (b * nblk + i, 0)),
                  pl.BlockSpec((1, HEAD_PAIRS, blk, LANES), lambda b, i: (1, 0, b * nblk + i, 0))],
        out_specs=[pl.BlockSpec((HEAD_PAIRS, blk, 2 * LANES), lambda b, i: (0, b * nblk + i, 0)),
                   pl.BlockSpec((1, 1, LANES), lambda b, i: (b * nblk + i, 0, 0))],
        out_shape=[jax.ShapeDtypeStruct((HEAD_PAIRS, m, 2 * LANES), BF16),
                   jax.ShapeDtypeStruct((n_seq * nblk, 1, LANES), F32)],
        scratch_shapes=[pltpu.VMEM((1, LANES), F32)],
        compiler_params=_cparams(2), name="fox_keys",
    )(lf_pad, qkvh)


def _cumsum_body(x_ref, o_ref, carry_ref, *, blk):
    i = pl.program_id(1)

    @pl.when(i == 0)
    def _():
        carry_ref[...] = jnp.zeros_like(carry_ref)

    hi, mid, lo = _split3(x_ref[...])
    row = lax.broadcasted_iota(jnp.int32, (blk, blk), 0)
    col = lax.broadcasted_iota(jnp.int32, (blk, blk), 1)
    tri = (row <= col).astype(BF16)
    cs = _dot(hi.astype(BF16), tri) + _dot(mid.astype(BF16), tri) + _dot(lo.astype(BF16), tri) + carry_ref[...]
    o_ref[0] = cs
    carry_ref[...] = cs[:, blk - 1:blk]


def _cumsum_time(lft, n_seq, blk):
    h, total = lft.shape
    nblk = total // n_seq // blk
    return pl.pallas_call(
        functools.partial(_cumsum_body, blk=blk),
        grid=(n_seq, nblk),
        in_specs=[pl.BlockSpec((h, blk), lambda b, i: (0, b * nblk + i))],
        out_specs=pl.BlockSpec((1, h, blk), lambda b, i: (b * nblk + i, 0, 0)),
        out_shape=jax.ShapeDtypeStruct((n_seq * nblk, h, blk), F32),
        scratch_shapes=[pltpu.VMEM((h, 1), F32)],
        compiler_params=_cparams(2), name="cumsum_time",
    )(lft)


def _augment_queries(q, blk, ones_first_lane):
    lane = lax.broadcasted_iota(jnp.int32, (blk, LANES), 1)
    lo_mask = lane < LANES // 2
    zero = jnp.zeros_like(q)
    out = []
    for u, part in enumerate((jnp.where(lo_mask, q, zero), jnp.where(lo_mask, zero, q))):
        first = ones_first_lane[u]
        ones = jnp.where((lane >= first) & (lane < first + 3), 1.0, 0.0).astype(BF16)
        out.append(jnp.concatenate([part, ones], axis=1))
    return out


def _softmax_step(s, bmax, c, vt, m_ref, l_ref, u):
    group = vt.shape[0] - ONES_ROWS
    m_old = m_ref[u]
    m_new = jnp.maximum(m_old, bmax + c)
    p = jnp.exp2(s - (m_new - c))
    alpha = jnp.exp2(m_old - m_new)
    pv = _dot(vt, p.astype(BF16))
    l_ref[u] = alpha * l_ref[u] + pv[group:group + 1, :]
    m_ref[u] = m_new
    return alpha, pv[:group, :]


def _sweep_key_tiles(qi, produce, consume):
    @pl.when(qi == 0)
    def _():
        produce(0, 0, True)

    @pl.when(qi > 0)
    def _():
        n_pairs = lax.shift_right_logical(qi - 1, 1)
        odd = (qi - 1) & 1

        @pl.when(odd == 1)
        def _():
            produce(0, 0, False)
            produce(1, 1, False)
            consume(0, 0)

        @pl.when(odd == 0)
        def _():
            produce(0, 1, False)

        def body(i, carry):
            j = odd + 2 * i
            produce(j + 1, 0, False)
            consume(j, 1)
            produce(j + 2, 1, False)
            consume(j + 1, 0)
            return carry

        lax.fori_loop(0, n_pairs, body, 0)
        produce(qi, 0, True)
        consume(qi - 1, 1)

    consume(qi, 0)


def _fox_attn_body(fs_ref, q_ref, ka_ref, vt_ref, o_ref, acc_ref, m_ref, l_ref, s_ref, bm_ref, *, blk, nq):
    seq = pl.program_id(0)
    pair = pl.program_id(1)
    qi = pl.program_id(2)
    qa = _augment_queries(q_ref[0, 0], blk, (0, 3))
    acc_ref[...] = jnp.zeros_like(acc_ref)
    m_ref[...] = jnp.full_like(m_ref, NEG_INF)
    l_ref[...] = jnp.zeros_like(l_ref)
    half = FOX_HD

    def produce(kj, slot, diagonal):
        start = pl.multiple_of(kj * blk, blk)
        ka = ka_ref[0, pl.ds(start, blk), :]
        if diagonal:
            key = lax.broadcasted_iota(jnp.int32, (blk, blk), 0)
            query = lax.broadcasted_iota(jnp.int32, (blk, blk), 1)
            causal = key <= query
        for h in range(2):
            s = _dot_nt(ka, qa[h])
            if diagonal:
                s = jnp.where(causal, s, NEG_INF)
            s_ref[slot, h] = s
            bm_ref[slot, h] = jnp.max(s, axis=0, keepdims=True)

    def consume(kj, slot):
        vt = vt_ref[0, kj]
        for h in range(2):
            head = 2 * pair + h
            f_q = jnp.full((1, blk), fs_ref[(seq * nq + qi) * FOX_HEADS + head], F32)
            f_k = jnp.full((1, blk), fs_ref[(seq * nq + kj) * FOX_HEADS + head], F32)
            c = (f_q - f_k) * LOG2E
            vt_h = vt[h * (half + ONES_ROWS):(h + 1) * (half + ONES_ROWS), :]
            alpha, pv = _softmax_step(s_ref[slot, h], bm_ref[slot, h], c, vt_h, m_ref, l_ref, h)
            rows = slice(h * half, (h + 1) * half)
            acc_ref[rows, :] = alpha * acc_ref[rows, :] + pv

    _sweep_key_tiles(qi, produce, consume)
    sub = lax.broadcasted_iota(jnp.int32, (LANES, blk), 0)
    o_t = acc_ref[...] / jnp.where(sub < half, l_ref[0], l_ref[1])
    o_ref[...] = o_t.T.astype(BF16)


def _fox_attention(qkvh, kaug, vt, f_start, n_seq, t, blk):
    m = n_seq * t
    nq = t // blk
    grid_spec = pltpu.PrefetchScalarGridSpec(
        num_scalar_prefetch=1,
        grid=(n_seq, HEAD_PAIRS, nq),
        in_specs=[pl.BlockSpec((1, 1, blk, LANES), lambda b, p, i, fs: (0, p, b * nq + i, 0)),
                  pl.BlockSpec((1, t, 2 * LANES), lambda b, p, i, fs: (p, b, 0)),
                  pl.BlockSpec((1, nq, _vt_rows(FOX_HD), blk), lambda b, p, i, fs: (p, b, 0, 0))],
        out_specs=pl.BlockSpec((blk, LANES), lambda b, p, i, fs: (b * nq + i, p)),
        scratch_shapes=[pltpu.VMEM((LANES, blk), F32),
                        pltpu.VMEM((2, 1, blk), F32),
                        pltpu.VMEM((2, 1, blk), F32),
                        pltpu.VMEM((2, 2, blk, blk), F32),
                        pltpu.VMEM((2, 2, 1, blk), F32)])
    return pl.pallas_call(
        functools.partial(_fox_attn_body, blk=blk, nq=nq),
        grid_spec=grid_spec,
        out_shape=jax.ShapeDtypeStruct((m, D_MODEL), BF16),
        compiler_params=_cparams(3), name="fox_attention",
    )(f_start, qkvh, kaug, vt)


def _diff_lambda(lam_ref, lam_init):
    lp = lam_ref[...]
    a = jnp.sum(lp[0:1] * lp[1:2], axis=1, keepdims=True)
    b = jnp.sum(lp[2:3] * lp[3:4], axis=1, keepdims=True)
    return jnp.exp(a) - jnp.exp(b) + lam_init


def _subln(o, g, lam_init):
    o = o * lax.rsqrt(jnp.mean(o * o, axis=-1, keepdims=True) + LN_EPS)
    return o * g * (1.0 - lam_init)


def _diff_attn_body(q_ref, ka_ref, vt_ref, lam_ref, g_ref, o_ref, acc_ref, m_ref, l_ref, s_ref, bm_ref, *,
                    blk, lam_init):
    head = pl.program_id(1)
    qi = pl.program_id(2)
    qa = _augment_queries(q_ref[0, 0], blk, (0, 0))
    acc_ref[...] = jnp.zeros_like(acc_ref)
    m_ref[...] = jnp.full_like(m_ref, NEG_INF)
    l_ref[...] = jnp.zeros_like(l_ref)
    slope = lax.bitcast_convert_type(jnp.full((1, blk), 126 - head, jnp.int32) << 23, F32) * LOG2E

    def produce(kj, slot, diagonal):
        start = pl.multiple_of(kj * blk, blk)
        ka = ka_ref[0, pl.ds(start, blk), :]
        if diagonal:
            key = lax.broadcasted_iota(jnp.int32, (blk, blk), 0)
            query = lax.broadcasted_iota(jnp.int32, (blk, blk), 1)
            visible = (key // CHUNK) <= (query // CHUNK)
            ahead = slope * (-2.0 * jnp.maximum(key - query, 0).astype(F32))
        for u in range(2):
            s = _dot_nt(ka, qa[u])
            if diagonal:
                s = jnp.where(visible, s + ahead, NEG_INF)
            s_ref[slot, u] = s
            bm_ref[slot, u] = jnp.max(s, axis=0, keepdims=True)

    def consume(kj, slot):
        vt = vt_ref[0, kj]
        c = slope * jnp.full((1, blk), (kj - qi) * blk, jnp.int32).astype(F32)
        for u in range(2):
            alpha, pv = _softmax_step(s_ref[slot, u], bm_ref[slot, u], c, vt, m_ref, l_ref, u)
            acc_ref[u] = alpha * acc_ref[u] + pv

    _sweep_key_tiles(qi, produce, consume)
    lam = _diff_lambda(lam_ref, lam_init)
    o_t = acc_ref[0] / l_ref[0] - lam * (acc_ref[1] / l_ref[1])
    o_ref[...] = _subln(o_t.T, g_ref[...], lam_init).astype(BF16)


def _diff_attention(qkvh, kaug, vt, lam_params, subln_g, n_seq, t, blk, lam_init):
    m = n_seq * t
    nq = t // blk
    return pl.pallas_call(
        functools.partial(_diff_attn_body, blk=blk, lam_init=lam_init),
        grid=(n_seq, DIFF_HEADS, nq),
        in_specs=[pl.BlockSpec((1, 1, blk, LANES), lambda b, p, i: (0, p, b * nq + i, 0)),
                  pl.BlockSpec((1, t, 2 * LANES), lambda b, p, i: (p, b, 0)),
                  pl.BlockSpec((1, nq, _vt_rows(LANES), blk), lambda b, p, i: (p, b, 0, 0)),
                  pl.BlockSpec((4, DIFF_HD), lambda b, p, i: (0, 0)),
                  pl.BlockSpec((1, LANES), lambda b, p, i: (0, 0))],
        out_specs=pl.BlockSpec((blk, LANES), lambda b, p, i: (b * nq + i, p)),
        out_shape=jax.ShapeDtypeStruct((m, D_MODEL), BF16),
        scratch_shapes=[pltpu.VMEM((2, LANES, blk), F32),
                        pltpu.VMEM((2, 1, blk), F32),
                        pltpu.VMEM((2, 1, blk), F32),
                        pltpu.VMEM((2, 2, blk, blk), F32),
                        pltpu.VMEM((2, 2, 1, blk), F32)],
        compiler_params=_cparams(3), name="diff_attention",
    )(qkvh, kaug, vt, lam_params, subln_g)


def _decode_attn_body(*refs, mode, t_new, past_len, blk, nk, lam_init):
    if mode == "fox":
        q_ref, kn_ref, vn_ref, kc_ref, vc_ref, f_ref, o_ref, qbd_ref, acc_ref, m_ref, l_ref = refs
    else:
        q_ref, kn_ref, vn_ref, kc_ref, vc_ref, lam_ref, g_ref, o_ref, qbd_ref, acc_ref, m_ref, l_ref = refs
    kj = pl.program_id(1)
    groups = D_MODEL // FOX_HD
    rows = groups * t_new
    row1 = lax.broadcasted_iota(jnp.int32, (rows, 1), 0)

    def gather_pairs(ref):
        return jnp.concatenate([ref[0, p] for p in range(HEAD_PAIRS)], axis=1)

    @pl.when(kj == 0)
    def _():
        q = gather_pairs(q_ref)
        qt = jnp.concatenate([q] * groups, axis=0)
        r = lax.broadcasted_iota(jnp.int32, (rows, D_MODEL), 0)
        c = lax.broadcasted_iota(jnp.int32, (rows, D_MODEL), 1)
        qbd_ref[...] = jnp.where((r // t_new) == (c // FOX_HD), qt, jnp.zeros_like(qt))
        acc_ref[...] = jnp.zeros_like(acc_ref)
        m_ref[...] = jnp.full_like(m_ref, NEG_INF)
        l_ref[...] = jnp.zeros_like(l_ref)

    if mode == "fox":
        f_last = f_ref[nk - 1][:, blk - 1:blk]

        def head_rows(x):
            n = x.shape[1]
            return jnp.concatenate(
                [jnp.broadcast_to(x[h:h + 1, :], (t_new, n)) for h in range(FOX_HEADS)], axis=0)
    else:
        slope = jnp.exp2(-(row1 // (2 * t_new) + 1).astype(F32)) * LOG2E

    def update(s, v):
        m_old = m_ref[...]
        m_new = jnp.maximum(m_old, jnp.max(s, axis=1, keepdims=True))
        p = jnp.exp2(s - m_new)
        a = jnp.exp2(m_old - m_new)
        l_ref[...] = a * l_ref[...] + jnp.sum(p, axis=1, keepdims=True)
        m_ref[...] = m_new
        acc_ref[...] = a * acc_ref[...] + _dot(p.astype(BF16), v)

    kc = kc_ref[0].astype(BF16)
    vc = vc_ref[0].astype(BF16)
    s = _dot_nt(qbd_ref[...], kc)
    if mode == "fox":
        s = s + head_rows((f_last - f_ref[kj]) * LOG2E)
    else:
        col = lax.broadcasted_iota(jnp.int32, (1, blk), 1)
        s = s + slope * (col + (kj * blk - past_len)).astype(F32)
    update(s, vc)

    @pl.when(kj == nk - 1)
    def _():
        kn = gather_pairs(kn_ref)
        vn = gather_pairs(vn_ref)
        s = _dot_nt(qbd_ref[...], kn)
        tq = lax.broadcasted_iota(jnp.int32, (rows, t_new), 0) % t_new
        tk = lax.broadcasted_iota(jnp.int32, (rows, t_new), 1)
        if mode == "fox":
            s = s + head_rows((f_last - f_ref[nk][:, :t_new]) * LOG2E)
            visible = tk <= tq
        else:
            s = s + slope * (tq - jnp.abs(tq - tk)).astype(F32)
            visible = ((tk + past_len) // CHUNK) <= ((tq + past_len) // CHUNK)
        update(jnp.where(visible, s, NEG_INF), vn)

        r = lax.broadcasted_iota(jnp.int32, (rows, D_MODEL), 0)
        c = lax.broadcasted_iota(jnp.int32, (rows, D_MODEL), 1)
        inv_l = 1.0 / l_ref[...]
        if mode == "fox":
            own = (r // t_new) == (c // FOX_HD)
            coef = inv_l
        else:
            own = (r // (2 * t_new)) == (c // LANES)
            lam = _diff_lambda(lam_ref, lam_init)
            coef = jnp.where((row1 // t_new) % 2 == 0, inv_l, -lam * inv_l)
        w = jnp.where(own, acc_ref[...] * coef, 0.0)
        o = w[0:t_new]
        for gi in range(1, groups):
            o = o + w[gi * t_new:(gi + 1) * t_new]
        if mode == "diff":
            g = g_ref[...]
            o = jnp.concatenate(
                [_subln(o[:, h * LANES:(h + 1) * LANES], g, lam_init) for h in range(DIFF_HEADS)], axis=1)
        o_ref[0] = o.astype(BF16)


def _decode_attention(qkvh, cache_k, cache_v, extra, mode, n_seq, t_new, blk, lam_init=0.0):
    past_len = cache_k.shape[1]
    nk = past_len // blk
    rows = (D_MODEL // FOX_HD) * t_new
    new_spec = lambda s: pl.BlockSpec((1, HEAD_PAIRS, t_new, LANES), lambda b, j: (s, 0, b, 0))
    cache_spec = pl.BlockSpec((1, blk, D_MODEL), lambda b, j: (b, j, 0))
    in_specs = [new_spec(0), new_spec(1), new_spec(2), cache_spec, cache_spec]
    if mode == "fox":
        in_specs.append(pl.BlockSpec((nk + 1, FOX_HEADS, blk), lambda b, j: (b, 0, 0)))
    else:
        in_specs += [pl.BlockSpec((4, DIFF_HD), lambda b, j: (0, 0)),
                     pl.BlockSpec((1, LANES), lambda b, j: (0, 0))]
    out = pl.pallas_call(
        functools.partial(_decode_attn_body, mode=mode, t_new=t_new, past_len=past_len, blk=blk, nk=nk,
                          lam_init=lam_init),
        grid=(n_seq, nk),
        in_specs=in_specs,
        out_specs=pl.BlockSpec((1, t_new, D_MODEL), lambda b, j: (b, 0, 0)),
        out_shape=jax.ShapeDtypeStruct((n_seq, t_new, D_MODEL), BF16),
        scratch_shapes=[pltpu.VMEM((rows, D_MODEL), BF16),
                        pltpu.VMEM((rows, D_MODEL), F32),
                        pltpu.VMEM((rows, 1), F32),
                        pltpu.VMEM((rows, 1), F32)],
        compiler_params=_cparams(2), name="decode_attention_" + mode,
    )(qkvh, qkvh, qkvh, cache_k, cache_v, *extra)
    return out.reshape(n_seq * t_new, D_MODEL)


def _oproj_ln_body(x_ref, o_ref, w_ref, g_ref, b_ref, y_ref):
    y = _dot(o_ref[...], w_ref[...])
    y_ref[...] = _layernorm(DEEPNORM_ALPHA * x_ref[...] + y, g_ref[...], b_ref[...])


def _oproj_ln(x2d, o2d, w, g, b, tm=512):
    m = x2d.shape[0]
    tm = min(tm, m)
    row = pl.BlockSpec((tm, D_MODEL), lambda i: (i, 0))
    vec = pl.BlockSpec((1, D_MODEL), lambda i: (0, 0))
    return pl.pallas_call(
        _oproj_ln_body, grid=(m // tm,),
        in_specs=[row, row, pl.BlockSpec((D_MODEL, D_MODEL), lambda i: (0, 0)), vec, vec],
        out_specs=row, out_shape=jax.ShapeDtypeStruct((m, D_MODEL), F32),
        compiler_params=_cparams(1), name="oproj_ln",
    )(x2d, o2d, w, g, b)


def _ffn_ln_body(x_ref, win_ref, wout_ref, g_ref, b_ref, y_ref):
    x = x_ref[...]
    h = _dot(x.astype(BF16), win_ref[...])
    gate = h[:, :D_FF]
    up = h[:, D_FF:]
    act = (gate * jax.nn.sigmoid(gate) * up).astype(BF16)
    y = _dot(act, wout_ref[...])
    y_ref[...] = _layernorm(DEEPNORM_ALPHA * x + y, g_ref[...], b_ref[...])


def _ffn_ln(x2d, w_in, w_out, g, b, tm=256):
    m = x2d.shape[0]
    tm = min(tm, m)
    row = pl.BlockSpec((tm, D_MODEL), lambda i: (i, 0))
    vec = pl.BlockSpec((1, D_MODEL), lambda i: (0, 0))
    return pl.pallas_call(
        _ffn_ln_body, grid=(m // tm,),
        in_specs=[row,
                  pl.BlockSpec((D_MODEL, 2 * D_FF), lambda i: (0, 0)),
                  pl.BlockSpec((D_FF, D_MODEL), lambda i: (0, 0)),
                  vec, vec],
        out_specs=row, out_shape=jax.ShapeDtypeStruct((m, D_MODEL), F32),
        compiler_params=_cparams(1), name="ffn_ln",
    )(x2d, w_in, w_out, g, b)


def _run_stream(x, past, wts):
    n_seq, t, _ = x.shape
    m = n_seq * t
    blk = ATTN_BLK
    x2d = x.reshape(m, D_MODEL)
    lam_init = 0.8 - 0.6 * math.exp(-0.3 * 1)

    qkvh, fk, fv, vt, logf, logf_pad, logft = _qkv_proj(
        x2d, wts["fox_wqkv"], wts["fox_wvt"], FOX_HD ** -0.5 * LOG2E, forget=wts["fox_forget"])
    if past is None:
        kaug, f_start = _fox_keys(logf_pad, qkvh, n_seq, t, blk)
        o = _fox_attention(qkvh, kaug, vt, f_start[:, 0, :FOX_HEADS].reshape(-1), n_seq, t, blk)
    else:
        past_k, past_v, past_lf = past[0], past[1], past[2]
        past_len = past_k.shape[1]
        lf_all = jnp.concatenate(
            [jnp.transpose(past_lf, (2, 0, 1)),
             logft.reshape(FOX_HEADS, n_seq, t),
             jnp.zeros((FOX_HEADS, n_seq, blk - t), F32)], axis=2)
        fcum = _cumsum_time(lf_all.reshape(FOX_HEADS, n_seq * (past_len + blk)), n_seq, blk)
        o = _decode_attention(qkvh, past_k.reshape(n_seq, past_len, D_MODEL),
                              past_v.reshape(n_seq, past_len, D_MODEL), (fcum,), "fox", n_seq, t, blk)
    x2d = _oproj_ln(x2d, o, wts["fox_wout"], wts["ln_g"][0][0], wts["ln_b"][0][0])
    x2d = _ffn_ln(x2d, wts["ffn_win"][0], wts["ffn_wout"][0], wts["ln_g"][0][1], wts["ln_b"][0][1])

    qkvh, dk, dv, vt, kaug = _qkv_proj(x2d, wts["diff_wqkv"], wts["diff_wvt"], DIFF_HD ** -0.5 * LOG2E)
    if past is None:
        o = _diff_attention(qkvh, kaug, vt, wts["diff_lambda"], wts["diff_g"], n_seq, t, blk, lam_init)
    else:
        past_k, past_v = past[3], past[4]
        past_len = past_k.shape[1]
        o = _decode_attention(qkvh, past_k.reshape(n_seq, past_len, D_MODEL),
                              past_v.reshape(n_seq, past_len, D_MODEL),
                              (wts["diff_lambda"], wts["diff_g"]), "diff", n_seq, t, blk, lam_init)
    x2d = _oproj_ln(x2d, o, wts["diff_wout"], wts["ln_g"][1][0], wts["ln_b"][1][0])
    x2d = _ffn_ln(x2d, wts["ffn_win"][1], wts["ffn_wout"][1], wts["ln_g"][1][1], wts["ln_b"][1][1])

    return (x2d.reshape(n_seq, t, D_MODEL),
            fk.reshape(1, n_seq, t, FOX_HEADS, FOX_HD), fv.reshape(1, n_seq, t, FOX_HEADS, FOX_HD),
            logf.reshape(1, n_seq, t, FOX_HEADS),
            dk.reshape(1, n_seq, t, DIFF_HEADS, 2 * DIFF_HD), dv.reshape(1, n_seq, t, DIFF_HEADS, 2 * DIFF_HD))


def _prepare_weights(fox_w_in, fox_b_f, fox_w_out, diff_w_in, diff_lambda, diff_subln_g, diff_w_out,
                     ffn_w_in, ffn_w_out, ln_g, ln_b):
    wf = fox_w_in[0][:, 3 * D_MODEL:].astype(BF16)
    wf_pad = jnp.pad(wf, ((0, 0), (0, LANES - FOX_HEADS)))
    bf_pad = jnp.pad(fox_b_f[0].reshape(1, FOX_HEADS), ((0, 0), (0, LANES - FOX_HEADS)))
    return {
        "fox_wqkv": fox_w_in[0][:, :3 * D_MODEL].astype(BF16),
        "fox_wvt": fox_w_in[0][:, 2 * D_MODEL:3 * D_MODEL].T.astype(BF16),
        "fox_forget": (wf_pad, wf.T, bf_pad, fox_b_f[0].reshape(FOX_HEADS, 1)),
        "fox_wout": fox_w_out[0].astype(BF16),
        "diff_wqkv": diff_w_in[0].astype(BF16),
        "diff_wvt": diff_w_in[0][:, 2 * D_MODEL:].T.astype(BF16),
        "diff_lambda": diff_lambda[0],
        "diff_g": diff_subln_g[0].reshape(1, 2 * DIFF_HD),
        "diff_wout": diff_w_out[0].astype(BF16),
        "ffn_win": [ffn_w_in[i].astype(BF16) for i in range(DEPTH)],
        "ffn_wout": [ffn_w_out[i].astype(BF16) for i in range(DEPTH)],
        "ln_g": [[ln_g[i, j].reshape(1, D_MODEL) for j in range(2)] for i in range(DEPTH)],
        "ln_b": [[ln_b[i, j].reshape(1, D_MODEL) for j in range(2)] for i in range(DEPTH)],
    }


def kernel(x_prompt, x_sample, cache_fox_k, cache_fox_v, cache_fox_logf, cache_diff_k, cache_diff_v, fox_w_in, fox_b_f, fox_w_out, diff_w_in, diff_lambda, diff_subln_g, diff_w_out, ffn_w_in, ffn_w_out, ln_g, ln_b):
    wts = _prepare_weights(fox_w_in, fox_b_f, fox_w_out, diff_w_in, diff_lambda, diff_subln_g, diff_w_out,
                           ffn_w_in, ffn_w_out, ln_g, ln_b)
    y_p, fk_p, fv_p, lf_p, dk_p, dv_p = _run_stream(x_prompt, None, wts)
    past = (cache_fox_k[0], cache_fox_v[0], cache_fox_logf[0], cache_diff_k[0], cache_diff_v[0])
    y_s, fk_s, fv_s, lf_s, dk_s, dv_s = _run_stream(x_sample, past, wts)
    return (y_p, y_s, fk_p, fv_p, lf_p, dk_p, dv_p, fk_s, fv_s, lf_s, dk_s, dv_s)
```

```python
import functools
import math

import jax
import jax.numpy as jnp
from jax import lax
from jax.experimental import pallas as pl
from jax.experimental.pallas import tpu as pltpu

F32 = jnp.float32
BF16 = jnp.bfloat16

D_MODEL = 1024
DEPTH = 2
FOX_HEADS = 16
FOX_HD = 64
DIFF_HEADS = 8
DIFF_HD = 64
D_FF = 2816
CHUNK = 64
DEEPNORM_ALPHA = (2.0 * DEPTH) ** 0.25
LN_EPS = 1e-5
NEG_INF = -1e30
LOG2E = 1.4426950408889634

LANES = 128
HEAD_PAIRS = D_MODEL // LANES
ATTN_BLK = 512
ONES_ROWS = 16
VMEM_LIMIT = 56 * 1024 * 1024


def _cparams(n_axes):
    return pltpu.CompilerParams(dimension_semantics=("arbitrary",) * n_axes,
                                vmem_limit_bytes=VMEM_LIMIT)


def _log_sigmoid(z):
    return jnp.minimum(z, 0.0) - jnp.log1p(jnp.exp(-jnp.abs(z)))


def _layernorm(z, g, b):
    mu = jnp.mean(z, axis=-1, keepdims=True)
    zc = z - mu
    var = jnp.mean(zc * zc, axis=-1, keepdims=True)
    return zc * lax.rsqrt(var + LN_EPS) * g + b


def _dot(a, b):
    return jnp.dot(a, b, preferred_element_type=F32)


def _dot_nt(a, b):
    return lax.dot_general(a, b, (((1,), (1,)), ((), ())), preferred_element_type=F32)


def _vt_rows(group):
    return (LANES // group) * (group + ONES_ROWS)


def _split3(x):
    hi = x.astype(BF16).astype(F32)
    r = x - hi
    mid = r.astype(BF16).astype(F32)
    lo = (r - mid).astype(BF16).astype(F32)
    return hi, mid, lo


def _proj_body(*refs, q_scale, with_forget):
    if with_forget:
        (x_ref, w_ref, wvt_ref, wf_ref, wft_ref, bf_ref, bft_ref,
         qkv_ref, k_ref, v_ref, vt_ref, lf_ref, lfp_ref, lft_ref) = refs
    else:
        x_ref, w_ref, wvt_ref, qkv_ref, k_ref, v_ref, vt_ref, ka_ref = refs
    j = pl.program_id(1)
    tm = x_ref.shape[0]
    xb = x_ref[...].astype(BF16)
    acc = _dot(xb, w_ref[...])

    def store_heads(val):
        vb = val.astype(BF16)
        for p in range(HEAD_PAIRS):
            qkv_ref[0, p] = vb[:, p * LANES:(p + 1) * LANES]
        return vb

    @pl.when(j == 0)
    def _():
        store_heads(acc * q_scale)
        if with_forget:
            lfp = _log_sigmoid(_dot(xb, wf_ref[...]) + bf_ref[...])
            lfp_ref[...] = lfp
            lf_ref[...] = lfp[:, :FOX_HEADS]
            lft_ref[...] = _log_sigmoid(_dot_nt(wft_ref[...], xb) + bft_ref[...])

    @pl.when(j == 1)
    def _():
        kb = store_heads(acc)
        k_ref[...] = acc
        if not with_forget:
            lane = lax.broadcasted_iota(jnp.int32, (tm, LANES), 1)
            r_local = lax.broadcasted_iota(jnp.int32, (tm, LANES), 0).astype(F32)
            for h in range(DIFF_HEADS):
                hi, mid, lo = _split3(r_local * (2.0 ** -(h + 1) * LOG2E))
                cols = jnp.where(lane == 0, hi, jnp.where(lane == 1, mid, jnp.where(lane == 2, lo, 0.0)))
                ka_ref[h, :, 0:LANES] = kb[:, h * LANES:(h + 1) * LANES]
                ka_ref[h, :, LANES:2 * LANES] = cols.astype(BF16)

    @pl.when(j == 2)
    def _():
        store_heads(acc)
        v_ref[...] = acc
        vt = _dot_nt(wvt_ref[...], xb).astype(BF16)
        ones = jnp.ones((ONES_ROWS, tm), BF16)
        group = FOX_HD if with_forget else LANES
        for p in range(HEAD_PAIRS):
            for gi in range(LANES // group):
                src = p * LANES + gi * group
                dst = gi * (group + ONES_ROWS)
                vt_ref[p, 0, dst:dst + group, :] = vt[src:src + group, :]
                vt_ref[p, 0, dst + group:dst + group + ONES_ROWS, :] = ones


def _qkv_proj(x2d, w_qkv, w_vt, q_scale, forget=None):
    m = x2d.shape[0]
    tm = min(ATTN_BLK, m)
    nblk = m // tm
    vt_rows = _vt_rows(FOX_HD if forget is not None else LANES)
    row = pl.BlockSpec((tm, D_MODEL), lambda i, j: (i, 0))
    const = lambda shape: pl.BlockSpec(shape, lambda i, j: (0,) * len(shape))
    in_specs = [row, pl.BlockSpec((D_MODEL, D_MODEL), lambda i, j: (0, j)), const((D_MODEL, D_MODEL))]
    out_shape = [jax.ShapeDtypeStruct((3, HEAD_PAIRS, m, LANES), BF16),
                 jax.ShapeDtypeStruct((m, D_MODEL), F32),
                 jax.ShapeDtypeStruct((m, D_MODEL), F32),
                 jax.ShapeDtypeStruct((HEAD_PAIRS, nblk, vt_rows, tm), BF16)]
    out_specs = [pl.BlockSpec((1, HEAD_PAIRS, tm, LANES), lambda i, j: (j, 0, i, 0)),
                 row, row,
                 pl.BlockSpec((HEAD_PAIRS, 1, vt_rows, tm), lambda i, j: (0, i, 0, 0))]
    args = [x2d, w_qkv, w_vt]
    if forget is not None:
        wf, wft, bf, bft = forget
        in_specs += [const((D_MODEL, LANES)), const((FOX_HEADS, D_MODEL)), const((1, LANES)), const((FOX_HEADS, 1))]
        out_shape += [jax.ShapeDtypeStruct((m, FOX_HEADS), F32),
                      jax.ShapeDtypeStruct((m, LANES), F32),
                      jax.ShapeDtypeStruct((FOX_HEADS, m), F32)]
        out_specs += [pl.BlockSpec((tm, FOX_HEADS), lambda i, j: (i, 0)),
                      pl.BlockSpec((tm, LANES), lambda i, j: (i, 0)),
                      pl.BlockSpec((FOX_HEADS, tm), lambda i, j: (0, i))]
        args += [wf, wft, bf, bft]
    else:
        out_shape.append(jax.ShapeDtypeStruct((DIFF_HEADS, m, 2 * LANES), BF16))
        out_specs.append(pl.BlockSpec((DIFF_HEADS, tm, 2 * LANES), lambda i, j: (0, i, 0)))
    return pl.pallas_call(
        functools.partial(_proj_body, q_scale=q_scale, with_forget=forget is not None),
        grid=(nblk, 3), in_specs=in_specs, out_specs=out_specs, out_shape=out_shape,
        compiler_params=_cparams(2), name="qkv_proj_f" if forget is not None else "qkv_proj",
    )(*args)


def _fox_keys_body(lf_ref, k_ref, ka_ref, fs_ref, carry_ref, *, blk):
    i = pl.program_id(1)

    @pl.when(i == 0)
    def _():
        carry_ref[...] = jnp.zeros_like(carry_ref)

    row = lax.broadcasted_iota(jnp.int32, (blk, blk), 0)
    col = lax.broadcasted_iota(jnp.int32, (blk, blk), 1)
    tri = (col <= row).astype(BF16)
    hi, mid, lo = _split3(lf_ref[...])
    f_loc = _dot(tri, hi.astype(BF16)) + _dot(tri, mid.astype(BF16)) + _dot(tri, lo.astype(BF16))
    fs_ref[0] = carry_ref[...]
    carry_ref[...] = carry_ref[...] + f_loc[blk - 1:blk, :]

    bhi, bmid, blo = (t.astype(BF16) for t in _split3(f_loc * (-LOG2E)))
    sr = lax.broadcasted_iota(jnp.int32, (LANES, LANES), 0)
    sc = lax.broadcasted_iota(jnp.int32, (LANES, LANES), 1)
    for p in range(HEAD_PAIRS):
        def pick(term):
            return (((sr == 2 * p) & (sc == term)) | ((sr == 2 * p + 1) & (sc == 3 + term))).astype(BF16)
        cols = _dot(bhi, pick(0)) + _dot(bmid, pick(1)) + _dot(blo, pick(2))
        ka_ref[p, :, 0:LANES] = k_ref[0, p]
        ka_ref[p, :, LANES:2 * LANES] = cols.astype(BF16)


def _fox_keys(lf_pad, qkvh, n_seq, t, blk):
    m = n_seq * t
    nblk = t // blk
    return pl.pallas_call(
        functools.partial(_fox_keys_body, blk=blk),
        grid=(n_seq, nblk),
        in_specs=[pl.BlockSpec((blk, LANES), lambda b, i: (b * nblk + i, 0)),
                  pl.BlockSpec((1, HEAD_PAIRS, blk, LANES), lambda b, i: (1, 0, b * nblk + i, 0))],
        out_specs=[pl.BlockSpec((HEAD_PAIRS, blk, 2 * LANES), lambda b, i: (0, b * nblk + i, 0)),
                   pl.BlockSpec((1, 1, LANES), lambda b, i: (b * nblk + i, 0, 0))],
        out_shape=[jax.ShapeDtypeStruct((HEAD_PAIRS, m, 2 * LANES), BF16),
                   jax.ShapeDtypeStruct((n_seq * nblk, 1, LANES), F32)],
        scratch_shapes=[pltpu.VMEM((1, LANES), F32)],
        compiler_params=_cparams(2), name="fox_keys",
    )(lf_pad, qkvh)


def _cumsum_body(x_ref, o_ref, carry_ref, *, blk):
    i = pl.program_id(1)

    @pl.when(i == 0)
    def _():
        carry_ref[...] = jnp.zeros_like(carry_ref)

    hi, mid, lo = _split3(x_ref[...])
    row = lax.broadcasted_iota(jnp.int32, (blk, blk), 0)
    col = lax.broadcasted_iota(jnp.int32, (blk, blk), 1)
    tri = (row <= col).astype(BF16)
    cs = _dot(hi.astype(BF16), tri) + _dot(mid.astype(BF16), tri) + _dot(lo.astype(BF16), tri) + carry_ref[...]
    o_ref[0] = cs
    carry_ref[...] = cs[:, blk - 1:blk]


def _cumsum_time(lft, n_seq, blk):
    h, total = lft.shape
    nblk = total // n_seq // blk
    return pl.pallas_call(
        functools.partial(_cumsum_body, blk=blk),
        grid=(n_seq, nblk),
        in_specs=[pl.BlockSpec((h, blk), lambda b, i: (0, b * nblk + i))],
        out_specs=pl.BlockSpec((1, h, blk), lambda b, i: (b * nblk + i, 0, 0)),
        out_shape=jax.ShapeDtypeStruct((n_seq * nblk, h, blk), F32),
        scratch_shapes=[pltpu.VMEM((h, 1), F32)],
        compiler_params=_cparams(2), name="cumsum_time",
    )(lft)


def _augment_queries(q, blk, ones_first_lane):
    lane = lax.broadcasted_iota(jnp.int32, (blk, LANES), 1)
    lo_mask = lane < LANES // 2
    zero = jnp.zeros_like(q)
    out = []
    for u, part in enumerate((jnp.where(lo_mask, q, zero), jnp.where(lo_mask, zero, q))):
        first = ones_first_lane[u]
        ones = jnp.where((lane >= first) & (lane < first + 3), 1.0, 0.0).astype(BF16)
        out.append(jnp.concatenate([part, ones], axis=1))
    return out


def _softmax_step(s, bmax, c, vt, m_ref, l_ref, u):
    group = vt.shape[0] - ONES_ROWS
    m_old = m_ref[u]
    m_new = jnp.maximum(m_old, bmax + c)
    p = jnp.exp2(s - (m_new - c))
    alpha = jnp.exp2(m_old - m_new)
    pv = _dot(vt, p.astype(BF16))
    l_ref[u] = alpha * l_ref[u] + pv[group:group + 1, :]
    m_ref[u] = m_new
    return alpha, pv[:group, :]


def _sweep_key_tiles(qi, produce, consume):
    @pl.when(qi == 0)
    def _():
        produce(0, 0, True)

    @pl.when(qi > 0)
    def _():
        n_pairs = lax.shift_right_logical(qi - 1, 1)
        odd = (qi - 1) & 1

        @pl.when(odd == 1)
        def _():
            produce(0, 0, False)
            produce(1, 1, False)
            consume(0, 0)

        @pl.when(odd == 0)
        def _():
            produce(0, 1, False)

        def body(i, carry):
            j = odd + 2 * i
            produce(j + 1, 0, False)
            consume(j, 1)
            produce(j + 2, 1, False)
            consume(j + 1, 0)
            return carry

        lax.fori_loop(0, n_pairs, body, 0)
        produce(qi, 0, True)
        consume(qi - 1, 1)

    consume(qi, 0)


def _fox_attn_body(fs_ref, q_ref, ka_ref, vt_ref, o_ref, acc_ref, m_ref, l_ref, s_ref, bm_ref, *, blk, nq):
    seq = pl.program_id(0)
    pair = pl.program_id(1)
    qi = pl.program_id(2)
    qa = _augment_queries(q_ref[0, 0], blk, (0, 3))
    acc_ref[...] = jnp.zeros_like(acc_ref)
    m_ref[...] = jnp.full_like(m_ref, NEG_INF)
    l_ref[...] = jnp.zeros_like(l_ref)
    half = FOX_HD

    def produce(kj, slot, diagonal):
        start = pl.multiple_of(kj * blk, blk)
        ka = ka_ref[0, pl.ds(start, blk), :]
        if diagonal:
            key = lax.broadcasted_iota(jnp.int32, (blk, blk), 0)
            query = lax.broadcasted_iota(jnp.int32, (blk, blk), 1)
            causal = key <= query
        for h in range(2):
            s = _dot_nt(ka, qa[h])
            if diagonal:
                s = jnp.where(causal, s, NEG_INF)
            s_ref[slot, h] = s
            bm_ref[slot, h] = jnp.max(s, axis=0, keepdims=True)

    def consume(kj, slot):
        vt = vt_ref[0, kj]
        for h in range(2):
            head = 2 * pair + h
            f_q = jnp.full((1, blk), fs_ref[(seq * nq + qi) * FOX_HEADS + head], F32)
            f_k = jnp.full((1, blk), fs_ref[(seq * nq + kj) * FOX_HEADS + head], F32)
            c = (f_q - f_k) * LOG2E
            vt_h = vt[h * (half + ONES_ROWS):(h + 1) * (half + ONES_ROWS), :]
            alpha, pv = _softmax_step(s_ref[slot, h], bm_ref[slot, h], c, vt_h, m_ref, l_ref, h)
            rows = slice(h * half, (h + 1) * half)
            acc_ref[rows, :] = alpha * acc_ref[rows, :] + pv

    _sweep_key_tiles(qi, produce, consume)
    sub = lax.broadcasted_iota(jnp.int32, (LANES, blk), 0)
    o_t = acc_ref[...] / jnp.where(sub < half, l_ref[0], l_ref[1])
    o_ref[...] = o_t.T.astype(BF16)


def _fox_attention(qkvh, kaug, vt, f_start, n_seq, t, blk):
    m = n_seq * t
    nq = t // blk
    grid_spec = pltpu.PrefetchScalarGridSpec(
        num_scalar_prefetch=1,
        grid=(n_seq, HEAD_PAIRS, nq),
        in_specs=[pl.BlockSpec((1, 1, blk, LANES), lambda b, p, i, fs: (0, p, b * nq + i, 0)),
                  pl.BlockSpec((1, t, 2 * LANES), lambda b, p, i, fs: (p, b, 0)),
                  pl.BlockSpec((1, nq, _vt_rows(FOX_HD), blk), lambda b, p, i, fs: (p, b, 0, 0))],
        out_specs=pl.BlockSpec((blk, LANES), lambda b, p, i, fs: (b * nq + i, p)),
        scratch_shapes=[pltpu.VMEM((LANES, blk), F32),
                        pltpu.VMEM((2, 1, blk), F32),
                        pltpu.VMEM((2, 1, blk), F32),
                        pltpu.VMEM((2, 2, blk, blk), F32),
                        pltpu.VMEM((2, 2, 1, blk), F32)])
    return pl.pallas_call(
        functools.partial(_fox_attn_body, blk=blk, nq=nq),
        grid_spec=grid_spec,
        out_shape=jax.ShapeDtypeStruct((m, D_MODEL), BF16),
        compiler_params=_cparams(3), name="fox_attention",
    )(f_start, qkvh, kaug, vt)


def _diff_lambda(lam_ref, lam_init):
    lp = lam_ref[...]
    a = jnp.sum(lp[0:1] * lp[1:2], axis=1, keepdims=True)
    b = jnp.sum(lp[2:3] * lp[3:4], axis=1, keepdims=True)
    return jnp.exp(a) - jnp.exp(b) + lam_init


def _subln(o, g, lam_init):
    o = o * lax.rsqrt(jnp.mean(o * o, axis=-1, keepdims=True) + LN_EPS)
    return o * g * (1.0 - lam_init)


def _diff_attn_body(q_ref, ka_ref, vt_ref, lam_ref, g_ref, o_ref, acc_ref, m_ref, l_ref, s_ref, bm_ref, *,
                    blk, lam_init):
    head = pl.program_id(1)
    qi = pl.program_id(2)
    qa = _augment_queries(q_ref[0, 0], blk, (0, 0))
    acc_ref[...] = jnp.zeros_like(acc_ref)
    m_ref[...] = jnp.full_like(m_ref, NEG_INF)
    l_ref[...] = jnp.zeros_like(l_ref)
    slope = lax.bitcast_convert_type(jnp.full((1, blk), 126 - head, jnp.int32) << 23, F32) * LOG2E

    def produce(kj, slot, diagonal):
        start = pl.multiple_of(kj * blk, blk)
        ka = ka_ref[0, pl.ds(start, blk), :]
        if diagonal:
            key = lax.broadcasted_iota(jnp.int32, (blk, blk), 0)
            query = lax.broadcasted_iota(jnp.int32, (blk, blk), 1)
            visible = (key // CHUNK) <= (query // CHUNK)
            ahead = slope * (-2.0 * jnp.maximum(key - query, 0).astype(F32))
        for u in range(2):
            s = _dot_nt(ka, qa[u])
            if diagonal:
                s = jnp.where(visible, s + ahead, NEG_INF)
            s_ref[slot, u] = s
            bm_ref[slot, u] = jnp.max(s, axis=0, keepdims=True)

    def consume(kj, slot):
        vt = vt_ref[0, kj]
        c = slope * jnp.full((1, blk), (kj - qi) * blk, jnp.int32).astype(F32)
        for u in range(2):
            alpha, pv = _softmax_step(s_ref[slot, u], bm_ref[slot, u], c, vt, m_ref, l_ref, u)
            acc_ref[u] = alpha * acc_ref[u] + pv

    _sweep_key_tiles(qi, produce, consume)
    lam = _diff_lambda(lam_ref, lam_init)
    o_t = acc_ref[0] / l_ref[0] - lam * (acc_ref[1] / l_ref[1])
    o_ref[...] = _subln(o_t.T, g_ref[...], lam_init).astype(BF16)


def _diff_attention(qkvh, kaug, vt, lam_params, subln_g, n_seq, t, blk, lam_init):
    m = n_seq * t
    nq = t // blk
    return pl.pallas_call(
        functools.partial(_diff_attn_body, blk=blk, lam_init=lam_init),
        grid=(n_seq, DIFF_HEADS, nq),
        in_specs=[pl.BlockSpec((1, 1, blk, LANES), lambda b, p, i: (0, p, b * nq + i, 0)),
                  pl.BlockSpec((1, t, 2 * LANES), lambda b, p, i: (p, b, 0)),
                  pl.BlockSpec((1, nq, _vt_rows(LANES), blk), lambda b, p, i: (p, b, 0, 0)),
                  pl.BlockSpec((4, DIFF_HD), lambda b, p, i: (0, 0)),
                  pl.BlockSpec((1, LANES), lambda b, p, i: (0, 0))],
        out_specs=pl.BlockSpec((blk, LANES), lambda b, p, i: (b * nq + i, p)),
        out_shape=jax.ShapeDtypeStruct((m, D_MODEL), BF16),
        scratch_shapes=[pltpu.VMEM((2, LANES, blk), F32),
                        pltpu.VMEM((2, 1, blk), F32),
                        pltpu.VMEM((2, 1, blk), F32),
                        pltpu.VMEM((2, 2, blk, blk), F32),
                        pltpu.VMEM((2, 2, 1, blk), F32)],
        compiler_params=_cparams(3), name="diff_attention",
    )(qkvh, kaug, vt, lam_params, subln_g)


def _decode_attn_body(*refs, mode, t_new, past_len, blk, nk, lam_init):
    if mode == "fox":
        q_ref, kn_ref, vn_ref, kc_ref, vc_ref, f_ref, o_ref, qbd_ref, acc_ref, m_ref, l_ref = refs
    else:
        q_ref, kn_ref, vn_ref, kc_ref, vc_ref, lam_ref, g_ref, o_ref, qbd_ref, acc_ref, m_ref, l_ref = refs
    kj = pl.program_id(1)
    groups = D_MODEL // FOX_HD
    rows = groups * t_new
    row1 = lax.broadcasted_iota(jnp.int32, (rows, 1), 0)

    def gather_pairs(ref):
        return jnp.concatenate([ref[0, p] for p in range(HEAD_PAIRS)], axis=1)

    @pl.when(kj == 0)
    def _():
        q = gather_pairs(q_ref)
        qt = jnp.concatenate([q] * groups, axis=0)
        r = lax.broadcasted_iota(jnp.int32, (rows, D_MODEL), 0)
        c = lax.broadcasted_iota(jnp.int32, (rows, D_MODEL), 1)
        qbd_ref[...] = jnp.where((r // t_new) == (c // FOX_HD), qt, jnp.zeros_like(qt))
        acc_ref[...] = jnp.zeros_like(acc_ref)
        m_ref[...] = jnp.full_like(m_ref, NEG_INF)
        l_ref[...] = jnp.zeros_like(l_ref)

    if mode == "fox":
        f_last = f_ref[nk - 1][:, blk - 1:blk]

        def head_rows(x):
            n = x.shape[1]
            return jnp.concatenate(
                [jnp.broadcast_to(x[h:h + 1, :], (t_new, n)) for h in range(FOX_HEADS)], axis=0)
    else:
        slope = jnp.exp2(-(row1 // (2 * t_new) + 1).astype(F32)) * LOG2E

    def update(s, v):
        m_old = m_ref[...]
        m_new = jnp.maximum(m_old, jnp.max(s, axis=1, keepdims=True))
        p = jnp.exp2(s - m_new)
        a = jnp.exp2(m_old - m_new)
        l_ref[...] = a * l_ref[...] + jnp.sum(p, axis=1, keepdims=True)
        m_ref[...] = m_new
        acc_ref[...] = a * acc_ref[...] + _dot(p.astype(BF16), v)

    def load_cache(ref):
        heads = ref.shape[1] // blk
        return jnp.concatenate(
            [ref[0, pl.ds(h, blk, stride=heads), :] for h in range(heads)], axis=1).astype(BF16)

    kc = load_cache(kc_ref)
    vc = load_cache(vc_ref)
    s = _dot_nt(qbd_ref[...], kc)
    if mode == "fox":
        s = s + head_rows((f_last - f_ref[kj]) * LOG2E)
    else:
        col = lax.broadcasted_iota(jnp.int32, (1, blk), 1)
        s = s + slope * (col + (kj * blk - past_len)).astype(F32)
    update(s, vc)

    @pl.when(kj == nk - 1)
    def _():
        kn = gather_pairs(kn_ref)
        vn = gather_pairs(vn_ref)
        s = _dot_nt(qbd_ref[...], kn)
        tq = lax.broadcasted_iota(jnp.int32, (rows, t_new), 0) % t_new
        tk = lax.broadcasted_iota(jnp.int32, (rows, t_new), 1)
        if mode == "fox":
            s = s + head_rows((f_last - f_ref[nk][:, :t_new]) * LOG2E)
            visible = tk <= tq
        else:
            s = s + slope * (tq - jnp.abs(tq - tk)).astype(F32)
            visible = ((tk + past_len) // CHUNK) <= ((tq + past_len) // CHUNK)
        update(jnp.where(visible, s, NEG_INF), vn)

        r = lax.broadcasted_iota(jnp.int32, (rows, D_MODEL), 0)
        c = lax.broadcasted_iota(jnp.int32, (rows, D_MODEL), 1)
        inv_l = 1.0 / l_ref[...]
        if mode == "fox":
            own = (r // t_new) == (c // FOX_HD)
            coef = inv_l
        else:
            own = (r // (2 * t_new)) == (c // LANES)
            lam = _diff_lambda(lam_ref, lam_init)
            coef = jnp.where((row1 // t_new) % 2 == 0, inv_l, -lam * inv_l)
        w = jnp.where(own, acc_ref[...] * coef, 0.0)
        o = w[0:t_new]
        for gi in range(1, groups):
            o = o + w[gi * t_new:(gi + 1) * t_new]
        if mode == "diff":
            g = g_ref[...]
            o = jnp.concatenate(
                [_subln(o[:, h * LANES:(h + 1) * LANES], g, lam_init) for h in range(DIFF_HEADS)], axis=1)
        o_ref[0] = o.astype(BF16)


def _decode_attention(qkvh, cache_k, cache_v, extra, mode, n_seq, t_new, blk, lam_init=0.0):
    past_len, heads, hd = cache_k.shape[1:]
    nk = past_len // blk
    rows = (D_MODEL // FOX_HD) * t_new
    new_spec = lambda s: pl.BlockSpec((1, HEAD_PAIRS, t_new, LANES), lambda b, j: (s, 0, b, 0))
    cache_k = cache_k.reshape(n_seq, past_len * heads, hd)
    cache_v = cache_v.reshape(n_seq, past_len * heads, hd)
    cache_spec = pl.BlockSpec((1, blk * heads, hd), lambda b, j: (b, j, 0))
    in_specs = [new_spec(0), new_spec(1), new_spec(2), cache_spec, cache_spec]
    if mode == "fox":
        in_specs.append(pl.BlockSpec((nk + 1, FOX_HEADS, blk), lambda b, j: (b, 0, 0)))
    else:
        in_specs += [pl.BlockSpec((4, DIFF_HD), lambda b, j: (0, 0)),
                     pl.BlockSpec((1, LANES), lambda b, j: (0, 0))]
    out = pl.pallas_call(
        functools.partial(_decode_attn_body, mode=mode, t_new=t_new, past_len=past_len, blk=blk, nk=nk,
                          lam_init=lam_init),
        grid=(n_seq, nk),
        in_specs=in_specs,
        out_specs=pl.BlockSpec((1, t_new, D_MODEL), lambda b, j: (b, 0, 0)),
        out_shape=jax.ShapeDtypeStruct((n_seq, t_new, D_MODEL), BF16),
        scratch_shapes=[pltpu.VMEM((rows, D_MODEL), BF16),
                        pltpu.VMEM((rows, D_MODEL), F32),
                        pltpu.VMEM((rows, 1), F32),
                        pltpu.VMEM((rows, 1), F32)],
        compiler_params=_cparams(2), name="decode_attention_" + mode,
    )(qkvh, qkvh, qkvh, cache_k, cache_v, *extra)
    return out.reshape(n_seq * t_new, D_MODEL)


def _oproj_ln_body(x_ref, o_ref, w_ref, g_ref, b_ref, y_ref):
    y = _dot(o_ref[...], w_ref[...])
    y_ref[...] = _layernorm(DEEPNORM_ALPHA * x_ref[...] + y, g_ref[...], b_ref[...])


def _oproj_ln(x2d, o2d, w, g, b, tm=512):
    m = x2d.shape[0]
    tm = min(tm, m)
    row = pl.BlockSpec((tm, D_MODEL), lambda i: (i, 0))
    vec = pl.BlockSpec((1, D_MODEL), lambda i: (0, 0))
    return pl.pallas_call(
        _oproj_ln_body, grid=(m // tm,),
        in_specs=[row, row, pl.BlockSpec((D_MODEL, D_MODEL), lambda i: (0, 0)), vec, vec],
        out_specs=row, out_shape=jax.ShapeDtypeStruct((m, D_MODEL), F32),
        compiler_params=_cparams(1), name="oproj_ln",
    )(x2d, o2d, w, g, b)


def _ffn_ln_body(x_ref, win_ref, wout_ref, g_ref, b_ref, y_ref):
    x = x_ref[...]
    h = _dot(x.astype(BF16), win_ref[...])
    gate = h[:, :D_FF]
    up = h[:, D_FF:]
    act = (gate * jax.nn.sigmoid(gate) * up).astype(BF16)
    y = _dot(act, wout_ref[...])
    y_ref[...] = _layernorm(DEEPNORM_ALPHA * x + y, g_ref[...], b_ref[...])


def _ffn_ln(x2d, w_in, w_out, g, b, tm=256):
    m = x2d.shape[0]
    tm = min(tm, m)
    row = pl.BlockSpec((tm, D_MODEL), lambda i: (i, 0))
    vec = pl.BlockSpec((1, D_MODEL), lambda i: (0, 0))
    return pl.pallas_call(
        _ffn_ln_body, grid=(m // tm,),
        in_specs=[row,
                  pl.BlockSpec((D_MODEL, 2 * D_FF), lambda i: (0, 0)),
                  pl.BlockSpec((D_FF, D_MODEL), lambda i: (0, 0)),
                  vec, vec],
        out_specs=row, out_shape=jax.ShapeDtypeStruct((m, D_MODEL), F32),
        compiler_params=_cparams(1), name="ffn_ln",
    )(x2d, w_in, w_out, g, b)


def _run_stream(x, past, wts):
    n_seq, t, _ = x.shape
    m = n_seq * t
    blk = ATTN_BLK
    x2d = x.reshape(m, D_MODEL)
    lam_init = 0.8 - 0.6 * math.exp(-0.3 * 1)

    qkvh, fk, fv, vt, logf, logf_pad, logft = _qkv_proj(
        x2d, wts["fox_wqkv"], wts["fox_wvt"], FOX_HD ** -0.5 * LOG2E, forget=wts["fox_forget"])
    if past is None:
        kaug, f_start = _fox_keys(logf_pad, qkvh, n_seq, t, blk)
        o = _fox_attention(qkvh, kaug, vt, f_start[:, 0, :FOX_HEADS].reshape(-1), n_seq, t, blk)
    else:
        past_k, past_v, past_lf = past[0], past[1], past[2]
        past_len = past_k.shape[1]
        lf_all = jnp.concatenate(
            [jnp.transpose(past_lf, (2, 0, 1)),
             logft.reshape(FOX_HEADS, n_seq, t),
             jnp.zeros((FOX_HEADS, n_seq, blk - t), F32)], axis=2)
        fcum = _cumsum_time(lf_all.reshape(FOX_HEADS, n_seq * (past_len + blk)), n_seq, blk)
        o = _decode_attention(qkvh, past_k, past_v, (fcum,), "fox", n_seq, t, blk)
    x2d = _oproj_ln(x2d, o, wts["fox_wout"], wts["ln_g"][0][0], wts["ln_b"][0][0])
    x2d = _ffn_ln(x2d, wts["ffn_win"][0], wts["ffn_wout"][0], wts["ln_g"][0][1], wts["ln_b"][0][1])

    qkvh, dk, dv, vt, kaug = _qkv_proj(x2d, wts["diff_wqkv"], wts["diff_wvt"], DIFF_HD ** -0.5 * LOG2E)
    if past is None:
        o = _diff_attention(qkvh, kaug, vt, wts["diff_lambda"], wts["diff_g"], n_seq, t, blk, lam_init)
    else:
        past_k, past_v = past[3], past[4]
        past_len = past_k.shape[1]
        o = _decode_attention(qkvh, past_k, past_v,
                              (wts["diff_lambda"], wts["diff_g"]), "diff", n_seq, t, blk, lam_init)
    x2d = _oproj_ln(x2d, o, wts["diff_wout"], wts["ln_g"][1][0], wts["ln_b"][1][0])
    x2d = _ffn_ln(x2d, wts["ffn_win"][1], wts["ffn_wout"][1], wts["ln_g"][1][1], wts["ln_b"][1][1])

    return (x2d.reshape(n_seq, t, D_MODEL),
            fk.reshape(1, n_seq, t, FOX_HEADS, FOX_HD), fv.reshape(1, n_seq, t, FOX_HEADS, FOX_HD),
            logf.reshape(1, n_seq, t, FOX_HEADS),
            dk.reshape(1, n_seq, t, DIFF_HEADS, 2 * DIFF_HD), dv.reshape(1, n_seq, t, DIFF_HEADS, 2 * DIFF_HD))


def _prepare_weights(fox_w_in, fox_b_f, fox_w_out, diff_w_in, diff_lambda, diff_subln_g, diff_w_out,
                     ffn_w_in, ffn_w_out, ln_g, ln_b):
    wf = fox_w_in[0][:, 3 * D_MODEL:].astype(BF16)
    wf_pad = jnp.pad(wf, ((0, 0), (0, LANES - FOX_HEADS)))
    bf_pad = jnp.pad(fox_b_f[0].reshape(1, FOX_HEADS), ((0, 0), (0, LANES - FOX_HEADS)))
    return {
        "fox_wqkv": fox_w_in[0][:, :3 * D_MODEL].astype(BF16),
        "fox_wvt": fox_w_in[0][:, 2 * D_MODEL:3 * D_MODEL].T.astype(BF16),
        "fox_forget": (wf_pad, wf.T, bf_pad, fox_b_f[0].reshape(FOX_HEADS, 1)),
        "fox_wout": fox_w_out[0].astype(BF16),
        "diff_wqkv": diff_w_in[0].astype(BF16),
        "diff_wvt": diff_w_in[0][:, 2 * D_MODEL:].T.astype(BF16),
        "diff_lambda": diff_lambda[0],
        "diff_g": diff_subln_g[0].reshape(1, 2 * DIFF_HD),
        "diff_wout": diff_w_out[0].astype(BF16),
        "ffn_win": [ffn_w_in[i].astype(BF16) for i in range(DEPTH)],
        "ffn_wout": [ffn_w_out[i].astype(BF16) for i in range(DEPTH)],
        "ln_g": [[ln_g[i, j].reshape(1, D_MODEL) for j in range(2)] for i in range(DEPTH)],
        "ln_b": [[ln_b[i, j].reshape(1, D_MODEL) for j in range(2)] for i in range(DEPTH)],
    }


def kernel(x_prompt, x_sample, cache_fox_k, cache_fox_v, cache_fox_logf, cache_diff_k, cache_diff_v, fox_w_in, fox_b_f, fox_w_out, diff_w_in, diff_lambda, diff_subln_g, diff_w_out, ffn_w_in, ffn_w_out, ln_g, ln_b):
    wts = _prepare_weights(fox_w_in, fox_b_f, fox_w_out, diff_w_in, diff_lambda, diff_subln_g, diff_w_out,
                           ffn_w_in, ffn_w_out, ln_g, ln_b)
    y_p, fk_p, fv_p, lf_p, dk_p, dv_p = _run_stream(x_prompt, None, wts)
    past = (cache_fox_k[0], cache_fox_v[0], cache_fox_logf[0], cache_diff_k[0], cache_diff_v[0])
    y_s, fk_s, fv_s, lf_s, dk_s, dv_s = _run_stream(x_sample, past, wts)
    return (y_p, y_s, fk_p, fv_p, lf_p, dk_p, dv_p, fk_s, fv_s, lf_s, dk_s, dv_s)
```

```python
import functools
import math

import jax
import jax.numpy as jnp
from jax import lax
from jax.experimental import pallas as pl
from jax.experimental.pallas import tpu as pltpu

F32 = jnp.float32
BF16 = jnp.bfloat16

D_MODEL = 1024
DEPTH = 2
FOX_HEADS = 16
FOX_HD = 64
DIFF_HEADS = 8
DIFF_HD = 64
D_FF = 2816
CHUNK = 64
DEEPNORM_ALPHA = (2.0 * DEPTH) ** 0.25
LN_EPS = 1e-5
NEG_INF = -1e30
LOG2E = 1.4426950408889634

LANES = 128
HEAD_PAIRS = D_MODEL // LANES
ATTN_BLK = 512
ONES_ROWS = 16
DECODE_FOX_BLK = 1024
VMEM_LIMIT = 56 * 1024 * 1024


def _cparams(n_axes):
    return pltpu.CompilerParams(dimension_semantics=("arbitrary",) * n_axes,
                                vmem_limit_bytes=VMEM_LIMIT)


def _log_sigmoid(z):
    return jnp.minimum(z, 0.0) - jnp.log1p(jnp.exp(-jnp.abs(z)))


def _layernorm(z, g, b):
    mu = jnp.mean(z, axis=-1, keepdims=True)
    zc = z - mu
    var = jnp.mean(zc * zc, axis=-1, keepdims=True)
    return zc * lax.rsqrt(var + LN_EPS) * g + b


def _dot(a, b):
    return jnp.dot(a, b, preferred_element_type=F32)


def _dot_nt(a, b):
    return lax.dot_general(a, b, (((1,), (1,)), ((), ())), preferred_element_type=F32)


def _vt_rows(group):
    return (LANES // group) * (group + ONES_ROWS)


def _split3(x):
    hi = x.astype(BF16).astype(F32)
    r = x - hi
    mid = r.astype(BF16).astype(F32)
    lo = (r - mid).astype(BF16).astype(F32)
    return hi, mid, lo


def _proj_body(*refs, q_scale, with_forget):
    if with_forget:
        (x_ref, w_ref, wvt_ref, wkt_ref, wf_ref, wft_ref, bf_ref, bft_ref,
         qkv_ref, k_ref, v_ref, vt_ref, lf_ref, lfp_ref, lft_ref) = refs
    else:
        x_ref, w_ref, wvt_ref, qkv_ref, k_ref, v_ref, vt_ref, ka_ref = refs
    j = pl.program_id(1)
    tm = x_ref.shape[0]
    xb = x_ref[...].astype(BF16)

    def store_heads(val):
        vb = val.astype(BF16)
        for p in range(HEAD_PAIRS):
            qkv_ref[0, p] = vb[:, p * LANES:(p + 1) * LANES]
        return vb

    @pl.when(j == 0)
    def _():
        store_heads(_dot(xb, w_ref[...]) * q_scale)
        if with_forget:
            lfp = _log_sigmoid(_dot(xb, wf_ref[...]) + bf_ref[...])
            lfp_ref[...] = lfp
            lf_ref[...] = lfp[:, :FOX_HEADS]
            lft_ref[...] = _log_sigmoid(_dot_nt(wft_ref[...], xb) + bft_ref[...])

    @pl.when(j == 1)
    def _():
        acc = _dot(xb, w_ref[...])
        kb = store_heads(acc)
        if with_forget:
            k_ref[0] = _dot_nt(wkt_ref[...], xb)
        else:
            k_ref[...] = acc
            lane = lax.broadcasted_iota(jnp.int32, (tm, LANES), 1)
            r_local = lax.broadcasted_iota(jnp.int32, (tm, LANES), 0).astype(F32)
            for h in range(DIFF_HEADS):
                hi, mid, lo = _split3(r_local * (2.0 ** -(h + 1) * LOG2E))
                cols = jnp.where(lane == 0, hi, jnp.where(lane == 1, mid, jnp.where(lane == 2, lo, 0.0)))
                ka_ref[h, :, 0:LANES] = kb[:, h * LANES:(h + 1) * LANES]
                ka_ref[h, :, LANES:2 * LANES] = cols.astype(BF16)

    @pl.when(j == 2)
    def _():
        vt32 = _dot_nt(wvt_ref[...], xb)
        if with_forget:
            v_ref[0] = vt32
        else:
            acc = _dot(xb, w_ref[...])
            store_heads(acc)
            v_ref[...] = acc
        vt = vt32.astype(BF16)
        ones = jnp.ones((ONES_ROWS, tm), BF16)
        group = FOX_HD if with_forget else LANES
        for p in range(HEAD_PAIRS):
            for gi in range(LANES // group):
                src = p * LANES + gi * group
                dst = gi * (group + ONES_ROWS)
                vt_ref[p, 0, dst:dst + group, :] = vt[src:src + group, :]
                vt_ref[p, 0, dst + group:dst + group + ONES_ROWS, :] = ones


def _qkv_proj(x2d, n_seq, w_qkv, w_vt, q_scale, forget=None):
    m = x2d.shape[0]
    tm = min(ATTN_BLK, m)
    nblk = m // tm
    group = FOX_HD if forget is not None else LANES
    vt_rows = _vt_rows(group)
    row = pl.BlockSpec((tm, D_MODEL), lambda i, j: (i, 0))
    const = lambda shape: pl.BlockSpec(shape, lambda i, j: (0,) * len(shape))
    if forget is not None:
        n_parts = 2
        w_spec = pl.BlockSpec((D_MODEL, D_MODEL), lambda i, j: (0, jnp.minimum(j, 1)))
        per_seq = nblk // n_seq
        kv_shape = jax.ShapeDtypeStruct((n_seq, D_MODEL, m // n_seq), F32)
        kv_spec = pl.BlockSpec((1, D_MODEL, tm), lambda i, j: (i // per_seq, 0, i % per_seq))
    else:
        n_parts = 3
        w_spec = pl.BlockSpec((D_MODEL, D_MODEL), lambda i, j: (0, j))
        kv_shape = jax.ShapeDtypeStruct((m, D_MODEL), F32)
        kv_spec = row
    in_specs = [row, w_spec, const((D_MODEL, D_MODEL))]
    out_shape = [jax.ShapeDtypeStruct((n_parts, HEAD_PAIRS, m, LANES), BF16),
                 kv_shape, kv_shape,
                 jax.ShapeDtypeStruct((HEAD_PAIRS, nblk, vt_rows, tm), BF16)]
    out_specs = [pl.BlockSpec((1, HEAD_PAIRS, tm, LANES), lambda i, j: (jnp.minimum(j, n_parts - 1), 0, i, 0)),
                 kv_spec, kv_spec,
                 pl.BlockSpec((HEAD_PAIRS, 1, vt_rows, tm), lambda i, j: (0, i, 0, 0))]
    args = [x2d, w_qkv, w_vt]
    if forget is not None:
        w_kt, wf, wft, bf, bft = forget
        in_specs.append(const((D_MODEL, D_MODEL)))
        args.append(w_kt)
        in_specs += [const((D_MODEL, LANES)), const((FOX_HEADS, D_MODEL)), const((1, LANES)), const((FOX_HEADS, 1))]
        out_shape += [jax.ShapeDtypeStruct((m, FOX_HEADS), F32),
                      jax.ShapeDtypeStruct((m, LANES), F32),
                      jax.ShapeDtypeStruct((FOX_HEADS, m), F32)]
        out_specs += [pl.BlockSpec((tm, FOX_HEADS), lambda i, j: (i, 0)),
                      pl.BlockSpec((tm, LANES), lambda i, j: (i, 0)),
                      pl.BlockSpec((FOX_HEADS, tm), lambda i, j: (0, i))]
        args += [wf, wft, bf, bft]
    else:
        out_shape.append(jax.ShapeDtypeStruct((DIFF_HEADS, m, 2 * LANES), BF16))
        out_specs.append(pl.BlockSpec((DIFF_HEADS, tm, 2 * LANES), lambda i, j: (0, i, 0)))
    return pl.pallas_call(
        functools.partial(_proj_body, q_scale=q_scale, with_forget=forget is not None),
        grid=(nblk, 3), in_specs=in_specs, out_specs=out_specs, out_shape=out_shape,
        compiler_params=_cparams(2), name="qkv_proj_f" if forget is not None else "qkv_proj",
    )(*args)


def _fox_keys_body(lf_ref, k_ref, ka_ref, fs_ref, carry_ref, *, blk):
    i = pl.program_id(1)

    @pl.when(i == 0)
    def _():
        carry_ref[...] = jnp.zeros_like(carry_ref)

    row = lax.broadcasted_iota(jnp.int32, (blk, blk), 0)
    col = lax.broadcasted_iota(jnp.int32, (blk, blk), 1)
    tri = (col <= row).astype(BF16)
    hi, mid, lo = _split3(lf_ref[...])
    f_loc = _dot(tri, hi.astype(BF16)) + _dot(tri, mid.astype(BF16)) + _dot(tri, lo.astype(BF16))
    fs_ref[0] = carry_ref[...]
    carry_ref[...] = carry_ref[...] + f_loc[blk - 1:blk, :]

    bhi, bmid, blo = (t.astype(BF16) for t in _split3(f_loc * (-LOG2E)))
    sr = lax.broadcasted_iota(jnp.int32, (LANES, LANES), 0)
    sc = lax.broadcasted_iota(jnp.int32, (LANES, LANES), 1)
    for p in range(HEAD_PAIRS):
        def pick(term):
            return (((sr == 2 * p) & (sc == term)) | ((sr == 2 * p + 1) & (sc == 3 + term))).astype(BF16)
        cols = _dot(bhi, pick(0)) + _dot(bmid, pick(1)) + _dot(blo, pick(2))
        ka_ref[p, :, 0:LANES] = k_ref[0, p]
        ka_ref[p, :, LANES:2 * LANES] = cols.astype(BF16)


def _fox_keys(lf_pad, qkvh, n_seq, t, blk):
    m = n_seq * t
    nblk = t // blk
    return pl.pallas_call(
        functools.partial(_fox_keys_body, blk=blk),
        grid=(n_seq, nblk),
        in_specs=[pl.BlockSpec((blk, LANES), lambda b, i: (b * nblk + i, 0)),
                  pl.BlockSpec((1, HEAD_PAIRS, blk, LANES), lambda b, i: (1, 0, b * nblk + i, 0))],
        out_specs=[pl.BlockSpec((HEAD_PAIRS, blk, 2 * LANES), lambda b, i: (0, b * nblk + i, 0)),
                   pl.BlockSpec((1, 1, LANES), lambda b, i: (b * nblk + i, 0, 0))],
        out_shape=[jax.ShapeDtypeStruct((HEAD_PAIRS, m, 2 * LANES), BF16),
                   jax.ShapeDtypeStruct((n_seq * nblk, 1, LANES), F32)],
        scratch_shapes=[pltpu.VMEM((1, LANES), F32)],
        compiler_params=_cparams(2), name="fox_keys",
    )(lf_pad, qkvh)


def _cumsum_body(x_ref, o_ref, carry_ref, tri_ref, *, blk):
    i = pl.program_id(1)

    @pl.when((pl.program_id(0) == 0) & (i == 0))
    def _():
        row = lax.broadcasted_iota(jnp.int32, (blk, blk), 0)
        col = lax.broadcasted_iota(jnp.int32, (blk, blk), 1)
        tri_ref[...] = (row <= col).astype(BF16)

    @pl.when(i == 0)
    def _():
        carry_ref[...] = jnp.zeros_like(carry_ref)

    hi, mid, lo = _split3(x_ref[0])
    tri = tri_ref[...]
    cs = _dot(hi.astype(BF16), tri) + _dot(mid.astype(BF16), tri) + _dot(lo.astype(BF16), tri) + carry_ref[...]
    o_ref[0] = cs
    carry_ref[...] = cs[:, blk - 1:blk]


def _cumsum_time(lf, blk):
    n_seq, h, r = lf.shape
    nblk = r // blk
    return pl.pallas_call(
        functools.partial(_cumsum_body, blk=blk),
        grid=(n_seq, nblk),
        in_specs=[pl.BlockSpec((1, h, blk), lambda b, i: (b, 0, i))],
        out_specs=pl.BlockSpec((1, h, blk), lambda b, i: (b * nblk + i, 0, 0)),
        out_shape=jax.ShapeDtypeStruct((n_seq * nblk, h, blk), F32),
        scratch_shapes=[pltpu.VMEM((h, 1), F32), pltpu.VMEM((blk, blk), BF16)],
        compiler_params=_cparams(2), name="cumsum_time",
    )(lf)


def _augment_queries(q, blk, ones_first_lane):
    lane = lax.broadcasted_iota(jnp.int32, (blk, LANES), 1)
    lo_mask = lane < LANES // 2
    zero = jnp.zeros_like(q)
    out = []
    for u, part in enumerate((jnp.where(lo_mask, q, zero), jnp.where(lo_mask, zero, q))):
        first = ones_first_lane[u]
        ones = jnp.where((lane >= first) & (lane < first + 3), 1.0, 0.0).astype(BF16)
        out.append(jnp.concatenate([part, ones], axis=1))
    return out


def _softmax_step(s, bmax, c, vt, m_ref, l_ref, u):
    group = vt.shape[0] - ONES_ROWS
    m_old = m_ref[u]
    m_new = jnp.maximum(m_old, bmax + c)
    p = jnp.exp2(s - (m_new - c))
    alpha = jnp.exp2(m_old - m_new)
    pv = _dot(vt, p.astype(BF16))
    l_ref[u] = alpha * l_ref[u] + pv[group:group + 1, :]
    m_ref[u] = m_new
    return alpha, pv[:group, :]


def _sweep_key_tiles(qi, produce, consume):
    @pl.when(qi == 0)
    def _():
        produce(0, 0, True)

    @pl.when(qi > 0)
    def _():
        n_pairs = lax.shift_right_logical(qi - 1, 1)
        odd = (qi - 1) & 1

        @pl.when(odd == 1)
        def _():
            produce(0, 0, False)
            produce(1, 1, False)
            consume(0, 0)

        @pl.when(odd == 0)
        def _():
            produce(0, 1, False)

        def body(i, carry):
            j = odd + 2 * i
            produce(j + 1, 0, False)
            consume(j, 1)
            produce(j + 2, 1, False)
            consume(j + 1, 0)
            return carry

        lax.fori_loop(0, n_pairs, body, 0)
        produce(qi, 0, True)
        consume(qi - 1, 1)

    consume(qi, 0)


def _fox_attn_body(fs_ref, q_ref, ka_ref, vt_ref, o_ref, acc_ref, m_ref, l_ref, s_ref, bm_ref, *, blk, nq):
    seq = pl.program_id(0)
    pair = pl.program_id(1)
    qi = pl.program_id(2)
    qa = _augment_queries(q_ref[0, 0], blk, (0, 3))
    acc_ref[...] = jnp.zeros_like(acc_ref)
    m_ref[...] = jnp.full_like(m_ref, NEG_INF)
    l_ref[...] = jnp.zeros_like(l_ref)
    half = FOX_HD

    def produce(kj, slot, diagonal):
        start = pl.multiple_of(kj * blk, blk)
        ka = ka_ref[0, pl.ds(start, blk), :]
        if diagonal:
            key = lax.broadcasted_iota(jnp.int32, (blk, blk), 0)
            query = lax.broadcasted_iota(jnp.int32, (blk, blk), 1)
            causal = key <= query
        for h in range(2):
            s = _dot_nt(ka, qa[h])
            if diagonal:
                s = jnp.where(causal, s, NEG_INF)
            s_ref[slot, h] = s
            bm_ref[slot, h] = jnp.max(s, axis=0, keepdims=True)

    def consume(kj, slot):
        vt = vt_ref[0, kj]
        for h in range(2):
            head = 2 * pair + h
            f_q = jnp.full((1, blk), fs_ref[(seq * nq + qi) * FOX_HEADS + head], F32)
            f_k = jnp.full((1, blk), fs_ref[(seq * nq + kj) * FOX_HEADS + head], F32)
            c = (f_q - f_k) * LOG2E
            vt_h = vt[h * (half + ONES_ROWS):(h + 1) * (half + ONES_ROWS), :]
            alpha, pv = _softmax_step(s_ref[slot, h], bm_ref[slot, h], c, vt_h, m_ref, l_ref, h)
            rows = slice(h * half, (h + 1) * half)
            acc_ref[rows, :] = alpha * acc_ref[rows, :] + pv

    _sweep_key_tiles(qi, produce, consume)
    sub = lax.broadcasted_iota(jnp.int32, (LANES, blk), 0)
    o_t = acc_ref[...] / jnp.where(sub < half, l_ref[0], l_ref[1])
    o_ref[...] = o_t.T.astype(BF16)


def _fox_attention(qkvh, kaug, vt, f_start, n_seq, t, blk):
    m = n_seq * t
    nq = t // blk
    grid_spec = pltpu.PrefetchScalarGridSpec(
        num_scalar_prefetch=1,
        grid=(n_seq, HEAD_PAIRS, nq),
        in_specs=[pl.BlockSpec((1, 1, blk, LANES), lambda b, p, i, fs: (0, p, b * nq + i, 0)),
                  pl.BlockSpec((1, t, 2 * LANES), lambda b, p, i, fs: (p, b, 0)),
                  pl.BlockSpec((1, nq, _vt_rows(FOX_HD), blk), lambda b, p, i, fs: (p, b, 0, 0))],
        out_specs=pl.BlockSpec((blk, LANES), lambda b, p, i, fs: (b * nq + i, p)),
        scratch_shapes=[pltpu.VMEM((LANES, blk), F32),
                        pltpu.VMEM((2, 1, blk), F32),
                        pltpu.VMEM((2, 1, blk), F32),
                        pltpu.VMEM((2, 2, blk, blk), F32),
                        pltpu.VMEM((2, 2, 1, blk), F32)])
    return pl.pallas_call(
        functools.partial(_fox_attn_body, blk=blk, nq=nq),
        grid_spec=grid_spec,
        out_shape=jax.ShapeDtypeStruct((m, D_MODEL), BF16),
        compiler_params=_cparams(3), name="fox_attention",
    )(f_start, qkvh, kaug, vt)


def _diff_lambda(lam_ref, lam_init):
    lp = lam_ref[...]
    a = jnp.sum(lp[0:1] * lp[1:2], axis=1, keepdims=True)
    b = jnp.sum(lp[2:3] * lp[3:4], axis=1, keepdims=True)
    return jnp.exp(a) - jnp.exp(b) + lam_init


def _subln(o, g, lam_init):
    o = o * lax.rsqrt(jnp.mean(o * o, axis=-1, keepdims=True) + LN_EPS)
    return o * g * (1.0 - lam_init)


def _diff_attn_body(q_ref, ka_ref, vt_ref, lam_ref, g_ref, o_ref, acc_ref, m_ref, l_ref, s_ref, bm_ref, *,
                    blk, lam_init):
    head = pl.program_id(1)
    qi = pl.program_id(2)
    qa = _augment_queries(q_ref[0, 0], blk, (0, 0))
    acc_ref[...] = jnp.zeros_like(acc_ref)
    m_ref[...] = jnp.full_like(m_ref, NEG_INF)
    l_ref[...] = jnp.zeros_like(l_ref)
    slope = lax.bitcast_convert_type(jnp.full((1, blk), 126 - head, jnp.int32) << 23, F32) * LOG2E

    def produce(kj, slot, diagonal):
        start = pl.multiple_of(kj * blk, blk)
        ka = ka_ref[0, pl.ds(start, blk), :]
        if diagonal:
            key = lax.broadcasted_iota(jnp.int32, (blk, blk), 0)
            query = lax.broadcasted_iota(jnp.int32, (blk, blk), 1)
            visible = (key // CHUNK) <= (query // CHUNK)
            ahead = slope * (-2.0 * jnp.maximum(key - query, 0).astype(F32))
        for u in range(2):
            s = _dot_nt(ka, qa[u])
            if diagonal:
                s = jnp.where(visible, s + ahead, NEG_INF)
            s_ref[slot, u] = s
            bm_ref[slot, u] = jnp.max(s, axis=0, keepdims=True)

    def consume(kj, slot):
        vt = vt_ref[0, kj]
        c = slope * jnp.full((1, blk), (kj - qi) * blk, jnp.int32).astype(F32)
        for u in range(2):
            alpha, pv = _softmax_step(s_ref[slot, u], bm_ref[slot, u], c, vt, m_ref, l_ref, u)
            acc_ref[u] = alpha * acc_ref[u] + pv

    _sweep_key_tiles(qi, produce, consume)
    lam = _diff_lambda(lam_ref, lam_init)
    o_t = acc_ref[0] / l_ref[0] - lam * (acc_ref[1] / l_ref[1])
    o_ref[...] = _subln(o_t.T, g_ref[...], lam_init).astype(BF16)


def _diff_attention(qkvh, kaug, vt, lam_params, subln_g, n_seq, t, blk, lam_init):
    m = n_seq * t
    nq = t // blk
    return pl.pallas_call(
        functools.partial(_diff_attn_body, blk=blk, lam_init=lam_init),
        grid=(n_seq, DIFF_HEADS, nq),
        in_specs=[pl.BlockSpec((1, 1, blk, LANES), lambda b, p, i: (0, p, b * nq + i, 0)),
                  pl.BlockSpec((1, t, 2 * LANES), lambda b, p, i: (p, b, 0)),
                  pl.BlockSpec((1, nq, _vt_rows(LANES), blk), lambda b, p, i: (p, b, 0, 0)),
                  pl.BlockSpec((4, DIFF_HD), lambda b, p, i: (0, 0)),
                  pl.BlockSpec((1, LANES), lambda b, p, i: (0, 0))],
        out_specs=pl.BlockSpec((blk, LANES), lambda b, p, i: (b * nq + i, p)),
        out_shape=jax.ShapeDtypeStruct((m, D_MODEL), BF16),
        scratch_shapes=[pltpu.VMEM((2, LANES, blk), F32),
                        pltpu.VMEM((2, 1, blk), F32),
                        pltpu.VMEM((2, 1, blk), F32),
                        pltpu.VMEM((2, 2, blk, blk), F32),
                        pltpu.VMEM((2, 2, 1, blk), F32)],
        compiler_params=_cparams(3), name="diff_attention",
    )(qkvh, kaug, vt, lam_params, subln_g)


def _decode_diff_body(q_ref, kn_ref, vn_ref, kc_ref, vc_ref, lam_ref, g_ref, o_ref, qbd_ref, acc_ref, m_ref,
                      l_ref, *, t_new, past_len, blk, nk, lam_init):
    kj = pl.program_id(1)
    groups = 2 * DIFF_HEADS
    rows = groups * t_new
    row1 = lax.broadcasted_iota(jnp.int32, (rows, 1), 0)

    def gather_pairs(ref):
        return jnp.concatenate([ref[0, p] for p in range(HEAD_PAIRS)], axis=1)

    @pl.when(kj == 0)
    def _():
        q = gather_pairs(q_ref)
        qt = jnp.concatenate([q] * groups, axis=0)
        r = lax.broadcasted_iota(jnp.int32, (rows, D_MODEL), 0)
        c = lax.broadcasted_iota(jnp.int32, (rows, D_MODEL), 1)
        qbd_ref[...] = jnp.where((r // t_new) == (c // DIFF_HD), qt, jnp.zeros_like(qt))
        acc_ref[...] = jnp.zeros_like(acc_ref)
        m_ref[...] = jnp.full_like(m_ref, NEG_INF)
        l_ref[...] = jnp.zeros_like(l_ref)

    slope = jnp.exp2(-(row1 // (2 * t_new) + 1).astype(F32)) * LOG2E

    def update(s, v):
        m_old = m_ref[...]
        m_new = jnp.maximum(m_old, jnp.max(s, axis=1, keepdims=True))
        p = jnp.exp2(s - m_new)
        a = jnp.exp2(m_old - m_new)
        l_ref[...] = a * l_ref[...] + jnp.sum(p, axis=1, keepdims=True)
        m_ref[...] = m_new
        acc_ref[...] = a * acc_ref[...] + _dot(p.astype(BF16), v)

    def load_cache(ref):
        heads = ref.shape[1] // blk
        return jnp.concatenate(
            [ref[0, pl.ds(h, blk, stride=heads), :] for h in range(heads)], axis=1).astype(BF16)

    kc = load_cache(kc_ref)
    vc = load_cache(vc_ref)
    s = _dot_nt(qbd_ref[...], kc)
    col = lax.broadcasted_iota(jnp.int32, (1, blk), 1)
    update(s + slope * (col + (kj * blk - past_len)).astype(F32), vc)

    @pl.when(kj == nk - 1)
    def _():
        kn = gather_pairs(kn_ref)
        vn = gather_pairs(vn_ref)
        s = _dot_nt(qbd_ref[...], kn)
        tq = lax.broadcasted_iota(jnp.int32, (rows, t_new), 0) % t_new
        tk = lax.broadcasted_iota(jnp.int32, (rows, t_new), 1)
        s = s + slope * (tq - jnp.abs(tq - tk)).astype(F32)
        visible = ((tk + past_len) // CHUNK) <= ((tq + past_len) // CHUNK)
        update(jnp.where(visible, s, NEG_INF), vn)

        r = lax.broadcasted_iota(jnp.int32, (rows, D_MODEL), 0)
        c = lax.broadcasted_iota(jnp.int32, (rows, D_MODEL), 1)
        inv_l = 1.0 / l_ref[...]
        own = (r // (2 * t_new)) == (c // LANES)
        lam = _diff_lambda(lam_ref, lam_init)
        coef = jnp.where((row1 // t_new) % 2 == 0, inv_l, -lam * inv_l)
        w = jnp.where(own, acc_ref[...] * coef, 0.0)
        o = w[0:t_new]
        for gi in range(1, groups):
            o = o + w[gi * t_new:(gi + 1) * t_new]
        g = g_ref[...]
        o = jnp.concatenate(
            [_subln(o[:, h * LANES:(h + 1) * LANES], g, lam_init) for h in range(DIFF_HEADS)], axis=1)
        o_ref[0] = o.astype(BF16)


def _decode_diff_attention(qkvh, cache_k, cache_v, lam_params, subln_g, n_seq, t_new, blk, lam_init):
    past_len, heads, hd = cache_k.shape[1:]
    nk = past_len // blk
    rows = 2 * DIFF_HEADS * t_new
    new_spec = lambda s: pl.BlockSpec((1, HEAD_PAIRS, t_new, LANES), lambda b, j: (s, 0, b, 0))
    cache_k = cache_k.reshape(n_seq, past_len * heads, hd)
    cache_v = cache_v.reshape(n_seq, past_len * heads, hd)
    cache_spec = pl.BlockSpec((1, blk * heads, hd), lambda b, j: (b, j, 0))
    out = pl.pallas_call(
        functools.partial(_decode_diff_body, t_new=t_new, past_len=past_len, blk=blk, nk=nk, lam_init=lam_init),
        grid=(n_seq, nk),
        in_specs=[new_spec(0), new_spec(1), new_spec(2), cache_spec, cache_spec,
                  pl.BlockSpec((4, DIFF_HD), lambda b, j: (0, 0)),
                  pl.BlockSpec((1, LANES), lambda b, j: (0, 0))],
        out_specs=pl.BlockSpec((1, t_new, D_MODEL), lambda b, j: (b, 0, 0)),
        out_shape=jax.ShapeDtypeStruct((n_seq, t_new, D_MODEL), BF16),
        scratch_shapes=[pltpu.VMEM((rows, D_MODEL), BF16),
                        pltpu.VMEM((rows, D_MODEL), F32),
                        pltpu.VMEM((rows, 1), F32),
                        pltpu.VMEM((rows, 1), F32)],
        compiler_params=_cparams(2), name="decode_attention_diff",
    )(qkvh, qkvh, qkvh, cache_k, cache_v, lam_params, subln_g)
    return out.reshape(n_seq * t_new, D_MODEL)


def _decode_fox_body(q_ref, knt_ref, vnt_ref, kt_ref, vt_ref, f_ref, o_ref, acc_ref, m_ref, l_ref, *,
                     t_new, blk, nk):
    kj = pl.program_id(1)
    lane = lax.broadcasted_iota(jnp.int32, (t_new, LANES), 1)
    lo_mask = lane < FOX_HD

    @pl.when(kj == 0)
    def _():
        acc_ref[...] = jnp.zeros_like(acc_ref)
        m_ref[...] = jnp.full_like(m_ref, NEG_INF)
        l_ref[...] = jnp.zeros_like(l_ref)

    f_last = f_ref[nk - 1][:, blk - 1:blk]

    def attend(kt_pairs, vt_pairs, bias, visible):
        heads = range(FOX_HEADS)
        scores = []
        for p in range(HEAD_PAIRS):
            q = q_ref[0, p]
            zero = jnp.zeros_like(q)
            kt = kt_pairs(p)
            for qh in (jnp.where(lo_mask, q, zero), jnp.where(lo_mask, zero, q)):
                h = len(scores)
                s = _dot(qh, kt) + bias[h:h + 1, :]
                scores.append(s if visible is None else jnp.where(visible, s, NEG_INF))
        m_old = [m_ref[h] for h in heads]
        m_new = [jnp.maximum(m_old[h], jnp.max(scores[h], axis=1, keepdims=True)) for h in heads]
        probs = [jnp.exp2(scores[h] - m_new[h]) for h in heads]
        alpha = [jnp.exp2(m_old[h] - m_new[h]) for h in heads]
        for h in heads:
            l_ref[h] = alpha[h] * l_ref[h] + jnp.sum(probs[h], axis=1, keepdims=True)
            m_ref[h] = m_new[h]
        for p in range(HEAD_PAIRS):
            vt = vt_pairs(p)
            pv = [_dot_nt(probs[2 * p + e].astype(BF16), vt) for e in range(2)]
            acc = acc_ref[p]
            acc_ref[p] = jnp.where(lo_mask, alpha[2 * p] * acc + pv[0], alpha[2 * p + 1] * acc + pv[1])

    attend(lambda p: kt_ref[0, p].astype(BF16), lambda p: vt_ref[0, p].astype(BF16),
           (f_last - f_ref[kj]) * LOG2E, None)

    @pl.when(kj == nk - 1)
    def _():
        tq = lax.broadcasted_iota(jnp.int32, (t_new, t_new), 0)
        tk = lax.broadcasted_iota(jnp.int32, (t_new, t_new), 1)
        attend(lambda p: knt_ref[0, p * LANES:(p + 1) * LANES, :].astype(BF16),
               lambda p: vnt_ref[0, p * LANES:(p + 1) * LANES, :].astype(BF16),
               (f_last - f_ref[nk][:, :t_new]) * LOG2E, tk <= tq)
        o_ref[0] = jnp.concatenate(
            [acc_ref[p] / jnp.where(lo_mask, l_ref[2 * p], l_ref[2 * p + 1]) for p in range(HEAD_PAIRS)],
            axis=1).astype(BF16)


def _decode_fox_attention(qkvh, knt, vnt, cache_kt, cache_vt, fcum, n_seq, t_new, blk):
    past_len = cache_kt.shape[3]
    nk = past_len // blk
    cache_spec = pl.BlockSpec((1, HEAD_PAIRS, LANES, blk), lambda b, j: (b, 0, 0, j))
    new_spec = pl.BlockSpec((1, D_MODEL, t_new), lambda b, j: (b, 0, 0))
    out = pl.pallas_call(
        functools.partial(_decode_fox_body, t_new=t_new, blk=blk, nk=nk),
        grid=(n_seq, nk),
        in_specs=[pl.BlockSpec((1, HEAD_PAIRS, t_new, LANES), lambda b, j: (0, 0, b, 0)),
                  new_spec, new_spec, cache_spec, cache_spec,
                  pl.BlockSpec((nk + 1, FOX_HEADS, blk), lambda b, j: (b, 0, 0))],
        out_specs=pl.BlockSpec((1, t_new, D_MODEL), lambda b, j: (b, 0, 0)),
        out_shape=jax.ShapeDtypeStruct((n_seq, t_new, D_MODEL), BF16),
        scratch_shapes=[pltpu.VMEM((HEAD_PAIRS, t_new, LANES), F32),
                        pltpu.VMEM((FOX_HEADS, t_new, 1), F32),
                        pltpu.VMEM((FOX_HEADS, t_new, 1), F32)],
        compiler_params=_cparams(2), name="decode_attention_fox",
    )(qkvh, knt, vnt, cache_kt, cache_vt, fcum)
    return out.reshape(n_seq * t_new, D_MODEL)


def _oproj_ln_body(x_ref, o_ref, w_ref, g_ref, b_ref, y_ref):
    y = _dot(o_ref[...], w_ref[...])
    y_ref[...] = _layernorm(DEEPNORM_ALPHA * x_ref[...] + y, g_ref[...], b_ref[...])


def _oproj_ln(x2d, o2d, w, g, b, tm=512):
    m = x2d.shape[0]
    tm = min(tm, m)
    row = pl.BlockSpec((tm, D_MODEL), lambda i: (i, 0))
    vec = pl.BlockSpec((1, D_MODEL), lambda i: (0, 0))
    return pl.pallas_call(
        _oproj_ln_body, grid=(m // tm,),
        in_specs=[row, row, pl.BlockSpec((D_MODEL, D_MODEL), lambda i: (0, 0)), vec, vec],
        out_specs=row, out_shape=jax.ShapeDtypeStruct((m, D_MODEL), F32),
        compiler_params=_cparams(1), name="oproj_ln",
    )(x2d, o2d, w, g, b)


def _ffn_ln_body(x_ref, win_ref, wout_ref, g_ref, b_ref, y_ref):
    x = x_ref[...]
    h = _dot(x.astype(BF16), win_ref[...])
    gate = h[:, :D_FF]
    up = h[:, D_FF:]
    act = (gate * jax.nn.sigmoid(gate) * up).astype(BF16)
    y = _dot(act, wout_ref[...])
    y_ref[...] = _layernorm(DEEPNORM_ALPHA * x + y, g_ref[...], b_ref[...])


def _ffn_ln(x2d, w_in, w_out, g, b, tm=256):
    m = x2d.shape[0]
    tm = min(tm, m)
    row = pl.BlockSpec((tm, D_MODEL), lambda i: (i, 0))
    vec = pl.BlockSpec((1, D_MODEL), lambda i: (0, 0))
    return pl.pallas_call(
        _ffn_ln_body, grid=(m // tm,),
        in_specs=[row,
                  pl.BlockSpec((D_MODEL, 2 * D_FF), lambda i: (0, 0)),
                  pl.BlockSpec((D_FF, D_MODEL), lambda i: (0, 0)),
                  vec, vec],
        out_specs=row, out_shape=jax.ShapeDtypeStruct((m, D_MODEL), F32),
        compiler_params=_cparams(1), name="ffn_ln",
    )(x2d, w_in, w_out, g, b)


def _run_stream(x, past, wts):
    n_seq, t, _ = x.shape
    m = n_seq * t
    blk = ATTN_BLK
    x2d = x.reshape(m, D_MODEL)
    lam_init = 0.8 - 0.6 * math.exp(-0.3 * 1)

    groups = n_seq if t >= blk else 1
    qkvh, fkt, fvt, vt, logf, logf_pad, logft = _qkv_proj(
        x2d, groups, wts["fox_wqkv"], wts["fox_wvt"], FOX_HD ** -0.5 * LOG2E, forget=wts["fox_forget"])
    if past is None:
        kaug, f_start = _fox_keys(logf_pad, qkvh, n_seq, t, blk)
        o = _fox_attention(qkvh, kaug, vt, f_start[:, 0, :FOX_HEADS].reshape(-1), n_seq, t, blk)
    else:
        past_k, past_v, past_lf = past[0], past[1], past[2]
        past_len = past_k.shape[1]
        dblk = min(DECODE_FOX_BLK, past_len)
        lf_all = jnp.concatenate(
            [jnp.transpose(past_lf, (0, 2, 1)),
             jnp.transpose(logft.reshape(FOX_HEADS, n_seq, t), (1, 0, 2)),
             jnp.zeros((n_seq, FOX_HEADS, dblk - t), F32)], axis=2)
        fcum = _cumsum_time(lf_all, dblk)
        cache_t = lambda c: jnp.transpose(c, (0, 2, 3, 1)).reshape(n_seq, HEAD_PAIRS, LANES, past_len)
        new_t = lambda a: jnp.transpose(a[0].reshape(D_MODEL, n_seq, t), (1, 0, 2))
        o = _decode_fox_attention(qkvh, new_t(fkt), new_t(fvt), cache_t(past_k), cache_t(past_v), fcum,
                                  n_seq, t, dblk)
    x2d = _oproj_ln(x2d, o, wts["fox_wout"], wts["ln_g"][0][0], wts["ln_b"][0][0])
    x2d = _ffn_ln(x2d, wts["ffn_win"][0], wts["ffn_wout"][0], wts["ln_g"][0][1], wts["ln_b"][0][1])

    qkvh, dk, dv, vt, kaug = _qkv_proj(x2d, n_seq, wts["diff_wqkv"], wts["diff_wvt"], DIFF_HD ** -0.5 * LOG2E)
    if past is None:
        o = _diff_attention(qkvh, kaug, vt, wts["diff_lambda"], wts["diff_g"], n_seq, t, blk, lam_init)
    else:
        o = _decode_diff_attention(qkvh, past[3], past[4], wts["diff_lambda"], wts["diff_g"], n_seq, t, blk,
                                   lam_init)
    x2d = _oproj_ln(x2d, o, wts["diff_wout"], wts["ln_g"][1][0], wts["ln_b"][1][0])
    x2d = _ffn_ln(x2d, wts["ffn_win"][1], wts["ffn_wout"][1], wts["ln_g"][1][1], wts["ln_b"][1][1])

    def untranspose(a):
        a = a.reshape(groups, FOX_HEADS, FOX_HD, n_seq // groups, t)
        return jnp.transpose(a, (0, 3, 4, 1, 2)).reshape(1, n_seq, t, FOX_HEADS, FOX_HD)

    return (x2d.reshape(n_seq, t, D_MODEL),
            untranspose(fkt), untranspose(fvt),
            logf.reshape(1, n_seq, t, FOX_HEADS),
            dk.reshape(1, n_seq, t, DIFF_HEADS, 2 * DIFF_HD), dv.reshape(1, n_seq, t, DIFF_HEADS, 2 * DIFF_HD))


def _prepare_weights(fox_w_in, fox_b_f, fox_w_out, diff_w_in, diff_lambda, diff_subln_g, diff_w_out,
                     ffn_w_in, ffn_w_out, ln_g, ln_b):
    wf = fox_w_in[0][:, 3 * D_MODEL:].astype(BF16)
    wf_pad = jnp.pad(wf, ((0, 0), (0, LANES - FOX_HEADS)))
    bf_pad = jnp.pad(fox_b_f[0].reshape(1, FOX_HEADS), ((0, 0), (0, LANES - FOX_HEADS)))
    return {
        "fox_wqkv": fox_w_in[0][:, :3 * D_MODEL].astype(BF16),
        "fox_wvt": fox_w_in[0][:, 2 * D_MODEL:3 * D_MODEL].T.astype(BF16),
        "fox_forget": (fox_w_in[0][:, D_MODEL:2 * D_MODEL].T.astype(BF16),
                       wf_pad, wf.T, bf_pad, fox_b_f[0].reshape(FOX_HEADS, 1)),
        "fox_wout": fox_w_out[0].astype(BF16),
        "diff_wqkv": diff_w_in[0].astype(BF16),
        "diff_wvt": diff_w_in[0][:, 2 * D_MODEL:].T.astype(BF16),
        "diff_lambda": diff_lambda[0],
        "diff_g": diff_subln_g[0].reshape(1, 2 * DIFF_HD),
        "diff_wout": diff_w_out[0].astype(BF16),
        "ffn_win": [ffn_w_in[i].astype(BF16) for i in range(DEPTH)],
        "ffn_wout": [ffn_w_out[i].astype(BF16) for i in range(DEPTH)],
        "ln_g": [[ln_g[i, j].reshape(1, D_MODEL) for j in range(2)] for i in range(DEPTH)],
        "ln_b": [[ln_b[i, j].reshape(1, D_MODEL) for j in range(2)] for i in range(DEPTH)],
    }


def kernel(x_prompt, x_sample, cache_fox_k, cache_fox_v, cache_fox_logf, cache_diff_k, cache_diff_v, fox_w_in, fox_b_f, fox_w_out, diff_w_in, diff_lambda, diff_subln_g, diff_w_out, ffn_w_in, ffn_w_out, ln_g, ln_b):
    wts = _prepare_weights(fox_w_in, fox_b_f, fox_w_out, diff_w_in, diff_lambda, diff_subln_g, diff_w_out,
                           ffn_w_in, ffn_w_out, ln_g, ln_b)
    y_p, fk_p, fv_p, lf_p, dk_p, dv_p = _run_stream(x_prompt, None, wts)
    past = (cache_fox_k[0], cache_fox_v[0], cache_fox_logf[0], cache_diff_k[0], cache_diff_v[0])
    y_s, fk_s, fv_s, lf_s, dk_s, dv_s = _run_stream(x_sample, past, wts)
    return (y_p, y_s, fk_p, fv_p, lf_p, dk_p, dv_p, fk_s, fv_s, lf_s, dk_s, dv_s)
```

```python
import functools
import math

import jax
import jax.numpy as jnp
from jax import lax
from jax.experimental import pallas as pl
from jax.experimental.pallas import tpu as pltpu

F32 = jnp.float32
BF16 = jnp.bfloat16

D_MODEL = 1024
DEPTH = 2
FOX_HEADS = 16
FOX_HD = 64
DIFF_HEADS = 8
DIFF_HD = 64
D_FF = 2816
CHUNK = 64
DEEPNORM_ALPHA = (2.0 * DEPTH) ** 0.25
LN_EPS = 1e-5
NEG_INF = -1e30
LOG2E = 1.4426950408889634

LANES = 128
HEAD_PAIRS = D_MODEL // LANES
ATTN_BLK = 512
ONES_ROWS = 16
DECODE_FOX_BLK = 1024
VMEM_LIMIT = 56 * 1024 * 1024


def _cparams(n_axes):
    return pltpu.CompilerParams(dimension_semantics=("arbitrary",) * n_axes,
                                vmem_limit_bytes=VMEM_LIMIT)


def _log_sigmoid(z):
    return jnp.minimum(z, 0.0) - jnp.log1p(jnp.exp(-jnp.abs(z)))


def _layernorm(z, g, b):
    mu = jnp.mean(z, axis=-1, keepdims=True)
    zc = z - mu
    var = jnp.mean(zc * zc, axis=-1, keepdims=True)
    return zc * lax.rsqrt(var + LN_EPS) * g + b


def _dot(a, b):
    return jnp.dot(a, b, preferred_element_type=F32)


def _dot_nt(a, b):
    return lax.dot_general(a, b, (((1,), (1,)), ((), ())), preferred_element_type=F32)


def _vt_rows(group):
    return (LANES // group) * (group + ONES_ROWS)


def _split3(x):
    hi = x.astype(BF16).astype(F32)
    r = x - hi
    mid = r.astype(BF16).astype(F32)
    lo = (r - mid).astype(BF16).astype(F32)
    return hi, mid, lo


def _proj_body(*refs, q_scale, with_forget):
    if with_forget:
        (x_ref, w_ref, wvt_ref, wkt_ref, wf_ref, wft_ref, bf_ref, bft_ref,
         qkv_ref, k_ref, v_ref, vt_ref, lf_ref, lfp_ref, lft_ref) = refs
    else:
        x_ref, w_ref, wvt_ref, qkv_ref, k_ref, v_ref, vt_ref, ka_ref = refs
    j = pl.program_id(1)
    tm = x_ref.shape[0]
    xb = x_ref[...].astype(BF16)

    def store_heads(val):
        vb = val.astype(BF16)
        for p in range(HEAD_PAIRS):
            qkv_ref[0, p] = vb[:, p * LANES:(p + 1) * LANES]
        return vb

    @pl.when(j == 0)
    def _():
        store_heads(_dot(xb, w_ref[...]) * q_scale)
        if with_forget:
            lfp = _log_sigmoid(_dot(xb, wf_ref[...]) + bf_ref[...])
            lfp_ref[...] = lfp
            lf_ref[...] = lfp[:, :FOX_HEADS]
            lft_ref[...] = _log_sigmoid(_dot_nt(wft_ref[...], xb) + bft_ref[...])

    @pl.when(j == 1)
    def _():
        acc = _dot(xb, w_ref[...])
        kb = store_heads(acc)
        if with_forget:
            k_ref[0] = _dot_nt(wkt_ref[...], xb)
        else:
            k_ref[...] = acc
            lane = lax.broadcasted_iota(jnp.int32, (tm, LANES), 1)
            r_local = lax.broadcasted_iota(jnp.int32, (tm, LANES), 0).astype(F32)
            for h in range(DIFF_HEADS):
                hi, mid, lo = _split3(r_local * (2.0 ** -(h + 1) * LOG2E))
                cols = jnp.where(lane == 0, hi, jnp.where(lane == 1, mid, jnp.where(lane == 2, lo, 0.0)))
                ka_ref[h, :, 0:LANES] = kb[:, h * LANES:(h + 1) * LANES]
                ka_ref[h, :, LANES:2 * LANES] = cols.astype(BF16)

    @pl.when(j == 2)
    def _():
        vt32 = _dot_nt(wvt_ref[...], xb)
        if with_forget:
            v_ref[0] = vt32
        else:
            acc = _dot(xb, w_ref[...])
            store_heads(acc)
            v_ref[...] = acc
        vt = vt32.astype(BF16)
        ones = jnp.ones((ONES_ROWS, tm), BF16)
        group = FOX_HD if with_forget else LANES
        for p in range(HEAD_PAIRS):
            for gi in range(LANES // group):
                src = p * LANES + gi * group
                dst = gi * (group + ONES_ROWS)
                vt_ref[p, 0, dst:dst + group, :] = vt[src:src + group, :]
                vt_ref[p, 0, dst + group:dst + group + ONES_ROWS, :] = ones


def _qkv_proj(x2d, n_seq, w_qkv, w_vt, q_scale, forget=None):
    m = x2d.shape[0]
    tm = min(ATTN_BLK, m)
    nblk = m // tm
    group = FOX_HD if forget is not None else LANES
    vt_rows = _vt_rows(group)
    row = pl.BlockSpec((tm, D_MODEL), lambda i, j: (i, 0))
    const = lambda shape: pl.BlockSpec(shape, lambda i, j: (0,) * len(shape))
    if forget is not None:
        n_parts = 2
        w_spec = pl.BlockSpec((D_MODEL, D_MODEL), lambda i, j: (0, jnp.minimum(j, 1)))
        per_seq = nblk // n_seq
        kv_shape = jax.ShapeDtypeStruct((n_seq, D_MODEL, m // n_seq), F32)
        kv_spec = pl.BlockSpec((1, D_MODEL, tm), lambda i, j: (i // per_seq, 0, i % per_seq))
    else:
        n_parts = 3
        w_spec = pl.BlockSpec((D_MODEL, D_MODEL), lambda i, j: (0, j))
        kv_shape = jax.ShapeDtypeStruct((m, D_MODEL), F32)
        kv_spec = row
    in_specs = [row, w_spec, const((D_MODEL, D_MODEL))]
    out_shape = [jax.ShapeDtypeStruct((n_parts, HEAD_PAIRS, m, LANES), BF16),
                 kv_shape, kv_shape,
                 jax.ShapeDtypeStruct((HEAD_PAIRS, nblk, vt_rows, tm), BF16)]
    out_specs = [pl.BlockSpec((1, HEAD_PAIRS, tm, LANES), lambda i, j: (jnp.minimum(j, n_parts - 1), 0, i, 0)),
                 kv_spec, kv_spec,
                 pl.BlockSpec((HEAD_PAIRS, 1, vt_rows, tm), lambda i, j: (0, i, 0, 0))]
    args = [x2d, w_qkv, w_vt]
    if forget is not None:
        w_kt, wf, wft, bf, bft = forget
        in_specs.append(const((D_MODEL, D_MODEL)))
        args.append(w_kt)
        in_specs += [const((D_MODEL, LANES)), const((FOX_HEADS, D_MODEL)), const((1, LANES)), const((FOX_HEADS, 1))]
        out_shape += [jax.ShapeDtypeStruct((m, FOX_HEADS), F32),
                      jax.ShapeDtypeStruct((m, LANES), F32),
                      jax.ShapeDtypeStruct((FOX_HEADS, m), F32)]
        out_specs += [pl.BlockSpec((tm, FOX_HEADS), lambda i, j: (i, 0)),
                      pl.BlockSpec((tm, LANES), lambda i, j: (i, 0)),
                      pl.BlockSpec((FOX_HEADS, tm), lambda i, j: (0, i))]
        args += [wf, wft, bf, bft]
    else:
        out_shape.append(jax.ShapeDtypeStruct((DIFF_HEADS, m, 2 * LANES), BF16))
        out_specs.append(pl.BlockSpec((DIFF_HEADS, tm, 2 * LANES), lambda i, j: (0, i, 0)))
    return pl.pallas_call(
        functools.partial(_proj_body, q_scale=q_scale, with_forget=forget is not None),
        grid=(nblk, 3), in_specs=in_specs, out_specs=out_specs, out_shape=out_shape,
        compiler_params=_cparams(2), name="qkv_proj_f" if forget is not None else "qkv_proj",
    )(*args)


def _fox_keys_body(lf_ref, k_ref, ka_ref, fs_ref, carry_ref, *, blk):
    i = pl.program_id(1)

    @pl.when(i == 0)
    def _():
        carry_ref[...] = jnp.zeros_like(carry_ref)

    row = lax.broadcasted_iota(jnp.int32, (blk, blk), 0)
    col = lax.broadcasted_iota(jnp.int32, (blk, blk), 1)
    tri = (col <= row).astype(BF16)
    hi, mid, lo = _split3(lf_ref[...])
    f_loc = _dot(tri, hi.astype(BF16)) + _dot(tri, mid.astype(BF16)) + _dot(tri, lo.astype(BF16))
    fs_ref[0] = carry_ref[...]
    carry_ref[...] = carry_ref[...] + f_loc[blk - 1:blk, :]

    bhi, bmid, blo = (t.astype(BF16) for t in _split3(f_loc * (-LOG2E)))
    sr = lax.broadcasted_iota(jnp.int32, (LANES, LANES), 0)
    sc = lax.broadcasted_iota(jnp.int32, (LANES, LANES), 1)
    for p in range(HEAD_PAIRS):
        def pick(term):
            return (((sr == 2 * p) & (sc == term)) | ((sr == 2 * p + 1) & (sc == 3 + term))).astype(BF16)
        cols = _dot(bhi, pick(0)) + _dot(bmid, pick(1)) + _dot(blo, pick(2))
        ka_ref[p, :, 0:LANES] = k_ref[0, p]
        ka_ref[p, :, LANES:2 * LANES] = cols.astype(BF16)


def _fox_keys(lf_pad, qkvh, n_seq, t, blk):
    m = n_seq * t
    nblk = t // blk
    return pl.pallas_call(
        functools.partial(_fox_keys_body, blk=blk),
        grid=(n_seq, nblk),
        in_specs=[pl.BlockSpec((blk, LANES), lambda b, i: (b * nblk + i, 0)),
                  pl.BlockSpec((1, HEAD_PAIRS, blk, LANES), lambda b, i: (1, 0, b * nblk + i, 0))],
        out_specs=[pl.BlockSpec((HEAD_PAIRS, blk, 2 * LANES), lambda b, i: (0, b * nblk + i, 0)),
                   pl.BlockSpec((1, 1, LANES), lambda b, i: (b * nblk + i, 0, 0))],
        out_shape=[jax.ShapeDtypeStruct((HEAD_PAIRS, m, 2 * LANES), BF16),
                   jax.ShapeDtypeStruct((n_seq * nblk, 1, LANES), F32)],
        scratch_shapes=[pltpu.VMEM((1, LANES), F32)],
        compiler_params=_cparams(2), name="fox_keys",
    )(lf_pad, qkvh)


def _cumsum_body(x_ref, o_ref, carry_ref, tri_ref, *, blk):
    i = pl.program_id(1)

    @pl.when((pl.program_id(0) == 0) & (i == 0))
    def _():
        row = lax.broadcasted_iota(jnp.int32, (blk, blk), 0)
        col = lax.broadcasted_iota(jnp.int32, (blk, blk), 1)
        tri_ref[...] = (row <= col).astype(BF16)

    @pl.when(i == 0)
    def _():
        carry_ref[...] = jnp.zeros_like(carry_ref)

    hi, mid, lo = _split3(x_ref[0])
    tri = tri_ref[...]
    cs = _dot(hi.astype(BF16), tri) + _dot(mid.astype(BF16), tri) + _dot(lo.astype(BF16), tri) + carry_ref[...]
    o_ref[0] = cs
    carry_ref[...] = cs[:, blk - 1:blk]


def _cumsum_time(lf, blk):
    n_seq, h, r = lf.shape
    nblk = r // blk
    return pl.pallas_call(
        functools.partial(_cumsum_body, blk=blk),
        grid=(n_seq, nblk),
        in_specs=[pl.BlockSpec((1, h, blk), lambda b, i: (b, 0, i))],
        out_specs=pl.BlockSpec((1, h, blk), lambda b, i: (b * nblk + i, 0, 0)),
        out_shape=jax.ShapeDtypeStruct((n_seq * nblk, h, blk), F32),
        scratch_shapes=[pltpu.VMEM((h, 1), F32), pltpu.VMEM((blk, blk), BF16)],
        compiler_params=_cparams(2), name="cumsum_time",
    )(lf)


def _augment_queries(q, blk, ones_first_lane):
    lane = lax.broadcasted_iota(jnp.int32, (blk, LANES), 1)
    lo_mask = lane < LANES // 2
    zero = jnp.zeros_like(q)
    out = []
    for u, part in enumerate((jnp.where(lo_mask, q, zero), jnp.where(lo_mask, zero, q))):
        first = ones_first_lane[u]
        ones = jnp.where((lane >= first) & (lane < first + 3), 1.0, 0.0).astype(BF16)
        out.append(jnp.concatenate([part, ones], axis=1))
    return out


def _softmax_step(s, bmax, c, vt, m_ref, l_ref, u):
    group = vt.shape[0] - ONES_ROWS
    m_old = m_ref[u]
    m_new = jnp.maximum(m_old, bmax + c)
    p = jnp.exp2(s - (m_new - c))
    alpha = jnp.exp2(m_old - m_new)
    pv = _dot(vt, p.astype(BF16))
    l_ref[u] = alpha * l_ref[u] + pv[group:group + 1, :]
    m_ref[u] = m_new
    return alpha, pv[:group, :]


def _sweep_key_tiles(qi, produce, consume):
    @pl.when(qi == 0)
    def _():
        produce(0, 0, True)

    @pl.when(qi > 0)
    def _():
        n_pairs = lax.shift_right_logical(qi - 1, 1)
        odd = (qi - 1) & 1

        @pl.when(odd == 1)
        def _():
            produce(0, 0, False)
            produce(1, 1, False)
            consume(0, 0)

        @pl.when(odd == 0)
        def _():
            produce(0, 1, False)

        def two_tiles(j):
            produce(j + 1, 0, False)
            consume(j, 1)
            produce(j + 2, 1, False)
            consume(j + 1, 0)

        def body2(i, carry):
            two_tiles(odd + 2 * i)
            return carry

        def body4(i, carry):
            j = odd + 2 * (n_pairs & 1) + 4 * i
            two_tiles(j)
            two_tiles(j + 2)
            return carry

        lax.fori_loop(0, n_pairs & 1, body2, 0)
        lax.fori_loop(0, lax.shift_right_logical(n_pairs, 1), body4, 0)
        produce(qi, 0, True)
        consume(qi - 1, 1)

    consume(qi, 0)


def _fox_attn_body(fs_ref, q_ref, ka_ref, vt_ref, o_ref, acc_ref, m_ref, l_ref, s_ref, bm_ref, *, blk, nq):
    seq = pl.program_id(0)
    pair = pl.program_id(1)
    qi = pl.program_id(2)
    qa = _augment_queries(q_ref[0, 0], blk, (0, 3))
    acc_ref[...] = jnp.zeros_like(acc_ref)
    m_ref[...] = jnp.full_like(m_ref, NEG_INF)
    l_ref[...] = jnp.zeros_like(l_ref)
    half = FOX_HD

    def produce(kj, slot, diagonal):
        start = pl.multiple_of(kj * blk, blk)
        ka = ka_ref[0, pl.ds(start, blk), :]
        if diagonal:
            key = lax.broadcasted_iota(jnp.int32, (blk, blk), 0)
            query = lax.broadcasted_iota(jnp.int32, (blk, blk), 1)
            causal = key <= query
        for h in range(2):
            s = _dot_nt(ka, qa[h])
            if diagonal:
                s = jnp.where(causal, s, NEG_INF)
            s_ref[slot, h] = s
            bm_ref[slot, h] = jnp.max(s, axis=0, keepdims=True)

    def consume(kj, slot):
        vt = vt_ref[0, kj]
        for h in range(2):
            head = 2 * pair + h
            f_q = jnp.full((1, blk), fs_ref[(seq * nq + qi) * FOX_HEADS + head], F32)
            f_k = jnp.full((1, blk), fs_ref[(seq * nq + kj) * FOX_HEADS + head], F32)
            c = (f_q - f_k) * LOG2E
            vt_h = vt[h * (half + ONES_ROWS):(h + 1) * (half + ONES_ROWS), :]
            alpha, pv = _softmax_step(s_ref[slot, h], bm_ref[slot, h], c, vt_h, m_ref, l_ref, h)
            rows = slice(h * half, (h + 1) * half)
            acc_ref[rows, :] = alpha * acc_ref[rows, :] + pv

    _sweep_key_tiles(qi, produce, consume)
    sub = lax.broadcasted_iota(jnp.int32, (LANES, blk), 0)
    o_t = acc_ref[...] / jnp.where(sub < half, l_ref[0], l_ref[1])
    o_ref[...] = o_t.T.astype(BF16)


def _fox_attention(qkvh, kaug, vt, f_start, n_seq, t, blk):
    m = n_seq * t
    nq = t // blk
    grid_spec = pltpu.PrefetchScalarGridSpec(
        num_scalar_prefetch=1,
        grid=(n_seq, HEAD_PAIRS, nq),
        in_specs=[pl.BlockSpec((1, 1, blk, LANES), lambda b, p, i, fs: (0, p, b * nq + i, 0)),
                  pl.BlockSpec((1, t, 2 * LANES), lambda b, p, i, fs: (p, b, 0)),
                  pl.BlockSpec((1, nq, _vt_rows(FOX_HD), blk), lambda b, p, i, fs: (p, b, 0, 0))],
        out_specs=pl.BlockSpec((blk, LANES), lambda b, p, i, fs: (b * nq + i, p)),
        scratch_shapes=[pltpu.VMEM((LANES, blk), F32),
                        pltpu.VMEM((2, 1, blk), F32),
                        pltpu.VMEM((2, 1, blk), F32),
                        pltpu.VMEM((2, 2, blk, blk), F32),
                        pltpu.VMEM((2, 2, 1, blk), F32)])
    return pl.pallas_call(
        functools.partial(_fox_attn_body, blk=blk, nq=nq),
        grid_spec=grid_spec,
        out_shape=jax.ShapeDtypeStruct((m, D_MODEL), BF16),
        compiler_params=_cparams(3), name="fox_attention",
    )(f_start, qkvh, kaug, vt)


def _diff_lambda(lam_ref, lam_init):
    lp = lam_ref[...]
    a = jnp.sum(lp[0:1] * lp[1:2], axis=1, keepdims=True)
    b = jnp.sum(lp[2:3] * lp[3:4], axis=1, keepdims=True)
    return jnp.exp(a) - jnp.exp(b) + lam_init


def _subln(o, g, lam_init):
    o = o * lax.rsqrt(jnp.mean(o * o, axis=-1, keepdims=True) + LN_EPS)
    return o * g * (1.0 - lam_init)


def _diff_attn_body(q_ref, ka_ref, vt_ref, lam_ref, g_ref, o_ref, acc_ref, m_ref, l_ref, s_ref, bm_ref, *,
                    blk, lam_init):
    head = pl.program_id(1)
    qi = pl.program_id(2)
    qa = _augment_queries(q_ref[0, 0], blk, (0, 0))
    acc_ref[...] = jnp.zeros_like(acc_ref)
    m_ref[...] = jnp.full_like(m_ref, NEG_INF)
    l_ref[...] = jnp.zeros_like(l_ref)
    slope = lax.bitcast_convert_type(jnp.full((1, blk), 126 - head, jnp.int32) << 23, F32) * LOG2E

    def produce(kj, slot, diagonal):
        start = pl.multiple_of(kj * blk, blk)
        ka = ka_ref[0, pl.ds(start, blk), :]
        if diagonal:
            key = lax.broadcasted_iota(jnp.int32, (blk, blk), 0)
            query = lax.broadcasted_iota(jnp.int32, (blk, blk), 1)
            visible = (key // CHUNK) <= (query // CHUNK)
            ahead = slope * (-2.0 * jnp.maximum(key - query, 0).astype(F32))
        for u in range(2):
            s = _dot_nt(ka, qa[u])
            if diagonal:
                s = jnp.where(visible, s + ahead, NEG_INF)
            s_ref[slot, u] = s
            bm_ref[slot, u] = jnp.max(s, axis=0, keepdims=True)

    def consume(kj, slot):
        vt = vt_ref[0, kj]
        c = slope * jnp.full((1, blk), (kj - qi) * blk, jnp.int32).astype(F32)
        for u in range(2):
            alpha, pv = _softmax_step(s_ref[slot, u], bm_ref[slot, u], c, vt, m_ref, l_ref, u)
            acc_ref[u] = alpha * acc_ref[u] + pv

    _sweep_key_tiles(qi, produce, consume)
    lam = _diff_lambda(lam_ref, lam_init)
    o_t = acc_ref[0] / l_ref[0] - lam * (acc_ref[1] / l_ref[1])
    o_ref[...] = _subln(o_t.T, g_ref[...], lam_init).astype(BF16)


def _diff_attention(qkvh, kaug, vt, lam_params, subln_g, n_seq, t, blk, lam_init):
    m = n_seq * t
    nq = t // blk
    return pl.pallas_call(
        functools.partial(_diff_attn_body, blk=blk, lam_init=lam_init),
        grid=(n_seq, DIFF_HEADS, nq),
        in_specs=[pl.BlockSpec((1, 1, blk, LANES), lambda b, p, i: (0, p, b * nq + i, 0)),
                  pl.BlockSpec((1, t, 2 * LANES), lambda b, p, i: (p, b, 0)),
                  pl.BlockSpec((1, nq, _vt_rows(LANES), blk), lambda b, p, i: (p, b, 0, 0)),
                  pl.BlockSpec((4, DIFF_HD), lambda b, p, i: (0, 0)),
                  pl.BlockSpec((1, LANES), lambda b, p, i: (0, 0))],
        out_specs=pl.BlockSpec((blk, LANES), lambda b, p, i: (b * nq + i, p)),
        out_shape=jax.ShapeDtypeStruct((m, D_MODEL), BF16),
        scratch_shapes=[pltpu.VMEM((2, LANES, blk), F32),
                        pltpu.VMEM((2, 1, blk), F32),
                        pltpu.VMEM((2, 1, blk), F32),
                        pltpu.VMEM((2, 2, blk, blk), F32),
                        pltpu.VMEM((2, 2, 1, blk), F32)],
        compiler_params=_cparams(3), name="diff_attention",
    )(qkvh, kaug, vt, lam_params, subln_g)


def _decode_diff_body(q_ref, kn_ref, vn_ref, kc_ref, vc_ref, lam_ref, g_ref, o_ref, qbd_ref, acc_ref, m_ref,
                      l_ref, *, t_new, past_len, blk, nk, lam_init):
    kj = pl.program_id(1)
    groups = 2 * DIFF_HEADS
    rows = groups * t_new
    row1 = lax.broadcasted_iota(jnp.int32, (rows, 1), 0)

    def gather_pairs(ref):
        return jnp.concatenate([ref[0, p] for p in range(HEAD_PAIRS)], axis=1)

    @pl.when(kj == 0)
    def _():
        q = gather_pairs(q_ref)
        qt = jnp.concatenate([q] * groups, axis=0)
        r = lax.broadcasted_iota(jnp.int32, (rows, D_MODEL), 0)
        c = lax.broadcasted_iota(jnp.int32, (rows, D_MODEL), 1)
        qbd_ref[...] = jnp.where((r // t_new) == (c // DIFF_HD), qt, jnp.zeros_like(qt))
        acc_ref[...] = jnp.zeros_like(acc_ref)
        m_ref[...] = jnp.full_like(m_ref, NEG_INF)
        l_ref[...] = jnp.zeros_like(l_ref)

    slope = jnp.exp2(-(row1 // (2 * t_new) + 1).astype(F32)) * LOG2E

    def update(s, v):
        m_old = m_ref[...]
        m_new = jnp.maximum(m_old, jnp.max(s, axis=1, keepdims=True))
        p = jnp.exp2(s - m_new)
        a = jnp.exp2(m_old - m_new)
        l_ref[...] = a * l_ref[...] + jnp.sum(p, axis=1, keepdims=True)
        m_ref[...] = m_new
        acc_ref[...] = a * acc_ref[...] + _dot(p.astype(BF16), v)

    def load_cache(ref):
        heads = ref.shape[1] // blk
        return jnp.concatenate(
            [ref[0, pl.ds(h, blk, stride=heads), :] for h in range(heads)], axis=1).astype(BF16)

    kc = load_cache(kc_ref)
    vc = load_cache(vc_ref)
    s = _dot_nt(qbd_ref[...], kc)
    col = lax.broadcasted_iota(jnp.int32, (1, blk), 1)
    update(s + slope * (col + (kj * blk - past_len)).astype(F32), vc)

    @pl.when(kj == nk - 1)
    def _():
        kn = gather_pairs(kn_ref)
        vn = gather_pairs(vn_ref)
        s = _dot_nt(qbd_ref[...], kn)
        tq = lax.broadcasted_iota(jnp.int32, (rows, t_new), 0) % t_new
        tk = lax.broadcasted_iota(jnp.int32, (rows, t_new), 1)
        s = s + slope * (tq - jnp.abs(tq - tk)).astype(F32)
        visible = ((tk + past_len) // CHUNK) <= ((tq + past_len) // CHUNK)
        update(jnp.where(visible, s, NEG_INF), vn)

        r = lax.broadcasted_iota(jnp.int32, (rows, D_MODEL), 0)
        c = lax.broadcasted_iota(jnp.int32, (rows, D_MODEL), 1)
        inv_l = 1.0 / l_ref[...]
        own = (r // (2 * t_new)) == (c // LANES)
        lam = _diff_lambda(lam_ref, lam_init)
        coef = jnp.where((row1 // t_new) % 2 == 0, inv_l, -lam * inv_l)
        w = jnp.where(own, acc_ref[...] * coef, 0.0)
        o = w[0:t_new]
        for gi in range(1, groups):
            o = o + w[gi * t_new:(gi + 1) * t_new]
        g = g_ref[...]
        o = jnp.concatenate(
            [_subln(o[:, h * LANES:(h + 1) * LANES], g, lam_init) for h in range(DIFF_HEADS)], axis=1)
        o_ref[0] = o.astype(BF16)


def _decode_diff_attention(qkvh, cache_k, cache_v, lam_params, subln_g, n_seq, t_new, blk, lam_init):
    past_len, heads, hd = cache_k.shape[1:]
    nk = past_len // blk
    rows = 2 * DIFF_HEADS * t_new
    new_spec = lambda s: pl.BlockSpec((1, HEAD_PAIRS, t_new, LANES), lambda b, j: (s, 0, b, 0))
    cache_k = cache_k.reshape(n_seq, past_len * heads, hd)
    cache_v = cache_v.reshape(n_seq, past_len * heads, hd)
    cache_spec = pl.BlockSpec((1, blk * heads, hd), lambda b, j: (b, j, 0))
    out = pl.pallas_call(
        functools.partial(_decode_diff_body, t_new=t_new, past_len=past_len, blk=blk, nk=nk, lam_init=lam_init),
        grid=(n_seq, nk),
        in_specs=[new_spec(0), new_spec(1), new_spec(2), cache_spec, cache_spec,
                  pl.BlockSpec((4, DIFF_HD), lambda b, j: (0, 0)),
                  pl.BlockSpec((1, LANES), lambda b, j: (0, 0))],
        out_specs=pl.BlockSpec((1, t_new, D_MODEL), lambda b, j: (b, 0, 0)),
        out_shape=jax.ShapeDtypeStruct((n_seq, t_new, D_MODEL), BF16),
        scratch_shapes=[pltpu.VMEM((rows, D_MODEL), BF16),
                        pltpu.VMEM((rows, D_MODEL), F32),
                        pltpu.VMEM((rows, 1), F32),
                        pltpu.VMEM((rows, 1), F32)],
        compiler_params=_cparams(2), name="decode_attention_diff",
    )(qkvh, qkvh, qkvh, cache_k, cache_v, lam_params, subln_g)
    return out.reshape(n_seq * t_new, D_MODEL)


def _decode_fox_body(q_ref, knt_ref, vnt_ref, kt_ref, vt_ref, f_ref, o_ref, acc_ref, m_ref, l_ref, *,
                     t_new, blk, nk):
    kj = pl.program_id(1)
    lane = lax.broadcasted_iota(jnp.int32, (t_new, LANES), 1)
    lo_mask = lane < FOX_HD

    @pl.when(kj == 0)
    def _():
        acc_ref[...] = jnp.zeros_like(acc_ref)
        m_ref[...] = jnp.full_like(m_ref, NEG_INF)
        l_ref[...] = jnp.zeros_like(l_ref)

    f_last = f_ref[nk - 1][:, blk - 1:blk]

    def attend(kt_pairs, vt_pairs, bias, visible):
        heads = range(FOX_HEADS)
        scores = []
        for p in range(HEAD_PAIRS):
            q = q_ref[0, p]
            zero = jnp.zeros_like(q)
            kt = kt_pairs(p)
            for qh in (jnp.where(lo_mask, q, zero), jnp.where(lo_mask, zero, q)):
                h = len(scores)
                s = _dot(qh, kt) + bias[h:h + 1, :]
                scores.append(s if visible is None else jnp.where(visible, s, NEG_INF))
        m_old = [m_ref[h] for h in heads]
        m_new = [jnp.maximum(m_old[h], jnp.max(scores[h], axis=1, keepdims=True)) for h in heads]
        probs = [jnp.exp2(scores[h] - m_new[h]) for h in heads]
        alpha = [jnp.exp2(m_old[h] - m_new[h]) for h in heads]
        for h in heads:
            l_ref[h] = alpha[h] * l_ref[h] + jnp.sum(probs[h], axis=1, keepdims=True)
            m_ref[h] = m_new[h]
        for p in range(HEAD_PAIRS):
            vt = vt_pairs(p)
            pv = [_dot_nt(probs[2 * p + e].astype(BF16), vt) for e in range(2)]
            acc = acc_ref[p]
            acc_ref[p] = jnp.where(lo_mask, alpha[2 * p] * acc + pv[0], alpha[2 * p + 1] * acc + pv[1])

    attend(lambda p: kt_ref[0, p].astype(BF16), lambda p: vt_ref[0, p].astype(BF16),
           (f_last - f_ref[kj]) * LOG2E, None)

    @pl.when(kj == nk - 1)
    def _():
        tq = lax.broadcasted_iota(jnp.int32, (t_new, t_new), 0)
        tk = lax.broadcasted_iota(jnp.int32, (t_new, t_new), 1)
        attend(lambda p: knt_ref[0, p * LANES:(p + 1) * LANES, :].astype(BF16),
               lambda p: vnt_ref[0, p * LANES:(p + 1) * LANES, :].astype(BF16),
               (f_last - f_ref[nk][:, :t_new]) * LOG2E, tk <= tq)
        o_ref[0] = jnp.concatenate(
            [acc_ref[p] / jnp.where(lo_mask, l_ref[2 * p], l_ref[2 * p + 1]) for p in range(HEAD_PAIRS)],
            axis=1).astype(BF16)


def _decode_fox_attention(qkvh, knt, vnt, cache_kt, cache_vt, fcum, n_seq, t_new, blk):
    past_len = cache_kt.shape[3]
    nk = past_len // blk
    cache_spec = pl.BlockSpec((1, HEAD_PAIRS, LANES, blk), lambda b, j: (b, 0, 0, j))
    new_spec = pl.BlockSpec((1, D_MODEL, t_new), lambda b, j: (b, 0, 0))
    out = pl.pallas_call(
        functools.partial(_decode_fox_body, t_new=t_new, blk=blk, nk=nk),
        grid=(n_seq, nk),
        in_specs=[pl.BlockSpec((1, HEAD_PAIRS, t_new, LANES), lambda b, j: (0, 0, b, 0)),
                  new_spec, new_spec, cache_spec, cache_spec,
                  pl.BlockSpec((nk + 1, FOX_HEADS, blk), lambda b, j: (b, 0, 0))],
        out_specs=pl.BlockSpec((1, t_new, D_MODEL), lambda b, j: (b, 0, 0)),
        out_shape=jax.ShapeDtypeStruct((n_seq, t_new, D_MODEL), BF16),
        scratch_shapes=[pltpu.VMEM((HEAD_PAIRS, t_new, LANES), F32),
                        pltpu.VMEM((FOX_HEADS, t_new, 1), F32),
                        pltpu.VMEM((FOX_HEADS, t_new, 1), F32)],
        compiler_params=_cparams(2), name="decode_attention_fox",
    )(qkvh, knt, vnt, cache_kt, cache_vt, fcum)
    return out.reshape(n_seq * t_new, D_MODEL)


def _oproj_ln_body(x_ref, o_ref, w_ref, g_ref, b_ref, y_ref):
    y = _dot(o_ref[...], w_ref[...])
    y_ref[...] = _layernorm(DEEPNORM_ALPHA * x_ref[...] + y, g_ref[...], b_ref[...])


def _oproj_ln(x2d, o2d, w, g, b, tm=512):
    m = x2d.shape[0]
    tm = min(tm, m)
    row = pl.BlockSpec((tm, D_MODEL), lambda i: (i, 0))
    vec = pl.BlockSpec((1, D_MODEL), lambda i: (0, 0))
    return pl.pallas_call(
        _oproj_ln_body, grid=(m // tm,),
        in_specs=[row, row, pl.BlockSpec((D_MODEL, D_MODEL), lambda i: (0, 0)), vec, vec],
        out_specs=row, out_shape=jax.ShapeDtypeStruct((m, D_MODEL), F32),
        compiler_params=_cparams(1), name="oproj_ln",
    )(x2d, o2d, w, g, b)


def _ffn_ln_body(x_ref, win_ref, wout_ref, g_ref, b_ref, y_ref):
    x = x_ref[...]
    h = _dot(x.astype(BF16), win_ref[...])
    gate = h[:, :D_FF]
    up = h[:, D_FF:]
    act = (gate * jax.nn.sigmoid(gate) * up).astype(BF16)
    y = _dot(act, wout_ref[...])
    y_ref[...] = _layernorm(DEEPNORM_ALPHA * x + y, g_ref[...], b_ref[...])


def _ffn_ln(x2d, w_in, w_out, g, b, tm=256):
    m = x2d.shape[0]
    tm = min(tm, m)
    row = pl.BlockSpec((tm, D_MODEL), lambda i: (i, 0))
    vec = pl.BlockSpec((1, D_MODEL), lambda i: (0, 0))
    return pl.pallas_call(
        _ffn_ln_body, grid=(m // tm,),
        in_specs=[row,
                  pl.BlockSpec((D_MODEL, 2 * D_FF), lambda i: (0, 0)),
                  pl.BlockSpec((D_FF, D_MODEL), lambda i: (0, 0)),
                  vec, vec],
        out_specs=row, out_shape=jax.ShapeDtypeStruct((m, D_MODEL), F32),
        compiler_params=_cparams(1), name="ffn_ln",
    )(x2d, w_in, w_out, g, b)


def _run_stream(x, past, wts):
    n_seq, t, _ = x.shape
    m = n_seq * t
    blk = ATTN_BLK
    x2d = x.reshape(m, D_MODEL)
    lam_init = 0.8 - 0.6 * math.exp(-0.3 * 1)

    groups = n_seq if t >= blk else 1
    qkvh, fkt, fvt, vt, logf, logf_pad, logft = _qkv_proj(
        x2d, groups, wts["fox_wqkv"], wts["fox_wvt"], FOX_HD ** -0.5 * LOG2E, forget=wts["fox_forget"])
    if past is None:
        kaug, f_start = _fox_keys(logf_pad, qkvh, n_seq, t, blk)
        o = _fox_attention(qkvh, kaug, vt, f_start[:, 0, :FOX_HEADS].reshape(-1), n_seq, t, blk)
    else:
        past_k, past_v, past_lf = past[0], past[1], past[2]
        past_len = past_k.shape[1]
        dblk = min(DECODE_FOX_BLK, past_len)
        lf_all = jnp.concatenate(
            [jnp.transpose(past_lf, (0, 2, 1)),
             jnp.transpose(logft.reshape(FOX_HEADS, n_seq, t), (1, 0, 2)),
             jnp.zeros((n_seq, FOX_HEADS, dblk - t), F32)], axis=2)
        fcum = _cumsum_time(lf_all, dblk)
        cache_t = lambda c: jnp.transpose(c, (0, 2, 3, 1)).reshape(n_seq, HEAD_PAIRS, LANES, past_len)
        new_t = lambda a: jnp.transpose(a[0].reshape(D_MODEL, n_seq, t), (1, 0, 2))
        o = _decode_fox_attention(qkvh, new_t(fkt), new_t(fvt), cache_t(past_k), cache_t(past_v), fcum,
                                  n_seq, t, dblk)
    x2d = _oproj_ln(x2d, o, wts["fox_wout"], wts["ln_g"][0][0], wts["ln_b"][0][0])
    x2d = _ffn_ln(x2d, wts["ffn_win"][0], wts["ffn_wout"][0], wts["ln_g"][0][1], wts["ln_b"][0][1])

    qkvh, dk, dv, vt, kaug = _qkv_proj(x2d, n_seq, wts["diff_wqkv"], wts["diff_wvt"], DIFF_HD ** -0.5 * LOG2E)
    if past is None:
        o = _diff_attention(qkvh, kaug, vt, wts["diff_lambda"], wts["diff_g"], n_seq, t, blk, lam_init)
    else:
        o = _decode_diff_attention(qkvh, past[3], past[4], wts["diff_lambda"], wts["diff_g"], n_seq, t, blk,
                                   lam_init)
    x2d = _oproj_ln(x2d, o, wts["diff_wout"], wts["ln_g"][1][0], wts["ln_b"][1][0])
    x2d = _ffn_ln(x2d, wts["ffn_win"][1], wts["ffn_wout"][1], wts["ln_g"][1][1], wts["ln_b"][1][1])

    def untranspose(a):
        a = a.reshape(groups, FOX_HEADS, FOX_HD, n_seq // groups, t)
        return jnp.transpose(a, (0, 3, 4, 1, 2)).reshape(1, n_seq, t, FOX_HEADS, FOX_HD)

    return (x2d.reshape(n_seq, t, D_MODEL),
            untranspose(fkt), untranspose(fvt),
            logf.reshape(1, n_seq, t, FOX_HEADS),
            dk.reshape(1, n_seq, t, DIFF_HEADS, 2 * DIFF_HD), dv.reshape(1, n_seq, t, DIFF_HEADS, 2 * DIFF_HD))


def _prepare_weights(fox_w_in, fox_b_f, fox_w_out, diff_w_in, diff_lambda, diff_subln_g, diff_w_out,
                     ffn_w_in, ffn_w_out, ln_g, ln_b):
    wf = fox_w_in[0][:, 3 * D_MODEL:].astype(BF16)
    wf_pad = jnp.pad(wf, ((0, 0), (0, LANES - FOX_HEADS)))
    bf_pad = jnp.pad(fox_b_f[0].reshape(1, FOX_HEADS), ((0, 0), (0, LANES - FOX_HEADS)))
    return {
        "fox_wqkv": fox_w_in[0][:, :3 * D_MODEL].astype(BF16),
        "fox_wvt": fox_w_in[0][:, 2 * D_MODEL:3 * D_MODEL].T.astype(BF16),
        "fox_forget": (fox_w_in[0][:, D_MODEL:2 * D_MODEL].T.astype(BF16),
                       wf_pad, wf.T, bf_pad, fox_b_f[0].reshape(FOX_HEADS, 1)),
        "fox_wout": fox_w_out[0].astype(BF16),
        "diff_wqkv": diff_w_in[0].astype(BF16),
        "diff_wvt": diff_w_in[0][:, 2 * D_MODEL:].T.astype(BF16),
        "diff_lambda": diff_lambda[0],
        "diff_g": diff_subln_g[0].reshape(1, 2 * DIFF_HD),
        "diff_wout": diff_w_out[0].astype(BF16),
        "ffn_win": [ffn_w_in[i].astype(BF16) for i in range(DEPTH)],
        "ffn_wout": [ffn_w_out[i].astype(BF16) for i in range(DEPTH)],
        "ln_g": [[ln_g[i, j].reshape(1, D_MODEL) for j in range(2)] for i in range(DEPTH)],
        "ln_b": [[ln_b[i, j].reshape(1, D_MODEL) for j in range(2)] for i in range(DEPTH)],
    }


def kernel(x_prompt, x_sample, cache_fox_k, cache_fox_v, cache_fox_logf, cache_diff_k, cache_diff_v, fox_w_in, fox_b_f, fox_w_out, diff_w_in, diff_lambda, diff_subln_g, diff_w_out, ffn_w_in, ffn_w_out, ln_g, ln_b):
    wts = _prepare_weights(fox_w_in, fox_b_f, fox_w_out, diff_w_in, diff_lambda, diff_subln_g, diff_w_out,
                           ffn_w_in, ffn_w_out, ln_g, ln_b)
    y_p, fk_p, fv_p, lf_p, dk_p, dv_p = _run_stream(x_prompt, None, wts)
    past = (cache_fox_k[0], cache_fox_v[0], cache_fox_logf[0], cache_diff_k[0], cache_diff_v[0])
    y_s, fk_s, fv_s, lf_s, dk_s, dv_s = _run_stream(x_sample, past, wts)
    return (y_p, y_s, fk_p, fv_p, lf_p, dk_p, dv_p, fk_s, fv_s, lf_s, dk_s, dv_s)
```

```python
import functools
import math

import jax
import jax.numpy as jnp
import numpy as np
from jax import lax
from jax.experimental import pallas as pl
from jax.experimental.pallas import tpu as pltpu

F32 = jnp.float32
BF16 = jnp.bfloat16

D_MODEL = 1024
DEPTH = 2
FOX_HEADS = 16
FOX_HD = 64
DIFF_HEADS = 8
DIFF_HD = 64
D_FF = 2816
CHUNK = 64
DEEPNORM_ALPHA = (2.0 * DEPTH) ** 0.25
LN_EPS = 1e-5
NEG_INF = -1e30
LOG2E = 1.4426950408889634

LANES = 128
HEAD_PAIRS = D_MODEL // LANES
ATTN_BLK = 512
ONES_ROWS = 16
DECODE_BLK = 1024
VMEM_LIMIT = 56 * 1024 * 1024


def _cparams(n_axes):
    return pltpu.CompilerParams(dimension_semantics=("arbitrary",) * n_axes,
                                vmem_limit_bytes=VMEM_LIMIT)


def _log_sigmoid(z):
    return jnp.minimum(z, 0.0) - jnp.log1p(jnp.exp(-jnp.abs(z)))


def _layernorm(z, g, b):
    mu = jnp.mean(z, axis=-1, keepdims=True)
    zc = z - mu
    var = jnp.mean(zc * zc, axis=-1, keepdims=True)
    return zc * lax.rsqrt(var + LN_EPS) * g + b


def _dot(a, b):
    return jnp.dot(a, b, preferred_element_type=F32)


def _dot_nt(a, b):
    return lax.dot_general(a, b, (((1,), (1,)), ((), ())), preferred_element_type=F32)


def _vt_rows(group):
    return (LANES // group) * (group + ONES_ROWS)


def _split3(x):
    hi = x.astype(BF16).astype(F32)
    r = x - hi
    mid = r.astype(BF16).astype(F32)
    lo = (r - mid).astype(BF16).astype(F32)
    return hi, mid, lo


def _alibi_columns(rows):
    slope = np.asarray([2.0 ** -(h + 1) * LOG2E for h in range(DIFF_HEADS)], np.float32)
    x = np.arange(rows, dtype=np.float32)[None, :] * slope[:, None]
    cols = np.zeros((DIFF_HEADS, rows, LANES), np.float32)
    for term in range(3):
        part = x.astype(BF16).astype(np.float32)
        cols[:, :, term] = part
        x = x - part
    return jnp.asarray(cols.astype(BF16))


def _proj_body(*refs, q_scale, with_forget):
    if with_forget:
        (x_ref, w_ref, wvt_ref, wkt_ref, wf_ref, wft_ref, bf_ref, bft_ref,
         qkv_ref, k_ref, v_ref, vt_ref, lf_ref, lfp_ref, lft_ref) = refs
    else:
        x_ref, w_ref, wvt_ref, pos_ref, qkv_ref, k_ref, v_ref, vt_ref, ka_ref = refs
    tm = x_ref.shape[0]
    xb = x_ref[...].astype(BF16)

    def project(part):
        return _dot(xb, w_ref[:, part * D_MODEL:(part + 1) * D_MODEL])

    def store_heads(part, val):
        vb = val.astype(BF16)
        for p in range(HEAD_PAIRS):
            qkv_ref[part, p] = vb[:, p * LANES:(p + 1) * LANES]
        return vb

    store_heads(0, project(0) * q_scale)
    k = project(1)
    kb = store_heads(1, k)
    vt32 = _dot_nt(wvt_ref[...], xb)
    if with_forget:
        lfp = _log_sigmoid(_dot(xb, wf_ref[...]) + bf_ref[...])
        lfp_ref[...] = lfp
        lf_ref[...] = lfp[:, :FOX_HEADS]
        lft_ref[...] = _log_sigmoid(_dot_nt(wft_ref[...], xb) + bft_ref[...])
        k_ref[0] = _dot_nt(wkt_ref[...], xb)
        v_ref[0] = vt32
    else:
        k_ref[...] = k
        for h in range(DIFF_HEADS):
            ka_ref[h, :, 0:LANES] = kb[:, h * LANES:(h + 1) * LANES]
            ka_ref[h, :, LANES:2 * LANES] = pos_ref[h]
        v = project(2)
        store_heads(2, v)
        v_ref[...] = v
    vt = vt32.astype(BF16)
    ones = jnp.ones((ONES_ROWS, tm), BF16)
    group = FOX_HD if with_forget else LANES
    for p in range(HEAD_PAIRS):
        for gi in range(LANES // group):
            src = p * LANES + gi * group
            dst = gi * (group + ONES_ROWS)
            vt_ref[p, 0, dst:dst + group, :] = vt[src:src + group, :]
            vt_ref[p, 0, dst + group:dst + group + ONES_ROWS, :] = ones


def _qkv_proj(x2d, n_seq, w_qkv, w_vt, q_scale, forget=None):
    m = x2d.shape[0]
    tm = min(ATTN_BLK, m)
    nblk = m // tm
    group = FOX_HD if forget is not None else LANES
    vt_rows = _vt_rows(group)
    row = pl.BlockSpec((tm, D_MODEL), lambda i: (i, 0))
    const = lambda shape: pl.BlockSpec(shape, lambda i: (0,) * len(shape))
    if forget is not None:
        n_parts = 2
        per_seq = nblk // n_seq
        kv_shape = jax.ShapeDtypeStruct((n_seq, D_MODEL, m // n_seq), F32)
        kv_spec = pl.BlockSpec((1, D_MODEL, tm), lambda i: (i // per_seq, 0, i % per_seq))
    else:
        n_parts = 3
        kv_shape = jax.ShapeDtypeStruct((m, D_MODEL), F32)
        kv_spec = row
    in_specs = [row, const((D_MODEL, n_parts * D_MODEL)), const((D_MODEL, D_MODEL))]
    out_shape = [jax.ShapeDtypeStruct((n_parts, HEAD_PAIRS, m, LANES), BF16),
                 kv_shape, kv_shape,
                 jax.ShapeDtypeStruct((HEAD_PAIRS, nblk, vt_rows, tm), BF16)]
    out_specs = [pl.BlockSpec((n_parts, HEAD_PAIRS, tm, LANES), lambda i: (0, 0, i, 0)),
                 kv_spec, kv_spec,
                 pl.BlockSpec((HEAD_PAIRS, 1, vt_rows, tm), lambda i: (0, i, 0, 0))]
    args = [x2d, w_qkv[:, :n_parts * D_MODEL], w_vt]
    if forget is not None:
        w_kt, wf, wft, bf, bft = forget
        in_specs.append(const((D_MODEL, D_MODEL)))
        args.append(w_kt)
        in_specs += [const((D_MODEL, LANES)), const((FOX_HEADS, D_MODEL)), const((1, LANES)), const((FOX_HEADS, 1))]
        out_shape += [jax.ShapeDtypeStruct((m, FOX_HEADS), F32),
                      jax.ShapeDtypeStruct((m, LANES), F32),
                      jax.ShapeDtypeStruct((FOX_HEADS, m), F32)]
        out_specs += [pl.BlockSpec((tm, FOX_HEADS), lambda i: (i, 0)),
                      pl.BlockSpec((tm, LANES), lambda i: (i, 0)),
                      pl.BlockSpec((FOX_HEADS, tm), lambda i: (0, i))]
        args += [wf, wft, bf, bft]
    else:
        in_specs.append(const((DIFF_HEADS, tm, LANES)))
        args.append(_alibi_columns(tm))
        out_shape.append(jax.ShapeDtypeStruct((DIFF_HEADS, m, 2 * LANES), BF16))
        out_specs.append(pl.BlockSpec((DIFF_HEADS, tm, 2 * LANES), lambda i: (0, i, 0)))
    return pl.pallas_call(
        functools.partial(_proj_body, q_scale=q_scale, with_forget=forget is not None),
        grid=(nblk,), in_specs=in_specs, out_specs=out_specs, out_shape=out_shape,
        compiler_params=_cparams(1), name="qkv_proj_f" if forget is not None else "qkv_proj",
    )(*args)


def _fox_keys_body(lf_ref, k_ref, ka_ref, fs_ref, carry_ref, *, blk):
    i = pl.program_id(1)

    @pl.when(i == 0)
    def _():
        carry_ref[...] = jnp.zeros_like(carry_ref)

    row = lax.broadcasted_iota(jnp.int32, (blk, blk), 0)
    col = lax.broadcasted_iota(jnp.int32, (blk, blk), 1)
    tri = (col <= row).astype(BF16)
    hi, mid, lo = _split3(lf_ref[...])
    f_loc = _dot(tri, hi.astype(BF16)) + _dot(tri, mid.astype(BF16)) + _dot(tri, lo.astype(BF16))
    fs_ref[0] = carry_ref[...]
    carry_ref[...] = carry_ref[...] + f_loc[blk - 1:blk, :]

    bhi, bmid, blo = (t.astype(BF16) for t in _split3(f_loc * (-LOG2E)))
    sr = lax.broadcasted_iota(jnp.int32, (LANES, LANES), 0)
    sc = lax.broadcasted_iota(jnp.int32, (LANES, LANES), 1)
    for p in range(HEAD_PAIRS):
        def pick(term):
            return (((sr == 2 * p) & (sc == term)) | ((sr == 2 * p + 1) & (sc == 3 + term))).astype(BF16)
        cols = _dot(bhi, pick(0)) + _dot(bmid, pick(1)) + _dot(blo, pick(2))
        ka_ref[p, :, 0:LANES] = k_ref[0, p]
        ka_ref[p, :, LANES:2 * LANES] = cols.astype(BF16)


def _fox_keys(lf_pad, qkvh, n_seq, t, blk):
    m = n_seq * t
    nblk = t // blk
    return pl.pallas_call(
        functools.partial(_fox_keys_body, blk=blk),
        grid=(n_seq, nblk),
        in_specs=[pl.BlockSpec((blk, LANES), lambda b, i: (b * nblk + i, 0)),
                  pl.BlockSpec((1, HEAD_PAIRS, blk, LANES), lambda b, i: (1, 0, b * nblk + i, 0))],
        out_specs=[pl.BlockSpec((HEAD_PAIRS, blk, 2 * LANES), lambda b, i: (0, b * nblk + i, 0)),
                   pl.BlockSpec((1, 1, LANES), lambda b, i: (b * nblk + i, 0, 0))],
        out_shape=[jax.ShapeDtypeStruct((HEAD_PAIRS, m, 2 * LANES), BF16),
                   jax.ShapeDtypeStruct((n_seq * nblk, 1, LANES), F32)],
        scratch_shapes=[pltpu.VMEM((1, LANES), F32)],
        compiler_params=_cparams(2), name="fox_keys",
    )(lf_pad, qkvh)


def _cumsum_body(x_ref, o_ref, carry_ref, tri_ref, *, blk):
    i = pl.program_id(1)

    @pl.when((pl.program_id(0) == 0) & (i == 0))
    def _():
        row = lax.broadcasted_iota(jnp.int32, (blk, blk), 0)
        col = lax.broadcasted_iota(jnp.int32, (blk, blk), 1)
        tri_ref[...] = (row <= col).astype(BF16)

    @pl.when(i == 0)
    def _():
        carry_ref[...] = jnp.zeros_like(carry_ref)

    hi, mid, lo = _split3(x_ref[0])
    tri = tri_ref[...]
    cs = _dot(hi.astype(BF16), tri) + _dot(mid.astype(BF16), tri) + _dot(lo.astype(BF16), tri) + carry_ref[...]
    o_ref[0] = cs
    carry_ref[...] = cs[:, blk - 1:blk]


def _cumsum_time(lf, blk):
    n_seq, h, r = lf.shape
    nblk = r // blk
    return pl.pallas_call(
        functools.partial(_cumsum_body, blk=blk),
        grid=(n_seq, nblk),
        in_specs=[pl.BlockSpec((1, h, blk), lambda b, i: (b, 0, i))],
        out_specs=pl.BlockSpec((1, h, blk), lambda b, i: (b * nblk + i, 0, 0)),
        out_shape=jax.ShapeDtypeStruct((n_seq * nblk, h, blk), F32),
        scratch_shapes=[pltpu.VMEM((h, 1), F32), pltpu.VMEM((blk, blk), BF16)],
        compiler_params=_cparams(2), name="cumsum_time",
    )(lf)


def _augment_queries(q, blk, ones_first_lane):
    lane = lax.broadcasted_iota(jnp.int32, (blk, LANES), 1)
    lo_mask = lane < LANES // 2
    zero = jnp.zeros_like(q)
    out = []
    for u, part in enumerate((jnp.where(lo_mask, q, zero), jnp.where(lo_mask, zero, q))):
        first = ones_first_lane[u]
        ones = jnp.where((lane >= first) & (lane < first + 3), 1.0, 0.0).astype(BF16)
        out.append(jnp.concatenate([part, ones], axis=1))
    return out


def _softmax_step(s, bmax, c, vt, m_ref, l_ref, u):
    group = vt.shape[0] - ONES_ROWS
    m_old = m_ref[u]
    m_new = jnp.maximum(m_old, bmax + c)
    p = jnp.exp2(s - (m_new - c))
    alpha = jnp.exp2(m_old - m_new)
    pv = _dot(vt, p.astype(BF16))
    l_ref[u] = alpha * l_ref[u] + pv[group:group + 1, :]
    m_ref[u] = m_new
    return alpha, pv[:group, :]


def _sweep_key_tiles(qi, produce, consume):
    @pl.when(qi == 0)
    def _():
        produce(0, 0, True)

    @pl.when(qi > 0)
    def _():
        n_pairs = lax.shift_right_logical(qi - 1, 1)
        odd = (qi - 1) & 1

        @pl.when(odd == 1)
        def _():
            produce(0, 0, False)
            produce(1, 1, False)
            consume(0, 0)

        @pl.when(odd == 0)
        def _():
            produce(0, 1, False)

        def two_tiles(j):
            produce(j + 1, 0, False)
            consume(j, 1)
            produce(j + 2, 1, False)
            consume(j + 1, 0)

        def body2(i, carry):
            two_tiles(odd + 2 * i)
            return carry

        def body4(i, carry):
            j = odd + 2 * (n_pairs & 1) + 4 * i
            two_tiles(j)
            two_tiles(j + 2)
            return carry

        lax.fori_loop(0, n_pairs & 1, body2, 0)
        lax.fori_loop(0, lax.shift_right_logical(n_pairs, 1), body4, 0)
        produce(qi, 0, True)
        consume(qi - 1, 1)

    consume(qi, 0)


def _fox_attn_body(fs_ref, q_ref, ka_ref, vt_ref, o_ref, acc_ref, m_ref, l_ref, s_ref, bm_ref, *, blk, nq):
    seq = pl.program_id(0)
    pair = pl.program_id(1)
    qi = pl.program_id(2)
    qa = _augment_queries(q_ref[0, 0], blk, (0, 3))
    acc_ref[...] = jnp.zeros_like(acc_ref)
    m_ref[...] = jnp.full_like(m_ref, NEG_INF)
    l_ref[...] = jnp.zeros_like(l_ref)
    half = FOX_HD

    def produce(kj, slot, diagonal):
        start = pl.multiple_of(kj * blk, blk)
        ka = ka_ref[0, pl.ds(start, blk), :]
        if diagonal:
            key = lax.broadcasted_iota(jnp.int32, (blk, blk), 0)
            query = lax.broadcasted_iota(jnp.int32, (blk, blk), 1)
            causal = key <= query
        for h in range(2):
            s = _dot_nt(ka, qa[h])
            if diagonal:
                s = jnp.where(causal, s, NEG_INF)
            s_ref[slot, h] = s
            bm_ref[slot, h] = jnp.max(s, axis=0, keepdims=True)

    def consume(kj, slot):
        vt = vt_ref[0, kj]
        for h in range(2):
            head = 2 * pair + h
            f_q = jnp.full((1, blk), fs_ref[(seq * nq + qi) * FOX_HEADS + head], F32)
            f_k = jnp.full((1, blk), fs_ref[(seq * nq + kj) * FOX_HEADS + head], F32)
            c = (f_q - f_k) * LOG2E
            vt_h = vt[h * (half + ONES_ROWS):(h + 1) * (half + ONES_ROWS), :]
            alpha, pv = _softmax_step(s_ref[slot, h], bm_ref[slot, h], c, vt_h, m_ref, l_ref, h)
            rows = slice(h * half, (h + 1) * half)
            acc_ref[rows, :] = alpha * acc_ref[rows, :] + pv

    _sweep_key_tiles(qi, produce, consume)
    sub = lax.broadcasted_iota(jnp.int32, (LANES, blk), 0)
    o_t = acc_ref[...] / jnp.where(sub < half, l_ref[0], l_ref[1])
    o_ref[...] = o_t.T.astype(BF16)


def _fox_attention(qkvh, kaug, vt, f_start, n_seq, t, blk):
    m = n_seq * t
    nq = t // blk
    grid_spec = pltpu.PrefetchScalarGridSpec(
        num_scalar_prefetch=1,
        grid=(n_seq, HEAD_PAIRS, nq),
        in_specs=[pl.BlockSpec((1, 1, blk, LANES), lambda b, p, i, fs: (0, p, b * nq + i, 0)),
                  pl.BlockSpec((1, t, 2 * LANES), lambda b, p, i, fs: (p, b, 0)),
                  pl.BlockSpec((1, nq, _vt_rows(FOX_HD), blk), lambda b, p, i, fs: (p, b, 0, 0))],
        out_specs=pl.BlockSpec((blk, LANES), lambda b, p, i, fs: (b * nq + i, p)),
        scratch_shapes=[pltpu.VMEM((LANES, blk), F32),
                        pltpu.VMEM((2, 1, blk), F32),
                        pltpu.VMEM((2, 1, blk), F32),
                        pltpu.VMEM((2, 2, blk, blk), F32),
                        pltpu.VMEM((2, 2, 1, blk), F32)])
    return pl.pallas_call(
        functools.partial(_fox_attn_body, blk=blk, nq=nq),
        grid_spec=grid_spec,
        out_shape=jax.ShapeDtypeStruct((m, D_MODEL), BF16),
        compiler_params=_cparams(3), name="fox_attention",
    )(f_start, qkvh, kaug, vt)


def _diff_lambda(lam_ref, lam_init):
    lp = lam_ref[...]
    a = jnp.sum(lp[0:1] * lp[1:2], axis=1, keepdims=True)
    b = jnp.sum(lp[2:3] * lp[3:4], axis=1, keepdims=True)
    return jnp.exp(a) - jnp.exp(b) + lam_init


def _subln(o, g, lam_init):
    o = o * lax.rsqrt(jnp.mean(o * o, axis=-1, keepdims=True) + LN_EPS)
    return o * g * (1.0 - lam_init)


def _diff_attn_body(q_ref, ka_ref, vt_ref, lam_ref, g_ref, o_ref, acc_ref, m_ref, l_ref, s_ref, bm_ref, *,
                    blk, lam_init):
    head = pl.program_id(1)
    qi = pl.program_id(2)
    qa = _augment_queries(q_ref[0, 0], blk, (0, 0))
    acc_ref[...] = jnp.zeros_like(acc_ref)
    m_ref[...] = jnp.full_like(m_ref, NEG_INF)
    l_ref[...] = jnp.zeros_like(l_ref)
    slope = lax.bitcast_convert_type(jnp.full((1, blk), 126 - head, jnp.int32) << 23, F32) * LOG2E

    def produce(kj, slot, diagonal):
        start = pl.multiple_of(kj * blk, blk)
        ka = ka_ref[0, pl.ds(start, blk), :]
        if diagonal:
            key = lax.broadcasted_iota(jnp.int32, (blk, blk), 0)
            query = lax.broadcasted_iota(jnp.int32, (blk, blk), 1)
            visible = (key // CHUNK) <= (query // CHUNK)
            ahead = slope * (-2.0 * jnp.maximum(key - query, 0).astype(F32))
        for u in range(2):
            s = _dot_nt(ka, qa[u])
            if diagonal:
                s = jnp.where(visible, s + ahead, NEG_INF)
            s_ref[slot, u] = s
            bm_ref[slot, u] = jnp.max(s, axis=0, keepdims=True)

    def consume(kj, slot):
        vt = vt_ref[0, kj]
        c = slope * jnp.full((1, blk), (kj - qi) * blk, jnp.int32).astype(F32)
        for u in range(2):
            alpha, pv = _softmax_step(s_ref[slot, u], bm_ref[slot, u], c, vt, m_ref, l_ref, u)
            acc_ref[u] = alpha * acc_ref[u] + pv

    _sweep_key_tiles(qi, produce, consume)
    lam = _diff_lambda(lam_ref, lam_init)
    o_t = acc_ref[0] / l_ref[0] - lam * (acc_ref[1] / l_ref[1])
    o_ref[...] = _subln(o_t.T, g_ref[...], lam_init).astype(BF16)


def _diff_attention(qkvh, kaug, vt, lam_params, subln_g, n_seq, t, blk, lam_init):
    m = n_seq * t
    nq = t // blk
    return pl.pallas_call(
        functools.partial(_diff_attn_body, blk=blk, lam_init=lam_init),
        grid=(n_seq, DIFF_HEADS, nq),
        in_specs=[pl.BlockSpec((1, 1, blk, LANES), lambda b, p, i: (0, p, b * nq + i, 0)),
                  pl.BlockSpec((1, t, 2 * LANES), lambda b, p, i: (p, b, 0)),
                  pl.BlockSpec((1, nq, _vt_rows(LANES), blk), lambda b, p, i: (p, b, 0, 0)),
                  pl.BlockSpec((4, DIFF_HD), lambda b, p, i: (0, 0)),
                  pl.BlockSpec((1, LANES), lambda b, p, i: (0, 0))],
        out_specs=pl.BlockSpec((blk, LANES), lambda b, p, i: (b * nq + i, p)),
        out_shape=jax.ShapeDtypeStruct((m, D_MODEL), BF16),
        scratch_shapes=[pltpu.VMEM((2, LANES, blk), F32),
                        pltpu.VMEM((2, 1, blk), F32),
                        pltpu.VMEM((2, 1, blk), F32),
                        pltpu.VMEM((2, 2, blk, blk), F32),
                        pltpu.VMEM((2, 2, 1, blk), F32)],
        compiler_params=_cparams(3), name="diff_attention",
    )(qkvh, kaug, vt, lam_params, subln_g)


def _decode_diff_body(q_ref, kn_ref, vn_ref, kc_ref, vc_ref, lam_ref, g_ref, o_ref, acc_ref, m_ref, l_ref, *,
                      t_new, past_len, blk, nk, lam_init):
    kj = pl.program_id(1)
    rows = 2 * t_new
    lane = lax.broadcasted_iota(jnp.int32, (t_new, LANES), 1)
    lo_mask = lane < DIFF_HD

    @pl.when(kj == 0)
    def _():
        acc_ref[...] = jnp.zeros_like(acc_ref)
        m_ref[...] = jnp.full_like(m_ref, NEG_INF)
        l_ref[...] = jnp.zeros_like(l_ref)

    def attend(keys, values, bias, visible):
        heads = range(DIFF_HEADS)
        scores = []
        for h in heads:
            q = q_ref[0, h]
            zero = jnp.zeros_like(q)
            q_maps = jnp.concatenate([jnp.where(lo_mask, q, zero), jnp.where(lo_mask, zero, q)], axis=0)
            s = _dot_nt(q_maps, keys(h)) + bias(h)
            scores.append(s if visible is None else jnp.where(visible, s, NEG_INF))
        m_old = [m_ref[h] for h in heads]
        m_new = [jnp.maximum(m_old[h], jnp.max(scores[h], axis=1, keepdims=True)) for h in heads]
        probs = [jnp.exp2(scores[h] - m_new[h]) for h in heads]
        alpha = [jnp.exp2(m_old[h] - m_new[h]) for h in heads]
        for h in heads:
            l_ref[h] = alpha[h] * l_ref[h] + jnp.sum(probs[h], axis=1, keepdims=True)
            m_ref[h] = m_new[h]
            acc_ref[h] = alpha[h] * acc_ref[h] + _dot(probs[h].astype(BF16), values(h))

    def slope(h):
        return 2.0 ** -(h + 1) * LOG2E

    col = lax.broadcasted_iota(jnp.int32, (1, blk), 1)
    dist = (col + (kj * blk - past_len)).astype(F32)
    attend(lambda h: kc_ref[0, pl.ds(h, blk, stride=DIFF_HEADS), :].astype(BF16),
           lambda h: vc_ref[0, pl.ds(h, blk, stride=DIFF_HEADS), :].astype(BF16),
           lambda h: slope(h) * dist, None)

    @pl.when(kj == nk - 1)
    def _():
        tq = lax.broadcasted_iota(jnp.int32, (rows, t_new), 0) % t_new
        tk = lax.broadcasted_iota(jnp.int32, (rows, t_new), 1)
        near = (tq - jnp.abs(tq - tk)).astype(F32)
        visible = ((tk + past_len) // CHUNK) <= ((tq + past_len) // CHUNK)
        attend(lambda h: kn_ref[0, h], lambda h: vn_ref[0, h], lambda h: slope(h) * near, visible)
        lam = _diff_lambda(lam_ref, lam_init)
        g = g_ref[...]
        outs = []
        for h in range(DIFF_HEADS):
            a = acc_ref[h] / l_ref[h]
            outs.append(_subln(a[:t_new] - lam * a[t_new:], g, lam_init))
        o_ref[0] = jnp.concatenate(outs, axis=1).astype(BF16)


def _decode_diff_attention(qkvh, cache_k, cache_v, lam_params, subln_g, n_seq, t_new, blk, lam_init):
    past_len, heads, hd = cache_k.shape[1:]
    nk = past_len // blk
    rows = 2 * t_new
    new_spec = lambda s: pl.BlockSpec((1, DIFF_HEADS, t_new, LANES), lambda b, j: (s, 0, b, 0))
    cache_k = cache_k.reshape(n_seq, past_len * heads, hd)
    cache_v = cache_v.reshape(n_seq, past_len * heads, hd)
    cache_spec = pl.BlockSpec((1, blk * heads, hd), lambda b, j: (b, j, 0))
    out = pl.pallas_call(
        functools.partial(_decode_diff_body, t_new=t_new, past_len=past_len, blk=blk, nk=nk, lam_init=lam_init),
        grid=(n_seq, nk),
        in_specs=[new_spec(0), new_spec(1), new_spec(2), cache_spec, cache_spec,
                  pl.BlockSpec((4, DIFF_HD), lambda b, j: (0, 0)),
                  pl.BlockSpec((1, LANES), lambda b, j: (0, 0))],
        out_specs=pl.BlockSpec((1, t_new, D_MODEL), lambda b, j: (b, 0, 0)),
        out_shape=jax.ShapeDtypeStruct((n_seq, t_new, D_MODEL), BF16),
        scratch_shapes=[pltpu.VMEM((DIFF_HEADS, rows, LANES), F32),
                        pltpu.VMEM((DIFF_HEADS, rows, 1), F32),
                        pltpu.VMEM((DIFF_HEADS, rows, 1), F32)],
        compiler_params=_cparams(2), name="decode_attention_diff",
    )(qkvh, qkvh, qkvh, cache_k, cache_v, lam_params, subln_g)
    return out.reshape(n_seq * t_new, D_MODEL)


def _decode_fox_body(q_ref, knt_ref, vnt_ref, kt_ref, vt_ref, f_ref, o_ref, acc_ref, m_ref, l_ref, *,
                     t_new, blk, nk):
    kj = pl.program_id(1)
    lane = lax.broadcasted_iota(jnp.int32, (t_new, LANES), 1)
    lo_mask = lane < FOX_HD

    @pl.when(kj == 0)
    def _():
        acc_ref[...] = jnp.zeros_like(acc_ref)
        m_ref[...] = jnp.full_like(m_ref, NEG_INF)
        l_ref[...] = jnp.zeros_like(l_ref)

    f_last = f_ref[nk - 1][:, blk - 1:blk]

    def attend(kt_pairs, vt_pairs, bias, visible):
        heads = range(FOX_HEADS)
        scores = []
        for p in range(HEAD_PAIRS):
            q = q_ref[0, p]
            zero = jnp.zeros_like(q)
            kt = kt_pairs(p)
            for qh in (jnp.where(lo_mask, q, zero), jnp.where(lo_mask, zero, q)):
                h = len(scores)
                s = _dot(qh, kt) + bias[h:h + 1, :]
                scores.append(s if visible is None else jnp.where(visible, s, NEG_INF))
        m_old = [m_ref[h] for h in heads]
        m_new = [jnp.maximum(m_old[h], jnp.max(scores[h], axis=1, keepdims=True)) for h in heads]
        probs = [jnp.exp2(scores[h] - m_new[h]) for h in heads]
        alpha = [jnp.exp2(m_old[h] - m_new[h]) for h in heads]
        for h in heads:
            l_ref[h] = alpha[h] * l_ref[h] + jnp.sum(probs[h], axis=1, keepdims=True)
            m_ref[h] = m_new[h]
        for p in range(HEAD_PAIRS):
            vt = vt_pairs(p)
            pv = [_dot_nt(probs[2 * p + e].astype(BF16), vt) for e in range(2)]
            acc = acc_ref[p]
            acc_ref[p] = jnp.where(lo_mask, alpha[2 * p] * acc + pv[0], alpha[2 * p + 1] * acc + pv[1])

    attend(lambda p: kt_ref[0, p].astype(BF16), lambda p: vt_ref[0, p].astype(BF16),
           (f_last - f_ref[kj]) * LOG2E, None)

    @pl.when(kj == nk - 1)
    def _():
        tq = lax.broadcasted_iota(jnp.int32, (t_new, t_new), 0)
        tk = lax.broadcasted_iota(jnp.int32, (t_new, t_new), 1)
        attend(lambda p: knt_ref[0, p * LANES:(p + 1) * LANES, :].astype(BF16),
               lambda p: vnt_ref[0, p * LANES:(p + 1) * LANES, :].astype(BF16),
               (f_last - f_ref[nk][:, :t_new]) * LOG2E, tk <= tq)
        o_ref[0] = jnp.concatenate(
            [acc_ref[p] / jnp.where(lo_mask, l_ref[2 * p], l_ref[2 * p + 1]) for p in range(HEAD_PAIRS)],
            axis=1).astype(BF16)


def _decode_fox_attention(qkvh, knt, vnt, cache_kt, cache_vt, fcum, n_seq, t_new, blk):
    past_len = cache_kt.shape[3]
    nk = past_len // blk
    cache_spec = pl.BlockSpec((1, HEAD_PAIRS, LANES, blk), lambda b, j: (b, 0, 0, j))
    new_spec = pl.BlockSpec((1, D_MODEL, t_new), lambda b, j: (b, 0, 0))
    out = pl.pallas_call(
        functools.partial(_decode_fox_body, t_new=t_new, blk=blk, nk=nk),
        grid=(n_seq, nk),
        in_specs=[pl.BlockSpec((1, HEAD_PAIRS, t_new, LANES), lambda b, j: (0, 0, b, 0)),
                  new_spec, new_spec, cache_spec, cache_spec,
                  pl.BlockSpec((nk + 1, FOX_HEADS, blk), lambda b, j: (b, 0, 0))],
        out_specs=pl.BlockSpec((1, t_new, D_MODEL), lambda b, j: (b, 0, 0)),
        out_shape=jax.ShapeDtypeStruct((n_seq, t_new, D_MODEL), BF16),
        scratch_shapes=[pltpu.VMEM((HEAD_PAIRS, t_new, LANES), F32),
                        pltpu.VMEM((FOX_HEADS, t_new, 1), F32),
                        pltpu.VMEM((FOX_HEADS, t_new, 1), F32)],
        compiler_params=_cparams(2), name="decode_attention_fox",
    )(qkvh, knt, vnt, cache_kt, cache_vt, fcum)
    return out.reshape(n_seq * t_new, D_MODEL)


def _oproj_ln_body(x_ref, o_ref, w_ref, g_ref, b_ref, y_ref):
    y = _dot(o_ref[...], w_ref[...])
    y_ref[...] = _layernorm(DEEPNORM_ALPHA * x_ref[...] + y, g_ref[...], b_ref[...])


def _oproj_ln(x2d, o2d, w, g, b, tm=512):
    m = x2d.shape[0]
    tm = min(tm, m)
    row = pl.BlockSpec((tm, D_MODEL), lambda i: (i, 0))
    vec = pl.BlockSpec((1, D_MODEL), lambda i: (0, 0))
    return pl.pallas_call(
        _oproj_ln_body, grid=(m // tm,),
        in_specs=[row, row, pl.BlockSpec((D_MODEL, D_MODEL), lambda i: (0, 0)), vec, vec],
        out_specs=row, out_shape=jax.ShapeDtypeStruct((m, D_MODEL), F32),
        compiler_params=_cparams(1), name="oproj_ln",
    )(x2d, o2d, w, g, b)


def _ffn_ln_body(x_ref, win_ref, wout_ref, g_ref, b_ref, y_ref):
    x = x_ref[...]
    h = _dot(x.astype(BF16), win_ref[...])
    gate = h[:, :D_FF]
    up = h[:, D_FF:]
    act = (gate * jax.nn.sigmoid(gate) * up).astype(BF16)
    y = _dot(act, wout_ref[...])
    y_ref[...] = _layernorm(DEEPNORM_ALPHA * x + y, g_ref[...], b_ref[...])


def _ffn_ln(x2d, w_in, w_out, g, b, tm=256):
    m = x2d.shape[0]
    tm = min(tm, m)
    row = pl.BlockSpec((tm, D_MODEL), lambda i: (i, 0))
    vec = pl.BlockSpec((1, D_MODEL), lambda i: (0, 0))
    return pl.pallas_call(
        _ffn_ln_body, grid=(m // tm,),
        in_specs=[row,
                  pl.BlockSpec((D_MODEL, 2 * D_FF), lambda i: (0, 0)),
                  pl.BlockSpec((D_FF, D_MODEL), lambda i: (0, 0)),
                  vec, vec],
        out_specs=row, out_shape=jax.ShapeDtypeStruct((m, D_MODEL), F32),
        compiler_params=_cparams(1), name="ffn_ln",
    )(x2d, w_in, w_out, g, b)


def _run_stream(x, past, wts):
    n_seq, t, _ = x.shape
    m = n_seq * t
    blk = ATTN_BLK
    x2d = x.reshape(m, D_MODEL)
    lam_init = 0.8 - 0.6 * math.exp(-0.3 * 1)

    groups = n_seq if t >= blk else 1
    qkvh, fkt, fvt, vt, logf, logf_pad, logft = _qkv_proj(
        x2d, groups, wts["fox_wqkv"], wts["fox_wvt"], FOX_HD ** -0.5 * LOG2E, forget=wts["fox_forget"])
    if past is None:
        kaug, f_start = _fox_keys(logf_pad, qkvh, n_seq, t, blk)
        o = _fox_attention(qkvh, kaug, vt, f_start[:, 0, :FOX_HEADS].reshape(-1), n_seq, t, blk)
    else:
        past_k, past_v, past_lf = past[0], past[1], past[2]
        past_len = past_k.shape[1]
        dblk = min(DECODE_BLK, past_len)
        lf_all = jnp.concatenate(
            [jnp.transpose(past_lf, (0, 2, 1)),
             jnp.transpose(logft.reshape(FOX_HEADS, n_seq, t), (1, 0, 2)),
             jnp.zeros((n_seq, FOX_HEADS, dblk - t), F32)], axis=2)
        fcum = _cumsum_time(lf_all, dblk)
        cache_t = lambda c: jnp.transpose(c, (0, 2, 3, 1)).reshape(n_seq, HEAD_PAIRS, LANES, past_len)
        new_t = lambda a: jnp.transpose(a[0].reshape(D_MODEL, n_seq, t), (1, 0, 2))
        o = _decode_fox_attention(qkvh, new_t(fkt), new_t(fvt), cache_t(past_k), cache_t(past_v), fcum,
                                  n_seq, t, dblk)
    x2d = _oproj_ln(x2d, o, wts["fox_wout"], wts["ln_g"][0][0], wts["ln_b"][0][0])
    x2d = _ffn_ln(x2d, wts["ffn_win"][0], wts["ffn_wout"][0], wts["ln_g"][0][1], wts["ln_b"][0][1])

    qkvh, dk, dv, vt, kaug = _qkv_proj(x2d, n_seq, wts["diff_wqkv"], wts["diff_wvt"], DIFF_HD ** -0.5 * LOG2E)
    if past is None:
        o = _diff_attention(qkvh, kaug, vt, wts["diff_lambda"], wts["diff_g"], n_seq, t, blk, lam_init)
    else:
        o = _decode_diff_attention(qkvh, past[3], past[4], wts["diff_lambda"], wts["diff_g"], n_seq, t,
                                   min(DECODE_BLK, past[3].shape[1]), lam_init)
    x2d = _oproj_ln(x2d, o, wts["diff_wout"], wts["ln_g"][1][0], wts["ln_b"][1][0])
    x2d = _ffn_ln(x2d, wts["ffn_win"][1], wts["ffn_wout"][1], wts["ln_g"][1][1], wts["ln_b"][1][1])

    def untranspose(a):
        a = a.reshape(groups, FOX_HEADS, FOX_HD, n_seq // groups, t)
        return jnp.transpose(a, (0, 3, 4, 1, 2)).reshape(1, n_seq, t, FOX_HEADS, FOX_HD)

    return (x2d.reshape(n_seq, t, D_MODEL),
            untranspose(fkt), untranspose(fvt),
            logf.reshape(1, n_seq, t, FOX_HEADS),
            dk.reshape(1, n_seq, t, DIFF_HEADS, 2 * DIFF_HD), dv.reshape(1, n_seq, t, DIFF_HEADS, 2 * DIFF_HD))


def _prepare_weights(fox_w_in, fox_b_f, fox_w_out, diff_w_in, diff_lambda, diff_subln_g, diff_w_out,
                     ffn_w_in, ffn_w_out, ln_g, ln_b):
    wf = fox_w_in[0][:, 3 * D_MODEL:].astype(BF16)
    wf_pad = jnp.pad(wf, ((0, 0), (0, LANES - FOX_HEADS)))
    bf_pad = jnp.pad(fox_b_f[0].reshape(1, FOX_HEADS), ((0, 0), (0, LANES - FOX_HEADS)))
    return {
        "fox_wqkv": fox_w_in[0][:, :3 * D_MODEL].astype(BF16),
        "fox_wvt": fox_w_in[0][:, 2 * D_MODEL:3 * D_MODEL].T.astype(BF16),
        "fox_forget": (fox_w_in[0][:, D_MODEL:2 * D_MODEL].T.astype(BF16),
                       wf_pad, wf.T, bf_pad, fox_b_f[0].reshape(FOX_HEADS, 1)),
        "fox_wout": fox_w_out[0].astype(BF16),
        "diff_wqkv": diff_w_in[0].astype(BF16),
        "diff_wvt": diff_w_in[0][:, 2 * D_MODEL:].T.astype(BF16),
        "diff_lambda": diff_lambda[0],
        "diff_g": diff_subln_g[0].reshape(1, 2 * DIFF_HD),
        "diff_wout": diff_w_out[0].astype(BF16),
        "ffn_win": [ffn_w_in[i].astype(BF16) for i in range(DEPTH)],
        "ffn_wout": [ffn_w_out[i].astype(BF16) for i in range(DEPTH)],
        "ln_g": [[ln_g[i, j].reshape(1, D_MODEL) for j in range(2)] for i in range(DEPTH)],
        "ln_b": [[ln_b[i, j].reshape(1, D_MODEL) for j in range(2)] for i in range(DEPTH)],
    }


def kernel(x_prompt, x_sample, cache_fox_k, cache_fox_v, cache_fox_logf, cache_diff_k, cache_diff_v, fox_w_in, fox_b_f, fox_w_out, diff_w_in, diff_lambda, diff_subln_g, diff_w_out, ffn_w_in, ffn_w_out, ln_g, ln_b):
    wts = _prepare_weights(fox_w_in, fox_b_f, fox_w_out, diff_w_in, diff_lambda, diff_subln_g, diff_w_out,
                           ffn_w_in, ffn_w_out, ln_g, ln_b)
    y_p, fk_p, fv_p, lf_p, dk_p, dv_p = _run_stream(x_prompt, None, wts)
    past = (cache_fox_k[0], cache_fox_v[0], cache_fox_logf[0], cache_diff_k[0], cache_diff_v[0])
    y_s, fk_s, fv_s, lf_s, dk_s, dv_s = _run_stream(x_sample, past, wts)
    return (y_p, y_s, fk_p, fv_p, lf_p, dk_p, dv_p, fk_s, fv_s, lf_s, dk_s, dv_s)
```

```python
import functools
import math

import jax
import jax.numpy as jnp
import numpy as np
from jax import lax
from jax.experimental import pallas as pl
from jax.experimental.pallas import tpu as pltpu

F32 = jnp.float32
BF16 = jnp.bfloat16

D_MODEL = 1024
DEPTH = 2
FOX_HEADS = 16
FOX_HD = 64
DIFF_HEADS = 8
DIFF_HD = 64
D_FF = 2816
CHUNK = 64
DEEPNORM_ALPHA = (2.0 * DEPTH) ** 0.25
LN_EPS = 1e-5
NEG_INF = -1e30
LOG2E = 1.4426950408889634

LANES = 128
HEAD_PAIRS = D_MODEL // LANES
ATTN_BLK = 512
ONES_ROWS = 16
DECODE_BLK = 1024
VMEM_LIMIT = 56 * 1024 * 1024


def _cparams(n_axes):
    return pltpu.CompilerParams(dimension_semantics=("arbitrary",) * n_axes,
                                vmem_limit_bytes=VMEM_LIMIT)


def _log_sigmoid(z):
    return jnp.minimum(z, 0.0) - jnp.log1p(jnp.exp(-jnp.abs(z)))


def _layernorm(z, g, b):
    mu = jnp.mean(z, axis=-1, keepdims=True)
    zc = z - mu
    var = jnp.mean(zc * zc, axis=-1, keepdims=True)
    return zc * lax.rsqrt(var + LN_EPS) * g + b


def _dot(a, b):
    return jnp.dot(a, b, preferred_element_type=F32)


def _dot_nt(a, b):
    return lax.dot_general(a, b, (((1,), (1,)), ((), ())), preferred_element_type=F32)


def _vt_rows(group):
    return (LANES // group) * (group + ONES_ROWS)


def _split3(x):
    hi = x.astype(BF16).astype(F32)
    r = x - hi
    mid = r.astype(BF16).astype(F32)
    lo = (r - mid).astype(BF16).astype(F32)
    return hi, mid, lo


def _alibi_columns(rows):
    slope = np.asarray([2.0 ** -(h + 1) * LOG2E for h in range(DIFF_HEADS)], np.float32)
    x = np.arange(rows, dtype=np.float32)[None, :] * slope[:, None]
    cols = np.zeros((DIFF_HEADS, rows, LANES), np.float32)
    for term in range(3):
        part = x.astype(BF16).astype(np.float32)
        cols[:, :, term] = part
        x = x - part
    return jnp.asarray(cols.astype(BF16))


def _proj_body(*refs, q_scale, with_forget):
    if with_forget:
        (x_ref, w_ref, wvt_ref, wkt_ref, wf_ref, wft_ref, bf_ref, bft_ref,
         qkv_ref, k_ref, v_ref, vt_ref, lf_ref, lfp_ref, lft_ref) = refs
    else:
        x_ref, w_ref, wvt_ref, pos_ref, qkv_ref, k_ref, v_ref, vt_ref, ka_ref = refs
    tm = x_ref.shape[0]
    xb = x_ref[...].astype(BF16)

    def project(part):
        return _dot(xb, w_ref[:, part * D_MODEL:(part + 1) * D_MODEL])

    def store_heads(part, val):
        vb = val.astype(BF16)
        for p in range(HEAD_PAIRS):
            qkv_ref[part, p] = vb[:, p * LANES:(p + 1) * LANES]
        return vb

    store_heads(0, project(0) * q_scale)
    k = project(1)
    kb = store_heads(1, k)
    vt32 = _dot_nt(wvt_ref[...], xb)
    if with_forget:
        lfp = _log_sigmoid(_dot(xb, wf_ref[...]) + bf_ref[...])
        lfp_ref[...] = lfp
        lf_ref[...] = lfp[:, :FOX_HEADS]
        lft_ref[...] = _log_sigmoid(_dot_nt(wft_ref[...], xb) + bft_ref[...])
        k_ref[0] = _dot_nt(wkt_ref[...], xb)
        v_ref[0] = vt32
    else:
        k_ref[...] = k
        for h in range(DIFF_HEADS):
            ka_ref[h, :, 0:LANES] = kb[:, h * LANES:(h + 1) * LANES]
            ka_ref[h, :, LANES:2 * LANES] = pos_ref[h]
        v = project(2)
        store_heads(2, v)
        v_ref[...] = v
    vt = vt32.astype(BF16)
    ones = jnp.ones((ONES_ROWS, tm), BF16)
    group = FOX_HD if with_forget else LANES
    for p in range(HEAD_PAIRS):
        for gi in range(LANES // group):
            src = p * LANES + gi * group
            dst = gi * (group + ONES_ROWS)
            vt_ref[p, 0, dst:dst + group, :] = vt[src:src + group, :]
            vt_ref[p, 0, dst + group:dst + group + ONES_ROWS, :] = ones


def _qkv_proj(x2d, n_seq, w_qkv, w_vt, q_scale, forget=None):
    m = x2d.shape[0]
    tm = min(ATTN_BLK, m)
    nblk = m // tm
    group = FOX_HD if forget is not None else LANES
    vt_rows = _vt_rows(group)
    row = pl.BlockSpec((tm, D_MODEL), lambda i: (i, 0))
    const = lambda shape: pl.BlockSpec(shape, lambda i: (0,) * len(shape))
    if forget is not None:
        n_parts = 2
        per_seq = nblk // n_seq
        kv_shape = jax.ShapeDtypeStruct((n_seq, D_MODEL, m // n_seq), F32)
        kv_spec = pl.BlockSpec((1, D_MODEL, tm), lambda i: (i // per_seq, 0, i % per_seq))
    else:
        n_parts = 3
        kv_shape = jax.ShapeDtypeStruct((m, D_MODEL), F32)
        kv_spec = row
    in_specs = [row, const((D_MODEL, n_parts * D_MODEL)), const((D_MODEL, D_MODEL))]
    out_shape = [jax.ShapeDtypeStruct((n_parts, HEAD_PAIRS, m, LANES), BF16),
                 kv_shape, kv_shape,
                 jax.ShapeDtypeStruct((HEAD_PAIRS, nblk, vt_rows, tm), BF16)]
    out_specs = [pl.BlockSpec((n_parts, HEAD_PAIRS, tm, LANES), lambda i: (0, 0, i, 0)),
                 kv_spec, kv_spec,
                 pl.BlockSpec((HEAD_PAIRS, 1, vt_rows, tm), lambda i: (0, i, 0, 0))]
    args = [x2d, w_qkv[:, :n_parts * D_MODEL], w_vt]
    if forget is not None:
        w_kt, wf, wft, bf, bft = forget
        in_specs.append(const((D_MODEL, D_MODEL)))
        args.append(w_kt)
        in_specs += [const((D_MODEL, LANES)), const((FOX_HEADS, D_MODEL)), const((1, LANES)), const((FOX_HEADS, 1))]
        out_shape += [jax.ShapeDtypeStruct((m, FOX_HEADS), F32),
                      jax.ShapeDtypeStruct((m, LANES), F32),
                      jax.ShapeDtypeStruct((FOX_HEADS, m), F32)]
        out_specs += [pl.BlockSpec((tm, FOX_HEADS), lambda i: (i, 0)),
                      pl.BlockSpec((tm, LANES), lambda i: (i, 0)),
                      pl.BlockSpec((FOX_HEADS, tm), lambda i: (0, i))]
        args += [wf, wft, bf, bft]
    else:
        in_specs.append(const((DIFF_HEADS, tm, LANES)))
        args.append(_alibi_columns(tm))
        out_shape.append(jax.ShapeDtypeStruct((DIFF_HEADS, m, 2 * LANES), BF16))
        out_specs.append(pl.BlockSpec((DIFF_HEADS, tm, 2 * LANES), lambda i: (0, i, 0)))
    return pl.pallas_call(
        functools.partial(_proj_body, q_scale=q_scale, with_forget=forget is not None),
        grid=(nblk,), in_specs=in_specs, out_specs=out_specs, out_shape=out_shape,
        compiler_params=_cparams(1), name="qkv_proj_f" if forget is not None else "qkv_proj",
    )(*args)


def _fox_keys_body(lf_ref, k_ref, ka_ref, fs_ref, carry_ref, *, blk):
    i = pl.program_id(1)

    @pl.when(i == 0)
    def _():
        carry_ref[...] = jnp.zeros_like(carry_ref)

    row = lax.broadcasted_iota(jnp.int32, (blk, blk), 0)
    col = lax.broadcasted_iota(jnp.int32, (blk, blk), 1)
    tri = (col <= row).astype(BF16)
    hi, mid, lo = _split3(lf_ref[...])
    f_loc = _dot(tri, hi.astype(BF16)) + _dot(tri, mid.astype(BF16)) + _dot(tri, lo.astype(BF16))
    fs_ref[0] = carry_ref[...]
    carry_ref[...] = carry_ref[...] + f_loc[blk - 1:blk, :]

    bhi, bmid, blo = (t.astype(BF16) for t in _split3(f_loc * (-LOG2E)))
    sr = lax.broadcasted_iota(jnp.int32, (LANES, LANES), 0)
    sc = lax.broadcasted_iota(jnp.int32, (LANES, LANES), 1)
    for p in range(HEAD_PAIRS):
        def pick(term):
            return (((sr == 2 * p) & (sc == term)) | ((sr == 2 * p + 1) & (sc == 3 + term))).astype(BF16)
        cols = _dot(bhi, pick(0)) + _dot(bmid, pick(1)) + _dot(blo, pick(2))
        ka_ref[p, :, 0:LANES] = k_ref[0, p]
        ka_ref[p, :, LANES:2 * LANES] = cols.astype(BF16)


def _fox_keys(lf_pad, qkvh, n_seq, t, blk):
    m = n_seq * t
    nblk = t // blk
    return pl.pallas_call(
        functools.partial(_fox_keys_body, blk=blk),
        grid=(n_seq, nblk),
        in_specs=[pl.BlockSpec((blk, LANES), lambda b, i: (b * nblk + i, 0)),
                  pl.BlockSpec((1, HEAD_PAIRS, blk, LANES), lambda b, i: (1, 0, b * nblk + i, 0))],
        out_specs=[pl.BlockSpec((HEAD_PAIRS, blk, 2 * LANES), lambda b, i: (0, b * nblk + i, 0)),
                   pl.BlockSpec((1, 1, LANES), lambda b, i: (b * nblk + i, 0, 0))],
        out_shape=[jax.ShapeDtypeStruct((HEAD_PAIRS, m, 2 * LANES), BF16),
                   jax.ShapeDtypeStruct((n_seq * nblk, 1, LANES), F32)],
        scratch_shapes=[pltpu.VMEM((1, LANES), F32)],
        compiler_params=_cparams(2), name="fox_keys",
    )(lf_pad, qkvh)


def _cumsum_body(x_ref, o_ref, carry_ref, tri_ref, *, blk):
    i = pl.program_id(1)

    @pl.when((pl.program_id(0) == 0) & (i == 0))
    def _():
        row = lax.broadcasted_iota(jnp.int32, (blk, blk), 0)
        col = lax.broadcasted_iota(jnp.int32, (blk, blk), 1)
        tri_ref[...] = (row <= col).astype(BF16)

    @pl.when(i == 0)
    def _():
        carry_ref[...] = jnp.zeros_like(carry_ref)

    hi, mid, lo = _split3(x_ref[0])
    tri = tri_ref[...]
    cs = _dot(hi.astype(BF16), tri) + _dot(mid.astype(BF16), tri) + _dot(lo.astype(BF16), tri) + carry_ref[...]
    o_ref[0] = cs
    carry_ref[...] = cs[:, blk - 1:blk]


def _cumsum_time(lf, blk):
    n_seq, h, r = lf.shape
    nblk = r // blk
    return pl.pallas_call(
        functools.partial(_cumsum_body, blk=blk),
        grid=(n_seq, nblk),
        in_specs=[pl.BlockSpec((1, h, blk), lambda b, i: (b, 0, i))],
        out_specs=pl.BlockSpec((1, h, blk), lambda b, i: (b * nblk + i, 0, 0)),
        out_shape=jax.ShapeDtypeStruct((n_seq * nblk, h, blk), F32),
        scratch_shapes=[pltpu.VMEM((h, 1), F32), pltpu.VMEM((blk, blk), BF16)],
        compiler_params=_cparams(2), name="cumsum_time",
    )(lf)


def _augment_queries(q, blk, ones_first_lane):
    lane = lax.broadcasted_iota(jnp.int32, (blk, LANES), 1)
    lo_mask = lane < LANES // 2
    zero = jnp.zeros_like(q)
    out = []
    for u, part in enumerate((jnp.where(lo_mask, q, zero), jnp.where(lo_mask, zero, q))):
        first = ones_first_lane[u]
        ones = jnp.where((lane >= first) & (lane < first + 3), 1.0, 0.0).astype(BF16)
        out.append(jnp.concatenate([part, ones], axis=1))
    return out


def _softmax_step(s, bmax, c, vt, m_ref, l_ref, u):
    group = vt.shape[0] - ONES_ROWS
    m_old = m_ref[u]
    m_new = jnp.maximum(m_old, bmax + c)
    p = jnp.exp2(s - (m_new - c))
    alpha = jnp.exp2(m_old - m_new)
    pv = _dot(vt, p.astype(BF16))
    l_ref[u] = alpha * l_ref[u] + pv[group:group + 1, :]
    m_ref[u] = m_new
    return alpha, pv[:group, :]


def _sweep_key_tiles(qi, produce, consume):
    @pl.when(qi == 0)
    def _():
        produce(0, 0, True)

    @pl.when(qi > 0)
    def _():
        n_pairs = lax.shift_right_logical(qi - 1, 1)
        odd = (qi - 1) & 1

        @pl.when(odd == 1)
        def _():
            produce(0, 0, False)
            produce(1, 1, False)
            consume(0, 0)

        @pl.when(odd == 0)
        def _():
            produce(0, 1, False)

        def two_tiles(j):
            produce(j + 1, 0, False)
            consume(j, 1)
            produce(j + 2, 1, False)
            consume(j + 1, 0)

        def body2(i, carry):
            two_tiles(odd + 2 * i)
            return carry

        n_quads = lax.shift_right_logical(n_pairs, 1)

        def body4(i, carry):
            j = odd + 2 * (n_pairs & 1)
            two_tiles(j)
            two_tiles(j + 2)
            return carry

        def body8(i, carry):
            j = odd + 2 * (n_pairs & 1) + 4 * (n_quads & 1) + 8 * i
            for step in range(4):
                two_tiles(j + 2 * step)
            return carry

        lax.fori_loop(0, n_pairs & 1, body2, 0)
        lax.fori_loop(0, n_quads & 1, body4, 0)
        lax.fori_loop(0, lax.shift_right_logical(n_quads, 1), body8, 0)
        produce(qi, 0, True)
        consume(qi - 1, 1)

    consume(qi, 0)


def _fox_attn_body(fs_ref, q_ref, ka_ref, vt_ref, o_ref, acc_ref, m_ref, l_ref, s_ref, bm_ref, *, blk, nq):
    seq = pl.program_id(0)
    pair = pl.program_id(1)
    qi = pl.program_id(2)
    qa = _augment_queries(q_ref[0, 0], blk, (0, 3))
    acc_ref[...] = jnp.zeros_like(acc_ref)
    m_ref[...] = jnp.full_like(m_ref, NEG_INF)
    l_ref[...] = jnp.zeros_like(l_ref)
    half = FOX_HD

    def produce(kj, slot, diagonal):
        start = pl.multiple_of(kj * blk, blk)
        ka = ka_ref[0, pl.ds(start, blk), :]
        if diagonal:
            key = lax.broadcasted_iota(jnp.int32, (blk, blk), 0)
            query = lax.broadcasted_iota(jnp.int32, (blk, blk), 1)
            causal = key <= query
        for h in range(2):
            s = _dot_nt(ka, qa[h])
            if diagonal:
                s = jnp.where(causal, s, NEG_INF)
            s_ref[slot, h] = s
            bm_ref[slot, h] = jnp.max(s, axis=0, keepdims=True)

    def consume(kj, slot):
        vt = vt_ref[0, kj]
        for h in range(2):
            head = 2 * pair + h
            f_q = jnp.full((1, blk), fs_ref[(seq * nq + qi) * FOX_HEADS + head], F32)
            f_k = jnp.full((1, blk), fs_ref[(seq * nq + kj) * FOX_HEADS + head], F32)
            c = (f_q - f_k) * LOG2E
            vt_h = vt[h * (half + ONES_ROWS):(h + 1) * (half + ONES_ROWS), :]
            alpha, pv = _softmax_step(s_ref[slot, h], bm_ref[slot, h], c, vt_h, m_ref, l_ref, h)
            rows = slice(h * half, (h + 1) * half)
            acc_ref[rows, :] = alpha * acc_ref[rows, :] + pv

    _sweep_key_tiles(qi, produce, consume)
    sub = lax.broadcasted_iota(jnp.int32, (LANES, blk), 0)
    o_t = acc_ref[...] / jnp.where(sub < half, l_ref[0], l_ref[1])
    o_ref[...] = o_t.T.astype(BF16)


def _fox_attention(qkvh, kaug, vt, f_start, n_seq, t, blk):
    m = n_seq * t
    nq = t // blk
    grid_spec = pltpu.PrefetchScalarGridSpec(
        num_scalar_prefetch=1,
        grid=(n_seq, HEAD_PAIRS, nq),
        in_specs=[pl.BlockSpec((1, 1, blk, LANES), lambda b, p, i, fs: (0, p, b * nq + i, 0)),
                  pl.BlockSpec((1, t, 2 * LANES), lambda b, p, i, fs: (p, b, 0)),
                  pl.BlockSpec((1, nq, _vt_rows(FOX_HD), blk), lambda b, p, i, fs: (p, b, 0, 0))],
        out_specs=pl.BlockSpec((blk, LANES), lambda b, p, i, fs: (b * nq + i, p)),
        scratch_shapes=[pltpu.VMEM((LANES, blk), F32),
                        pltpu.VMEM((2, 1, blk), F32),
                        pltpu.VMEM((2, 1, blk), F32),
                        pltpu.VMEM((2, 2, blk, blk), F32),
                        pltpu.VMEM((2, 2, 1, blk), F32)])
    return pl.pallas_call(
        functools.partial(_fox_attn_body, blk=blk, nq=nq),
        grid_spec=grid_spec,
        out_shape=jax.ShapeDtypeStruct((m, D_MODEL), BF16),
        compiler_params=_cparams(3), name="fox_attention",
    )(f_start, qkvh, kaug, vt)


def _diff_lambda(lam_ref, lam_init):
    lp = lam_ref[...]
    a = jnp.sum(lp[0:1] * lp[1:2], axis=1, keepdims=True)
    b = jnp.sum(lp[2:3] * lp[3:4], axis=1, keepdims=True)
    return jnp.exp(a) - jnp.exp(b) + lam_init


def _subln(o, g, lam_init):
    o = o * lax.rsqrt(jnp.mean(o * o, axis=-1, keepdims=True) + LN_EPS)
    return o * g * (1.0 - lam_init)


def _diff_attn_body(q_ref, ka_ref, vt_ref, lam_ref, g_ref, o_ref, acc_ref, m_ref, l_ref, s_ref, bm_ref, *,
                    blk, lam_init):
    head = pl.program_id(1)
    qi = pl.program_id(2)
    qa = _augment_queries(q_ref[0, 0], blk, (0, 0))
    acc_ref[...] = jnp.zeros_like(acc_ref)
    m_ref[...] = jnp.full_like(m_ref, NEG_INF)
    l_ref[...] = jnp.zeros_like(l_ref)
    slope = lax.bitcast_convert_type(jnp.full((1, blk), 126 - head, jnp.int32) << 23, F32) * LOG2E

    def produce(kj, slot, diagonal):
        start = pl.multiple_of(kj * blk, blk)
        ka = ka_ref[0, pl.ds(start, blk), :]
        if diagonal:
            key = lax.broadcasted_iota(jnp.int32, (blk, blk), 0)
            query = lax.broadcasted_iota(jnp.int32, (blk, blk), 1)
            visible = (key // CHUNK) <= (query // CHUNK)
            ahead = slope * (-2.0 * jnp.maximum(key - query, 0).astype(F32))
        for u in range(2):
            s = _dot_nt(ka, qa[u])
            if diagonal:
                s = jnp.where(visible, s + ahead, NEG_INF)
            s_ref[slot, u] = s
            bm_ref[slot, u] = jnp.max(s, axis=0, keepdims=True)

    def consume(kj, slot):
        vt = vt_ref[0, kj]
        c = slope * jnp.full((1, blk), (kj - qi) * blk, jnp.int32).astype(F32)
        for u in range(2):
            alpha, pv = _softmax_step(s_ref[slot, u], bm_ref[slot, u], c, vt, m_ref, l_ref, u)
            acc_ref[u] = alpha * acc_ref[u] + pv

    _sweep_key_tiles(qi, produce, consume)
    lam = _diff_lambda(lam_ref, lam_init)
    o_t = acc_ref[0] / l_ref[0] - lam * (acc_ref[1] / l_ref[1])
    o_ref[...] = _subln(o_t.T, g_ref[...], lam_init).astype(BF16)


def _diff_attention(qkvh, kaug, vt, lam_params, subln_g, n_seq, t, blk, lam_init):
    m = n_seq * t
    nq = t // blk
    return pl.pallas_call(
        functools.partial(_diff_attn_body, blk=blk, lam_init=lam_init),
        grid=(n_seq, DIFF_HEADS, nq),
        in_specs=[pl.BlockSpec((1, 1, blk, LANES), lambda b, p, i: (0, p, b * nq + i, 0)),
                  pl.BlockSpec((1, t, 2 * LANES), lambda b, p, i: (p, b, 0)),
                  pl.BlockSpec((1, nq, _vt_rows(LANES), blk), lambda b, p, i: (p, b, 0, 0)),
                  pl.BlockSpec((4, DIFF_HD), lambda b, p, i: (0, 0)),
                  pl.BlockSpec((1, LANES), lambda b, p, i: (0, 0))],
        out_specs=pl.BlockSpec((blk, LANES), lambda b, p, i: (b * nq + i, p)),
        out_shape=jax.ShapeDtypeStruct((m, D_MODEL), BF16),
        scratch_shapes=[pltpu.VMEM((2, LANES, blk), F32),
                        pltpu.VMEM((2, 1, blk), F32),
                        pltpu.VMEM((2, 1, blk), F32),
                        pltpu.VMEM((2, 2, blk, blk), F32),
                        pltpu.VMEM((2, 2, 1, blk), F32)],
        compiler_params=_cparams(3), name="diff_attention",
    )(qkvh, kaug, vt, lam_params, subln_g)


def _decode_diff_body(q_ref, kn_ref, vn_ref, kc_ref, vc_ref, lam_ref, g_ref, o_ref, acc_ref, m_ref, l_ref, *,
                      t_new, past_len, blk, nk, lam_init):
    kj = pl.program_id(1)
    rows = 2 * t_new
    lane = lax.broadcasted_iota(jnp.int32, (t_new, LANES), 1)
    lo_mask = lane < DIFF_HD

    @pl.when(kj == 0)
    def _():
        acc_ref[...] = jnp.zeros_like(acc_ref)
        m_ref[...] = jnp.full_like(m_ref, NEG_INF)
        l_ref[...] = jnp.zeros_like(l_ref)

    def attend(keys, values, bias, visible):
        heads = range(DIFF_HEADS)
        scores = []
        for h in heads:
            q = q_ref[0, h]
            zero = jnp.zeros_like(q)
            q_maps = jnp.concatenate([jnp.where(lo_mask, q, zero), jnp.where(lo_mask, zero, q)], axis=0)
            s = _dot_nt(q_maps, keys(h)) + bias(h)
            scores.append(s if visible is None else jnp.where(visible, s, NEG_INF))
        m_old = [m_ref[h] for h in heads]
        m_new = [jnp.maximum(m_old[h], jnp.max(scores[h], axis=1, keepdims=True)) for h in heads]
        probs = [jnp.exp2(scores[h] - m_new[h]) for h in heads]
        alpha = [jnp.exp2(m_old[h] - m_new[h]) for h in heads]
        for h in heads:
            l_ref[h] = alpha[h] * l_ref[h] + jnp.sum(probs[h], axis=1, keepdims=True)
            m_ref[h] = m_new[h]
            acc_ref[h] = alpha[h] * acc_ref[h] + _dot(probs[h].astype(BF16), values(h))

    def slope(h):
        return 2.0 ** -(h + 1) * LOG2E

    col = lax.broadcasted_iota(jnp.int32, (1, blk), 1)
    dist = (col + (kj * blk - past_len)).astype(F32)
    attend(lambda h: kc_ref[0, pl.ds(h, blk, stride=DIFF_HEADS), :].astype(BF16),
           lambda h: vc_ref[0, pl.ds(h, blk, stride=DIFF_HEADS), :].astype(BF16),
           lambda h: slope(h) * dist, None)

    @pl.when(kj == nk - 1)
    def _():
        tq = lax.broadcasted_iota(jnp.int32, (rows, t_new), 0) % t_new
        tk = lax.broadcasted_iota(jnp.int32, (rows, t_new), 1)
        near = (tq - jnp.abs(tq - tk)).astype(F32)
        visible = ((tk + past_len) // CHUNK) <= ((tq + past_len) // CHUNK)
        attend(lambda h: kn_ref[0, h], lambda h: vn_ref[0, h], lambda h: slope(h) * near, visible)
        lam = _diff_lambda(lam_ref, lam_init)
        g = g_ref[...]
        outs = []
        for h in range(DIFF_HEADS):
            a = acc_ref[h] / l_ref[h]
            outs.append(_subln(a[:t_new] - lam * a[t_new:], g, lam_init))
        o_ref[0] = jnp.concatenate(outs, axis=1).astype(BF16)


def _decode_diff_attention(qkvh, cache_k, cache_v, lam_params, subln_g, n_seq, t_new, blk, lam_init):
    past_len, heads, hd = cache_k.shape[1:]
    nk = past_len // blk
    rows = 2 * t_new
    new_spec = lambda s: pl.BlockSpec((1, DIFF_HEADS, t_new, LANES), lambda b, j: (s, 0, b, 0))
    cache_k = cache_k.reshape(n_seq, past_len * heads, hd)
    cache_v = cache_v.reshape(n_seq, past_len * heads, hd)
    cache_spec = pl.BlockSpec((1, blk * heads, hd), lambda b, j: (b, j, 0))
    out = pl.pallas_call(
        functools.partial(_decode_diff_body, t_new=t_new, past_len=past_len, blk=blk, nk=nk, lam_init=lam_init),
        grid=(n_seq, nk),
        in_specs=[new_spec(0), new_spec(1), new_spec(2), cache_spec, cache_spec,
                  pl.BlockSpec((4, DIFF_HD), lambda b, j: (0, 0)),
                  pl.BlockSpec((1, LANES), lambda b, j: (0, 0))],
        out_specs=pl.BlockSpec((1, t_new, D_MODEL), lambda b, j: (b, 0, 0)),
        out_shape=jax.ShapeDtypeStruct((n_seq, t_new, D_MODEL), BF16),
        scratch_shapes=[pltpu.VMEM((DIFF_HEADS, rows, LANES), F32),
                        pltpu.VMEM((DIFF_HEADS, rows, 1), F32),
                        pltpu.VMEM((DIFF_HEADS, rows, 1), F32)],
        compiler_params=_cparams(2), name="decode_attention_diff",
    )(qkvh, qkvh, qkvh, cache_k, cache_v, lam_params, subln_g)
    return out.reshape(n_seq * t_new, D_MODEL)


def _decode_fox_body(q_ref, knt_ref, vnt_ref, kt_ref, vt_ref, f_ref, o_ref, acc_ref, m_ref, l_ref, *,
                     t_new, blk, nk):
    kj = pl.program_id(1)
    lane = lax.broadcasted_iota(jnp.int32, (t_new, LANES), 1)
    lo_mask = lane < FOX_HD

    @pl.when(kj == 0)
    def _():
        acc_ref[...] = jnp.zeros_like(acc_ref)
        m_ref[...] = jnp.full_like(m_ref, NEG_INF)
        l_ref[...] = jnp.zeros_like(l_ref)

    f_last = f_ref[nk - 1][:, blk - 1:blk]

    def attend(kt_pairs, vt_pairs, bias, visible):
        heads = range(FOX_HEADS)
        scores = []
        for p in range(HEAD_PAIRS):
            q = q_ref[0, p]
            zero = jnp.zeros_like(q)
            kt = kt_pairs(p)
            for qh in (jnp.where(lo_mask, q, zero), jnp.where(lo_mask, zero, q)):
                h = len(scores)
                s = _dot(qh, kt) + bias[h:h + 1, :]
                scores.append(s if visible is None else jnp.where(visible, s, NEG_INF))
        m_old = [m_ref[h] for h in heads]
        m_new = [jnp.maximum(m_old[h], jnp.max(scores[h], axis=1, keepdims=True)) for h in heads]
        probs = [jnp.exp2(scores[h] - m_new[h]) for h in heads]
        alpha = [jnp.exp2(m_old[h] - m_new[h]) for h in heads]
        for h in heads:
            l_ref[h] = alpha[h] * l_ref[h] + jnp.sum(probs[h], axis=1, keepdims=True)
            m_ref[h] = m_new[h]
        for p in range(HEAD_PAIRS):
            vt = vt_pairs(p)
            pv = [_dot_nt(probs[2 * p + e].astype(BF16), vt) for e in range(2)]
            acc = acc_ref[p]
            acc_ref[p] = jnp.where(lo_mask, alpha[2 * p] * acc + pv[0], alpha[2 * p + 1] * acc + pv[1])

    attend(lambda p: kt_ref[0, p].astype(BF16), lambda p: vt_ref[0, p].astype(BF16),
           (f_last - f_ref[kj]) * LOG2E, None)

    @pl.when(kj == nk - 1)
    def _():
        tq = lax.broadcasted_iota(jnp.int32, (t_new, t_new), 0)
        tk = lax.broadcasted_iota(jnp.int32, (t_new, t_new), 1)
        attend(lambda p: knt_ref[0, p * LANES:(p + 1) * LANES, :].astype(BF16),
               lambda p: vnt_ref[0, p * LANES:(p + 1) * LANES, :].astype(BF16),
               (f_last - f_ref[nk][:, :t_new]) * LOG2E, tk <= tq)
        o_ref[0] = jnp.concatenate(
            [acc_ref[p] / jnp.where(lo_mask, l_ref[2 * p], l_ref[2 * p + 1]) for p in range(HEAD_PAIRS)],
            axis=1).astype(BF16)


def _decode_fox_attention(qkvh, knt, vnt, cache_kt, cache_vt, fcum, n_seq, t_new, blk):
    past_len = cache_kt.shape[3]
    nk = past_len // blk
    cache_spec = pl.BlockSpec((1, HEAD_PAIRS, LANES, blk), lambda b, j: (b, 0, 0, j))
    new_spec = pl.BlockSpec((1, D_MODEL, t_new), lambda b, j: (b, 0, 0))
    out = pl.pallas_call(
        functools.partial(_decode_fox_body, t_new=t_new, blk=blk, nk=nk),
        grid=(n_seq, nk),
        in_specs=[pl.BlockSpec((1, HEAD_PAIRS, t_new, LANES), lambda b, j: (0, 0, b, 0)),
                  new_spec, new_spec, cache_spec, cache_spec,
                  pl.BlockSpec((nk + 1, FOX_HEADS, blk), lambda b, j: (b, 0, 0))],
        out_specs=pl.BlockSpec((1, t_new, D_MODEL), lambda b, j: (b, 0, 0)),
        out_shape=jax.ShapeDtypeStruct((n_seq, t_new, D_MODEL), BF16),
        scratch_shapes=[pltpu.VMEM((HEAD_PAIRS, t_new, LANES), F32),
                        pltpu.VMEM((FOX_HEADS, t_new, 1), F32),
                        pltpu.VMEM((FOX_HEADS, t_new, 1), F32)],
        compiler_params=_cparams(2), name="decode_attention_fox",
    )(qkvh, knt, vnt, cache_kt, cache_vt, fcum)
    return out.reshape(n_seq * t_new, D_MODEL)


def _mixer_out_ffn_body(x_ref, o_ref, wo_ref, g1_ref, b1_ref, win_ref, wout_ref, g2_ref, b2_ref, y_ref):
    x = _layernorm(DEEPNORM_ALPHA * x_ref[...] + _dot(o_ref[...], wo_ref[...]), g1_ref[...], b1_ref[...])
    h = _dot(x.astype(BF16), win_ref[...])
    gate = h[:, :D_FF]
    up = h[:, D_FF:]
    act = (gate * jax.nn.sigmoid(gate) * up).astype(BF16)
    y = _dot(act, wout_ref[...])
    y_ref[...] = _layernorm(DEEPNORM_ALPHA * x + y, g2_ref[...], b2_ref[...])


def _mixer_out_ffn(x2d, o2d, w_o, g1, b1, w_in, w_out, g2, b2, tm=256):
    m = x2d.shape[0]
    tm = min(tm, m)
    row = pl.BlockSpec((tm, D_MODEL), lambda i: (i, 0))
    vec = pl.BlockSpec((1, D_MODEL), lambda i: (0, 0))
    return pl.pallas_call(
        _mixer_out_ffn_body, grid=(m // tm,),
        in_specs=[row, row, pl.BlockSpec((D_MODEL, D_MODEL), lambda i: (0, 0)), vec, vec,
                  pl.BlockSpec((D_MODEL, 2 * D_FF), lambda i: (0, 0)),
                  pl.BlockSpec((D_FF, D_MODEL), lambda i: (0, 0)),
                  vec, vec],
        out_specs=row, out_shape=jax.ShapeDtypeStruct((m, D_MODEL), F32),
        compiler_params=_cparams(1), name="mixer_out_ffn",
    )(x2d, o2d, w_o, g1, b1, w_in, w_out, g2, b2)


def _run_stream(x, past, wts):
    n_seq, t, _ = x.shape
    m = n_seq * t
    blk = ATTN_BLK
    x2d = x.reshape(m, D_MODEL)
    lam_init = 0.8 - 0.6 * math.exp(-0.3 * 1)

    groups = n_seq if t >= blk else 1
    qkvh, fkt, fvt, vt, logf, logf_pad, logft = _qkv_proj(
        x2d, groups, wts["fox_wqkv"], wts["fox_wvt"], FOX_HD ** -0.5 * LOG2E, forget=wts["fox_forget"])
    if past is None:
        kaug, f_start = _fox_keys(logf_pad, qkvh, n_seq, t, blk)
        o = _fox_attention(qkvh, kaug, vt, f_start[:, 0, :FOX_HEADS].reshape(-1), n_seq, t, blk)
    else:
        past_k, past_v, past_lf = past[0], past[1], past[2]
        past_len = past_k.shape[1]
        dblk = min(DECODE_BLK, past_len)
        lf_all = jnp.concatenate(
            [jnp.transpose(past_lf, (0, 2, 1)),
             jnp.transpose(logft.reshape(FOX_HEADS, n_seq, t), (1, 0, 2)),
             jnp.zeros((n_seq, FOX_HEADS, dblk - t), F32)], axis=2)
        fcum = _cumsum_time(lf_all, dblk)
        cache_t = lambda c: jnp.transpose(c, (0, 2, 3, 1)).reshape(n_seq, HEAD_PAIRS, LANES, past_len)
        new_t = lambda a: jnp.transpose(a[0].reshape(D_MODEL, n_seq, t), (1, 0, 2))
        o = _decode_fox_attention(qkvh, new_t(fkt), new_t(fvt), cache_t(past_k), cache_t(past_v), fcum,
                                  n_seq, t, dblk)
    x2d = _mixer_out_ffn(x2d, o, wts["fox_wout"], wts["ln_g"][0][0], wts["ln_b"][0][0],
                         wts["ffn_win"][0], wts["ffn_wout"][0], wts["ln_g"][0][1], wts["ln_b"][0][1])

    qkvh, dk, dv, vt, kaug = _qkv_proj(x2d, n_seq, wts["diff_wqkv"], wts["diff_wvt"], DIFF_HD ** -0.5 * LOG2E)
    if past is None:
        o = _diff_attention(qkvh, kaug, vt, wts["diff_lambda"], wts["diff_g"], n_seq, t, blk, lam_init)
    else:
        o = _decode_diff_attention(qkvh, past[3], past[4], wts["diff_lambda"], wts["diff_g"], n_seq, t,
                                   min(DECODE_BLK, past[3].shape[1]), lam_init)
    x2d = _mixer_out_ffn(x2d, o, wts["diff_wout"], wts["ln_g"][1][0], wts["ln_b"][1][0],
                         wts["ffn_win"][1], wts["ffn_wout"][1], wts["ln_g"][1][1], wts["ln_b"][1][1])

    def untranspose(a):
        a = a.reshape(groups, FOX_HEADS, FOX_HD, n_seq // groups, t)
        return jnp.transpose(a, (0, 3, 4, 1, 2)).reshape(1, n_seq, t, FOX_HEADS, FOX_HD)

    return (x2d.reshape(n_seq, t, D_MODEL),
            untranspose(fkt), untranspose(fvt),
            logf.reshape(1, n_seq, t, FOX_HEADS),
            dk.reshape(1, n_seq, t, DIFF_HEADS, 2 * DIFF_HD), dv.reshape(1, n_seq, t, DIFF_HEADS, 2 * DIFF_HD))


def _prepare_weights(fox_w_in, fox_b_f, fox_w_out, diff_w_in, diff_lambda, diff_subln_g, diff_w_out,
                     ffn_w_in, ffn_w_out, ln_g, ln_b):
    wf = fox_w_in[0][:, 3 * D_MODEL:].astype(BF16)
    wf_pad = jnp.pad(wf, ((0, 0), (0, LANES - FOX_HEADS)))
    bf_pad = jnp.pad(fox_b_f[0].reshape(1, FOX_HEADS), ((0, 0), (0, LANES - FOX_HEADS)))
    return {
        "fox_wqkv": fox_w_in[0][:, :3 * D_MODEL].astype(BF16),
        "fox_wvt": fox_w_in[0][:, 2 * D_MODEL:3 * D_MODEL].T.astype(BF16),
        "fox_forget": (fox_w_in[0][:, D_MODEL:2 * D_MODEL].T.astype(BF16),
                       wf_pad, wf.T, bf_pad, fox_b_f[0].reshape(FOX_HEADS, 1)),
        "fox_wout": fox_w_out[0].astype(BF16),
        "diff_wqkv": diff_w_in[0].astype(BF16),
        "diff_wvt": diff_w_in[0][:, 2 * D_MODEL:].T.astype(BF16),
        "diff_lambda": diff_lambda[0],
        "diff_g": diff_subln_g[0].reshape(1, 2 * DIFF_HD),
        "diff_wout": diff_w_out[0].astype(BF16),
        "ffn_win": [ffn_w_in[i].astype(BF16) for i in range(DEPTH)],
        "ffn_wout": [ffn_w_out[i].astype(BF16) for i in range(DEPTH)],
        "ln_g": [[ln_g[i, j].reshape(1, D_MODEL) for j in range(2)] for i in range(DEPTH)],
        "ln_b": [[ln_b[i, j].reshape(1, D_MODEL) for j in range(2)] for i in range(DEPTH)],
    }


def kernel(x_prompt, x_sample, cache_fox_k, cache_fox_v, cache_fox_logf, cache_diff_k, cache_diff_v, fox_w_in, fox_b_f, fox_w_out, diff_w_in, diff_lambda, diff_subln_g, diff_w_out, ffn_w_in, ffn_w_out, ln_g, ln_b):
    wts = _prepare_weights(fox_w_in, fox_b_f, fox_w_out, diff_w_in, diff_lambda, diff_subln_g, diff_w_out,
                           ffn_w_in, ffn_w_out, ln_g, ln_b)
    y_p, fk_p, fv_p, lf_p, dk_p, dv_p = _run_stream(x_prompt, None, wts)
    past = (cache_fox_k[0], cache_fox_v[0], cache_fox_logf[0], cache_diff_k[0], cache_diff_v[0])
    y_s, fk_s, fv_s, lf_s, dk_s, dv_s = _run_stream(x_sample, past, wts)
    return (y_p, y_s, fk_p, fv_p, lf_p, dk_p, dv_p, fk_s, fv_s, lf_s, dk_s, dv_s)
```

```python
import functools
import math

import jax
import jax.numpy as jnp
import numpy as np
from jax import lax
from jax.experimental import pallas as pl
from jax.experimental.pallas import tpu as pltpu

F32 = jnp.float32
BF16 = jnp.bfloat16

D_MODEL = 1024
DEPTH = 2
FOX_HEADS = 16
FOX_HD = 64
DIFF_HEADS = 8
DIFF_HD = 64
D_FF = 2816
CHUNK = 64
DEEPNORM_ALPHA = (2.0 * DEPTH) ** 0.25
LN_EPS = 1e-5
NEG_INF = -1e30
LOG2E = 1.4426950408889634

LANES = 128
HEAD_PAIRS = D_MODEL // LANES
ATTN_BLK = 512
ONES_ROWS = 16
DECODE_BLK = 1024
VMEM_LIMIT = 56 * 1024 * 1024


def _cparams(n_axes):
    return pltpu.CompilerParams(dimension_semantics=("arbitrary",) * n_axes,
                                vmem_limit_bytes=VMEM_LIMIT)


def _log_sigmoid(z):
    return jnp.minimum(z, 0.0) - jnp.log1p(jnp.exp(-jnp.abs(z)))


def _layernorm(z, g, b):
    mu = jnp.mean(z, axis=-1, keepdims=True)
    zc = z - mu
    var = jnp.mean(zc * zc, axis=-1, keepdims=True)
    return zc * lax.rsqrt(var + LN_EPS) * g + b


def _dot(a, b):
    return jnp.dot(a, b, preferred_element_type=F32)


def _dot_nt(a, b):
    return lax.dot_general(a, b, (((1,), (1,)), ((), ())), preferred_element_type=F32)


def _vt_rows(group):
    return (LANES // group) * (group + ONES_ROWS)


def _split3(x):
    hi = x.astype(BF16).astype(F32)
    r = x - hi
    mid = r.astype(BF16).astype(F32)
    lo = (r - mid).astype(BF16).astype(F32)
    return hi, mid, lo


def _alibi_columns(rows):
    slope = np.asarray([2.0 ** -(h + 1) * LOG2E for h in range(DIFF_HEADS)], np.float32)
    x = np.arange(rows, dtype=np.float32)[None, :] * slope[:, None]
    cols = np.zeros((DIFF_HEADS, rows, LANES), np.float32)
    for term in range(3):
        part = x.astype(BF16).astype(np.float32)
        cols[:, :, term] = part
        x = x - part
    return jnp.asarray(cols.astype(BF16))


def _proj_body(*refs, q_scale, with_forget):
    if with_forget:
        (x_ref, w_ref, wvt_ref, wkt_ref, wf_ref, wft_ref, bf_ref, bft_ref,
         qkv_ref, k_ref, v_ref, vt_ref, lf_ref, lfp_ref, lft_ref) = refs
    else:
        x_ref, w_ref, wvt_ref, pos_ref, qkv_ref, k_ref, v_ref, vt_ref, ka_ref = refs
    tm = x_ref.shape[0]
    xb = x_ref[...].astype(BF16)

    def project(part):
        return _dot(xb, w_ref[:, part * D_MODEL:(part + 1) * D_MODEL])

    def store_heads(part, val):
        vb = val.astype(BF16)
        for p in range(HEAD_PAIRS):
            qkv_ref[part, p] = vb[:, p * LANES:(p + 1) * LANES]
        return vb

    store_heads(0, project(0) * q_scale)
    k = project(1)
    kb = store_heads(1, k)
    vt32 = _dot_nt(wvt_ref[...], xb)
    if with_forget:
        lfp = _log_sigmoid(_dot(xb, wf_ref[...]) + bf_ref[...])
        lfp_ref[...] = lfp
        lf_ref[...] = lfp[:, :FOX_HEADS]
        lft_ref[...] = _log_sigmoid(_dot_nt(wft_ref[...], xb) + bft_ref[...])
        k_ref[0] = _dot_nt(wkt_ref[...], xb)
        v_ref[0] = vt32
    else:
        k_ref[...] = k
        for h in range(DIFF_HEADS):
            ka_ref[h, :, 0:LANES] = kb[:, h * LANES:(h + 1) * LANES]
            ka_ref[h, :, LANES:2 * LANES] = pos_ref[h]
        v = project(2)
        store_heads(2, v)
        v_ref[...] = v
    vt = vt32.astype(BF16)
    ones = jnp.ones((ONES_ROWS, tm), BF16)
    group = FOX_HD if with_forget else LANES
    for p in range(HEAD_PAIRS):
        for gi in range(LANES // group):
            src = p * LANES + gi * group
            dst = gi * (group + ONES_ROWS)
            vt_ref[p, 0, dst:dst + group, :] = vt[src:src + group, :]
            vt_ref[p, 0, dst + group:dst + group + ONES_ROWS, :] = ones


def _qkv_proj(x2d, n_seq, w_qkv, w_vt, q_scale, forget=None):
    m = x2d.shape[0]
    tm = min(ATTN_BLK, m)
    nblk = m // tm
    group = FOX_HD if forget is not None else LANES
    vt_rows = _vt_rows(group)
    row = pl.BlockSpec((tm, D_MODEL), lambda i: (i, 0))
    const = lambda shape: pl.BlockSpec(shape, lambda i: (0,) * len(shape))
    if forget is not None:
        n_parts = 2
        per_seq = nblk // n_seq
        kv_shape = jax.ShapeDtypeStruct((n_seq, D_MODEL, m // n_seq), F32)
        kv_spec = pl.BlockSpec((1, D_MODEL, tm), lambda i: (i // per_seq, 0, i % per_seq))
    else:
        n_parts = 3
        kv_shape = jax.ShapeDtypeStruct((m, D_MODEL), F32)
        kv_spec = row
    in_specs = [row, const((D_MODEL, n_parts * D_MODEL)), const((D_MODEL, D_MODEL))]
    out_shape = [jax.ShapeDtypeStruct((n_parts, HEAD_PAIRS, m, LANES), BF16),
                 kv_shape, kv_shape,
                 jax.ShapeDtypeStruct((HEAD_PAIRS, nblk, vt_rows, tm), BF16)]
    out_specs = [pl.BlockSpec((n_parts, HEAD_PAIRS, tm, LANES), lambda i: (0, 0, i, 0)),
                 kv_spec, kv_spec,
                 pl.BlockSpec((HEAD_PAIRS, 1, vt_rows, tm), lambda i: (0, i, 0, 0))]
    args = [x2d, w_qkv[:, :n_parts * D_MODEL], w_vt]
    if forget is not None:
        w_kt, wf, wft, bf, bft = forget
        in_specs.append(const((D_MODEL, D_MODEL)))
        args.append(w_kt)
        in_specs += [const((D_MODEL, LANES)), const((FOX_HEADS, D_MODEL)), const((1, LANES)), const((FOX_HEADS, 1))]
        out_shape += [jax.ShapeDtypeStruct((m, FOX_HEADS), F32),
                      jax.ShapeDtypeStruct((m, LANES), F32),
                      jax.ShapeDtypeStruct((FOX_HEADS, m), F32)]
        out_specs += [pl.BlockSpec((tm, FOX_HEADS), lambda i: (i, 0)),
                      pl.BlockSpec((tm, LANES), lambda i: (i, 0)),
                      pl.BlockSpec((FOX_HEADS, tm), lambda i: (0, i))]
        args += [wf, wft, bf, bft]
    else:
        in_specs.append(const((DIFF_HEADS, tm, LANES)))
        args.append(_alibi_columns(tm))
        out_shape.append(jax.ShapeDtypeStruct((DIFF_HEADS, m, 2 * LANES), BF16))
        out_specs.append(pl.BlockSpec((DIFF_HEADS, tm, 2 * LANES), lambda i: (0, i, 0)))
    return pl.pallas_call(
        functools.partial(_proj_body, q_scale=q_scale, with_forget=forget is not None),
        grid=(nblk,), in_specs=in_specs, out_specs=out_specs, out_shape=out_shape,
        compiler_params=_cparams(1), name="qkv_proj_f" if forget is not None else "qkv_proj",
    )(*args)


def _fox_keys_body(lf_ref, k_ref, ka_ref, fs_ref, carry_ref, tri_ref, pick_ref, *, blk):
    i = pl.program_id(1)

    @pl.when((pl.program_id(0) == 0) & (i == 0))
    def _():
        row = lax.broadcasted_iota(jnp.int32, (blk, blk), 0)
        col = lax.broadcasted_iota(jnp.int32, (blk, blk), 1)
        tri_ref[...] = (col <= row).astype(BF16)
        sr = lax.broadcasted_iota(jnp.int32, (LANES, D_MODEL), 0)
        sc = lax.broadcasted_iota(jnp.int32, (LANES, D_MODEL), 1)
        pair, lane = sc // LANES, sc % LANES
        for term in range(3):
            pick_ref[term] = (((sr == 2 * pair) & (lane == term))
                              | ((sr == 2 * pair + 1) & (lane == 3 + term))).astype(BF16)

    @pl.when(i == 0)
    def _():
        carry_ref[...] = jnp.zeros_like(carry_ref)

    tri = tri_ref[...]
    hi, mid, lo = _split3(lf_ref[...])
    f_loc = _dot(tri, hi.astype(BF16)) + _dot(tri, mid.astype(BF16)) + _dot(tri, lo.astype(BF16))
    fs_ref[0] = carry_ref[...]
    carry_ref[...] = carry_ref[...] + f_loc[blk - 1:blk, :]

    terms = _split3(f_loc * (-LOG2E))
    cols = sum(_dot(terms[t].astype(BF16), pick_ref[t]) for t in range(3)).astype(BF16)
    for p in range(HEAD_PAIRS):
        ka_ref[p, :, 0:LANES] = k_ref[0, p]
        ka_ref[p, :, LANES:2 * LANES] = cols[:, p * LANES:(p + 1) * LANES]


def _fox_keys(lf_pad, qkvh, n_seq, t, blk):
    m = n_seq * t
    nblk = t // blk
    return pl.pallas_call(
        functools.partial(_fox_keys_body, blk=blk),
        grid=(n_seq, nblk),
        in_specs=[pl.BlockSpec((blk, LANES), lambda b, i: (b * nblk + i, 0)),
                  pl.BlockSpec((1, HEAD_PAIRS, blk, LANES), lambda b, i: (1, 0, b * nblk + i, 0))],
        out_specs=[pl.BlockSpec((HEAD_PAIRS, blk, 2 * LANES), lambda b, i: (0, b * nblk + i, 0)),
                   pl.BlockSpec((1, 1, LANES), lambda b, i: (b * nblk + i, 0, 0))],
        out_shape=[jax.ShapeDtypeStruct((HEAD_PAIRS, m, 2 * LANES), BF16),
                   jax.ShapeDtypeStruct((n_seq * nblk, 1, LANES), F32)],
        scratch_shapes=[pltpu.VMEM((1, LANES), F32),
                        pltpu.VMEM((blk, blk), BF16),
                        pltpu.VMEM((3, LANES, D_MODEL), BF16)],
        compiler_params=_cparams(2), name="fox_keys",
    )(lf_pad, qkvh)


def _cumsum_body(x_ref, o_ref, carry_ref, tri_ref, *, blk):
    @pl.when(pl.program_id(0) == 0)
    def _():
        row = lax.broadcasted_iota(jnp.int32, (blk, blk), 0)
        col = lax.broadcasted_iota(jnp.int32, (blk, blk), 1)
        tri_ref[...] = (row <= col).astype(BF16)
        carry_ref[...] = jnp.zeros_like(carry_ref)

    hi, mid, lo = _split3(x_ref[...])
    tri = tri_ref[...]
    cs = _dot(hi.astype(BF16), tri) + _dot(mid.astype(BF16), tri) + _dot(lo.astype(BF16), tri) + carry_ref[...]
    o_ref[0] = cs
    carry_ref[...] = cs[:, blk - 1:blk]


def _cumsum_time(lf, blk):
    rows, r = lf.shape
    nblk = r // blk
    return pl.pallas_call(
        functools.partial(_cumsum_body, blk=blk),
        grid=(nblk,),
        in_specs=[pl.BlockSpec((rows, blk), lambda i: (0, i))],
        out_specs=pl.BlockSpec((1, rows, blk), lambda i: (i, 0, 0)),
        out_shape=jax.ShapeDtypeStruct((nblk, rows, blk), F32),
        scratch_shapes=[pltpu.VMEM((rows, 1), F32), pltpu.VMEM((blk, blk), BF16)],
        compiler_params=_cparams(1), name="cumsum_time",
    )(lf)


def _augment_queries(q, blk, ones_first_lane):
    lane = lax.broadcasted_iota(jnp.int32, (blk, LANES), 1)
    lo_mask = lane < LANES // 2
    zero = jnp.zeros_like(q)
    out = []
    for u, part in enumerate((jnp.where(lo_mask, q, zero), jnp.where(lo_mask, zero, q))):
        first = ones_first_lane[u]
        ones = jnp.where((lane >= first) & (lane < first + 3), 1.0, 0.0).astype(BF16)
        out.append(jnp.concatenate([part, ones], axis=1))
    return out


def _softmax_step(s, bmax, c, vt, m_ref, l_ref, u):
    group = vt.shape[0] - ONES_ROWS
    m_old = m_ref[u]
    m_new = jnp.maximum(m_old, bmax + c)
    p = jnp.exp2(s - (m_new - c))
    alpha = jnp.exp2(m_old - m_new)
    pv = _dot(vt, p.astype(BF16))
    l_ref[u] = alpha * l_ref[u] + pv[group:group + 1, :]
    m_ref[u] = m_new
    return alpha, pv[:group, :]


def _sweep_key_tiles(qi, produce, consume):
    @pl.when(qi == 0)
    def _():
        produce(0, 0, True)

    @pl.when(qi > 0)
    def _():
        n_pairs = lax.shift_right_logical(qi - 1, 1)
        odd = (qi - 1) & 1

        @pl.when(odd == 1)
        def _():
            produce(0, 0, False)
            produce(1, 1, False)
            consume(0, 0)

        @pl.when(odd == 0)
        def _():
            produce(0, 1, False)

        def two_tiles(j):
            produce(j + 1, 0, False)
            consume(j, 1)
            produce(j + 2, 1, False)
            consume(j + 1, 0)

        def body2(i, carry):
            two_tiles(odd + 2 * i)
            return carry

        n_quads = lax.shift_right_logical(n_pairs, 1)

        def body4(i, carry):
            j = odd + 2 * (n_pairs & 1)
            two_tiles(j)
            two_tiles(j + 2)
            return carry

        def body8(i, carry):
            j = odd + 2 * (n_pairs & 1) + 4 * (n_quads & 1) + 8 * i
            for step in range(4):
                two_tiles(j + 2 * step)
            return carry

        lax.fori_loop(0, n_pairs & 1, body2, 0)
        lax.fori_loop(0, n_quads & 1, body4, 0)
        lax.fori_loop(0, lax.shift_right_logical(n_quads, 1), body8, 0)
        produce(qi, 0, True)
        consume(qi - 1, 1)

    consume(qi, 0)


def _fox_attn_body(fs_ref, q_ref, ka_ref, vt_ref, o_ref, acc_ref, m_ref, l_ref, s_ref, bm_ref, *, blk, nq):
    seq = pl.program_id(0)
    pair = pl.program_id(1)
    qi = pl.program_id(2)
    qa = _augment_queries(q_ref[0, 0], blk, (0, 3))
    acc_ref[...] = jnp.zeros_like(acc_ref)
    m_ref[...] = jnp.full_like(m_ref, NEG_INF)
    l_ref[...] = jnp.zeros_like(l_ref)
    half = FOX_HD

    def produce(kj, slot, diagonal):
        start = pl.multiple_of(kj * blk, blk)
        ka = ka_ref[0, pl.ds(start, blk), :]
        if diagonal:
            key = lax.broadcasted_iota(jnp.int32, (blk, blk), 0)
            query = lax.broadcasted_iota(jnp.int32, (blk, blk), 1)
            causal = key <= query
        for h in range(2):
            s = _dot_nt(ka, qa[h])
            if diagonal:
                s = jnp.where(causal, s, NEG_INF)
            s_ref[slot, h] = s
            bm_ref[slot, h] = jnp.max(s, axis=0, keepdims=True)

    def consume(kj, slot):
        vt = vt_ref[0, kj]
        for h in range(2):
            head = 2 * pair + h
            f_q = jnp.full((1, blk), fs_ref[(seq * nq + qi) * FOX_HEADS + head], F32)
            f_k = jnp.full((1, blk), fs_ref[(seq * nq + kj) * FOX_HEADS + head], F32)
            c = (f_q - f_k) * LOG2E
            vt_h = vt[h * (half + ONES_ROWS):(h + 1) * (half + ONES_ROWS), :]
            alpha, pv = _softmax_step(s_ref[slot, h], bm_ref[slot, h], c, vt_h, m_ref, l_ref, h)
            rows = slice(h * half, (h + 1) * half)
            acc_ref[rows, :] = alpha * acc_ref[rows, :] + pv

    _sweep_key_tiles(qi, produce, consume)
    sub = lax.broadcasted_iota(jnp.int32, (LANES, blk), 0)
    o_t = acc_ref[...] / jnp.where(sub < half, l_ref[0], l_ref[1])
    o_ref[...] = o_t.T.astype(BF16)


def _fox_attention(qkvh, kaug, vt, f_start, n_seq, t, blk):
    m = n_seq * t
    nq = t // blk
    grid_spec = pltpu.PrefetchScalarGridSpec(
        num_scalar_prefetch=1,
        grid=(n_seq, HEAD_PAIRS, nq),
        in_specs=[pl.BlockSpec((1, 1, blk, LANES), lambda b, p, i, fs: (0, p, b * nq + i, 0)),
                  pl.BlockSpec((1, t, 2 * LANES), lambda b, p, i, fs: (p, b, 0)),
                  pl.BlockSpec((1, nq, _vt_rows(FOX_HD), blk), lambda b, p, i, fs: (p, b, 0, 0))],
        out_specs=pl.BlockSpec((blk, LANES), lambda b, p, i, fs: (b * nq + i, p)),
        scratch_shapes=[pltpu.VMEM((LANES, blk), F32),
                        pltpu.VMEM((2, 1, blk), F32),
                        pltpu.VMEM((2, 1, blk), F32),
                        pltpu.VMEM((2, 2, blk, blk), F32),
                        pltpu.VMEM((2, 2, 1, blk), F32)])
    return pl.pallas_call(
        functools.partial(_fox_attn_body, blk=blk, nq=nq),
        grid_spec=grid_spec,
        out_shape=jax.ShapeDtypeStruct((m, D_MODEL), BF16),
        compiler_params=_cparams(3), name="fox_attention",
    )(f_start, qkvh, kaug, vt)


def _diff_lambda(lam_ref, lam_init):
    lp = lam_ref[...]
    a = jnp.sum(lp[0:1] * lp[1:2], axis=1, keepdims=True)
    b = jnp.sum(lp[2:3] * lp[3:4], axis=1, keepdims=True)
    return jnp.exp(a) - jnp.exp(b) + lam_init


def _subln(o, g, lam_init):
    o = o * lax.rsqrt(jnp.mean(o * o, axis=-1, keepdims=True) + LN_EPS)
    return o * g * (1.0 - lam_init)


def _diff_attn_body(q_ref, ka_ref, vt_ref, lam_ref, g_ref, o_ref, acc_ref, m_ref, l_ref, s_ref, bm_ref, *,
                    blk, lam_init):
    head = pl.program_id(1)
    qi = pl.program_id(2)
    qa = _augment_queries(q_ref[0, 0], blk, (0, 0))
    acc_ref[...] = jnp.zeros_like(acc_ref)
    m_ref[...] = jnp.full_like(m_ref, NEG_INF)
    l_ref[...] = jnp.zeros_like(l_ref)
    slope = lax.bitcast_convert_type(jnp.full((1, blk), 126 - head, jnp.int32) << 23, F32) * LOG2E

    def produce(kj, slot, diagonal):
        start = pl.multiple_of(kj * blk, blk)
        ka = ka_ref[0, pl.ds(start, blk), :]
        if diagonal:
            key = lax.broadcasted_iota(jnp.int32, (blk, blk), 0)
            query = lax.broadcasted_iota(jnp.int32, (blk, blk), 1)
            visible = (key // CHUNK) <= (query // CHUNK)
            ahead = slope * (-2.0 * jnp.maximum(key - query, 0).astype(F32))
        for u in range(2):
            s = _dot_nt(ka, qa[u])
            if diagonal:
                s = jnp.where(visible, s + ahead, NEG_INF)
            s_ref[slot, u] = s
            bm_ref[slot, u] = jnp.max(s, axis=0, keepdims=True)

    def consume(kj, slot):
        vt = vt_ref[0, kj]
        c = slope * jnp.full((1, blk), (kj - qi) * blk, jnp.int32).astype(F32)
        for u in range(2):
            alpha, pv = _softmax_step(s_ref[slot, u], bm_ref[slot, u], c, vt, m_ref, l_ref, u)
            acc_ref[u] = alpha * acc_ref[u] + pv

    _sweep_key_tiles(qi, produce, consume)
    lam = _diff_lambda(lam_ref, lam_init)
    o_t = acc_ref[0] / l_ref[0] - lam * (acc_ref[1] / l_ref[1])
    o_ref[...] = _subln(o_t.T, g_ref[...], lam_init).astype(BF16)


def _diff_attention(qkvh, kaug, vt, lam_params, subln_g, n_seq, t, blk, lam_init):
    m = n_seq * t
    nq = t // blk
    return pl.pallas_call(
        functools.partial(_diff_attn_body, blk=blk, lam_init=lam_init),
        grid=(n_seq, DIFF_HEADS, nq),
        in_specs=[pl.BlockSpec((1, 1, blk, LANES), lambda b, p, i: (0, p, b * nq + i, 0)),
                  pl.BlockSpec((1, t, 2 * LANES), lambda b, p, i: (p, b, 0)),
                  pl.BlockSpec((1, nq, _vt_rows(LANES), blk), lambda b, p, i: (p, b, 0, 0)),
                  pl.BlockSpec((4, DIFF_HD), lambda b, p, i: (0, 0)),
                  pl.BlockSpec((1, LANES), lambda b, p, i: (0, 0))],
        out_specs=pl.BlockSpec((blk, LANES), lambda b, p, i: (b * nq + i, p)),
        out_shape=jax.ShapeDtypeStruct((m, D_MODEL), BF16),
        scratch_shapes=[pltpu.VMEM((2, LANES, blk), F32),
                        pltpu.VMEM((2, 1, blk), F32),
                        pltpu.VMEM((2, 1, blk), F32),
                        pltpu.VMEM((2, 2, blk, blk), F32),
                        pltpu.VMEM((2, 2, 1, blk), F32)],
        compiler_params=_cparams(3), name="diff_attention",
    )(qkvh, kaug, vt, lam_params, subln_g)


def _decode_diff_body(q_ref, kn_ref, vn_ref, kc_ref, vc_ref, lam_ref, g_ref, o_ref, acc_ref, m_ref, l_ref, *,
                      t_new, past_len, blk, nk, lam_init):
    kj = pl.program_id(1)
    rows = 2 * t_new
    lane = lax.broadcasted_iota(jnp.int32, (t_new, LANES), 1)
    lo_mask = lane < DIFF_HD

    @pl.when(kj == 0)
    def _():
        acc_ref[...] = jnp.zeros_like(acc_ref)
        m_ref[...] = jnp.full_like(m_ref, NEG_INF)
        l_ref[...] = jnp.zeros_like(l_ref)

    def attend(keys, values, bias, visible):
        heads = range(DIFF_HEADS)
        scores = []
        for h in heads:
            q = q_ref[0, h]
            zero = jnp.zeros_like(q)
            q_maps = jnp.concatenate([jnp.where(lo_mask, q, zero), jnp.where(lo_mask, zero, q)], axis=0)
            s = _dot_nt(q_maps, keys(h)) + bias(h)
            scores.append(s if visible is None else jnp.where(visible, s, NEG_INF))
        m_old = [m_ref[h] for h in heads]
        m_new = [jnp.maximum(m_old[h], jnp.max(scores[h], axis=1, keepdims=True)) for h in heads]
        probs = [jnp.exp2(scores[h] - m_new[h]) for h in heads]
        alpha = [jnp.exp2(m_old[h] - m_new[h]) for h in heads]
        for h in heads:
            l_ref[h] = alpha[h] * l_ref[h] + jnp.sum(probs[h], axis=1, keepdims=True)
            m_ref[h] = m_new[h]
            acc_ref[h] = alpha[h] * acc_ref[h] + _dot(probs[h].astype(BF16), values(h))

    def slope(h):
        return 2.0 ** -(h + 1) * LOG2E

    col = lax.broadcasted_iota(jnp.int32, (1, blk), 1)
    dist = (col + (kj * blk - past_len)).astype(F32)
    attend(lambda h: kc_ref[0, pl.ds(h, blk, stride=DIFF_HEADS), :].astype(BF16),
           lambda h: vc_ref[0, pl.ds(h, blk, stride=DIFF_HEADS), :].astype(BF16),
           lambda h: slope(h) * dist, None)

    @pl.when(kj == nk - 1)
    def _():
        tq = lax.broadcasted_iota(jnp.int32, (rows, t_new), 0) % t_new
        tk = lax.broadcasted_iota(jnp.int32, (rows, t_new), 1)
        near = (tq - jnp.abs(tq - tk)).astype(F32)
        visible = ((tk + past_len) // CHUNK) <= ((tq + past_len) // CHUNK)
        attend(lambda h: kn_ref[0, h], lambda h: vn_ref[0, h], lambda h: slope(h) * near, visible)
        lam = _diff_lambda(lam_ref, lam_init)
        g = g_ref[...]
        outs = []
        for h in range(DIFF_HEADS):
            a = acc_ref[h] / l_ref[h]
            outs.append(_subln(a[:t_new] - lam * a[t_new:], g, lam_init))
        o_ref[0] = jnp.concatenate(outs, axis=1).astype(BF16)


def _decode_diff_attention(qkvh, cache_k, cache_v, lam_params, subln_g, n_seq, t_new, blk, lam_init):
    past_len, heads, hd = cache_k.shape[1:]
    nk = past_len // blk
    rows = 2 * t_new
    new_spec = lambda s: pl.BlockSpec((1, DIFF_HEADS, t_new, LANES), lambda b, j: (s, 0, b, 0))
    cache_k = cache_k.reshape(n_seq, past_len * heads, hd)
    cache_v = cache_v.reshape(n_seq, past_len * heads, hd)
    cache_spec = pl.BlockSpec((1, blk * heads, hd), lambda b, j: (b, j, 0))
    out = pl.pallas_call(
        functools.partial(_decode_diff_body, t_new=t_new, past_len=past_len, blk=blk, nk=nk, lam_init=lam_init),
        grid=(n_seq, nk),
        in_specs=[new_spec(0), new_spec(1), new_spec(2), cache_spec, cache_spec,
                  pl.BlockSpec((4, DIFF_HD), lambda b, j: (0, 0)),
                  pl.BlockSpec((1, LANES), lambda b, j: (0, 0))],
        out_specs=pl.BlockSpec((1, t_new, D_MODEL), lambda b, j: (b, 0, 0)),
        out_shape=jax.ShapeDtypeStruct((n_seq, t_new, D_MODEL), BF16),
        scratch_shapes=[pltpu.VMEM((DIFF_HEADS, rows, LANES), F32),
                        pltpu.VMEM((DIFF_HEADS, rows, 1), F32),
                        pltpu.VMEM((DIFF_HEADS, rows, 1), F32)],
        compiler_params=_cparams(2), name="decode_attention_diff",
    )(qkvh, qkvh, qkvh, cache_k, cache_v, lam_params, subln_g)
    return out.reshape(n_seq * t_new, D_MODEL)


def _decode_fox_body(q_ref, knt_ref, vnt_ref, kt_ref, vt_ref, f_ref, o_ref, acc_ref, m_ref, l_ref, *,
                     t_new, blk, nk):
    kj = pl.program_id(1)
    lane = lax.broadcasted_iota(jnp.int32, (t_new, LANES), 1)
    lo_mask = lane < FOX_HD

    @pl.when(kj == 0)
    def _():
        acc_ref[...] = jnp.zeros_like(acc_ref)
        m_ref[...] = jnp.full_like(m_ref, NEG_INF)
        l_ref[...] = jnp.zeros_like(l_ref)

    f_last = f_ref[nk - 1][:, blk - 1:blk]

    def attend(kt_pairs, vt_pairs, bias, visible):
        heads = range(FOX_HEADS)
        scores = []
        for p in range(HEAD_PAIRS):
            q = q_ref[0, p]
            zero = jnp.zeros_like(q)
            kt = kt_pairs(p)
            for qh in (jnp.where(lo_mask, q, zero), jnp.where(lo_mask, zero, q)):
                h = len(scores)
                s = _dot(qh, kt) + bias[h:h + 1, :]
                scores.append(s if visible is None else jnp.where(visible, s, NEG_INF))
        m_old = [m_ref[h] for h in heads]
        m_new = [jnp.maximum(m_old[h], jnp.max(scores[h], axis=1, keepdims=True)) for h in heads]
        probs = [jnp.exp2(scores[h] - m_new[h]) for h in heads]
        alpha = [jnp.exp2(m_old[h] - m_new[h]) for h in heads]
        for h in heads:
            l_ref[h] = alpha[h] * l_ref[h] + jnp.sum(probs[h], axis=1, keepdims=True)
            m_ref[h] = m_new[h]
        for p in range(HEAD_PAIRS):
            vt = vt_pairs(p)
            pv = [_dot_nt(probs[2 * p + e].astype(BF16), vt) for e in range(2)]
            acc = acc_ref[p]
            acc_ref[p] = jnp.where(lo_mask, alpha[2 * p] * acc + pv[0], alpha[2 * p + 1] * acc + pv[1])

    attend(lambda p: kt_ref[0, p].astype(BF16), lambda p: vt_ref[0, p].astype(BF16),
           (f_last - f_ref[kj]) * LOG2E, None)

    @pl.when(kj == nk - 1)
    def _():
        tq = lax.broadcasted_iota(jnp.int32, (t_new, t_new), 0)
        tk = lax.broadcasted_iota(jnp.int32, (t_new, t_new), 1)
        attend(lambda p: knt_ref[0, p * LANES:(p + 1) * LANES, :].astype(BF16),
               lambda p: vnt_ref[0, p * LANES:(p + 1) * LANES, :].astype(BF16),
               (f_last - f_ref[nk][:, :t_new]) * LOG2E, tk <= tq)
        o_ref[0] = jnp.concatenate(
            [acc_ref[p] / jnp.where(lo_mask, l_ref[2 * p], l_ref[2 * p + 1]) for p in range(HEAD_PAIRS)],
            axis=1).astype(BF16)


def _decode_fox_attention(qkvh, knt, vnt, cache_kt, cache_vt, fcum, n_seq, t_new, blk):
    past_len = cache_kt.shape[3]
    nk = past_len // blk
    cache_spec = pl.BlockSpec((1, HEAD_PAIRS, LANES, blk), lambda b, j: (b, 0, 0, j))
    new_spec = pl.BlockSpec((1, D_MODEL, t_new), lambda b, j: (b, 0, 0))
    out = pl.pallas_call(
        functools.partial(_decode_fox_body, t_new=t_new, blk=blk, nk=nk),
        grid=(n_seq, nk),
        in_specs=[pl.BlockSpec((1, HEAD_PAIRS, t_new, LANES), lambda b, j: (0, 0, b, 0)),
                  new_spec, new_spec, cache_spec, cache_spec,
                  pl.BlockSpec((nk + 1, FOX_HEADS, blk), lambda b, j: (0, b, 0))],
        out_specs=pl.BlockSpec((1, t_new, D_MODEL), lambda b, j: (b, 0, 0)),
        out_shape=jax.ShapeDtypeStruct((n_seq, t_new, D_MODEL), BF16),
        scratch_shapes=[pltpu.VMEM((HEAD_PAIRS, t_new, LANES), F32),
                        pltpu.VMEM((FOX_HEADS, t_new, 1), F32),
                        pltpu.VMEM((FOX_HEADS, t_new, 1), F32)],
        compiler_params=_cparams(2), name="decode_attention_fox",
    )(qkvh, knt, vnt, cache_kt, cache_vt, fcum)
    return out.reshape(n_seq * t_new, D_MODEL)


def _mixer_out_ffn_body(x_ref, o_ref, wo_ref, g1_ref, b1_ref, win_ref, wout_ref, g2_ref, b2_ref, y_ref):
    half = x_ref.shape[0] // 2
    halves = [slice(0, half), slice(half, 2 * half)]
    mix = [_dot(o_ref[r, :], wo_ref[...]) for r in halves]
    x = [_layernorm(DEEPNORM_ALPHA * x_ref[r, :] + mix[i], g1_ref[...], b1_ref[...]) for i, r in enumerate(halves)]
    h = [_dot(xi.astype(BF16), win_ref[...]) for xi in x]
    act = [(hi[:, :D_FF] * jax.nn.sigmoid(hi[:, :D_FF]) * hi[:, D_FF:]).astype(BF16) for hi in h]
    y = [_dot(a, wout_ref[...]) for a in act]
    for i, r in enumerate(halves):
        y_ref[r, :] = _layernorm(DEEPNORM_ALPHA * x[i] + y[i], g2_ref[...], b2_ref[...])


def _mixer_out_ffn(x2d, o2d, w_o, g1, b1, w_in, w_out, g2, b2, tm=512):
    m = x2d.shape[0]
    tm = min(tm, m)
    row = pl.BlockSpec((tm, D_MODEL), lambda i: (i, 0))
    vec = pl.BlockSpec((1, D_MODEL), lambda i: (0, 0))
    resident = lambda shape: pl.BlockSpec(shape, lambda i: (0, 0), pipeline_mode=pl.Buffered(1))
    return pl.pallas_call(
        _mixer_out_ffn_body, grid=(m // tm,),
        in_specs=[row, row, resident((D_MODEL, D_MODEL)), vec, vec,
                  resident((D_MODEL, 2 * D_FF)),
                  resident((D_FF, D_MODEL)),
                  vec, vec],
        out_specs=row, out_shape=jax.ShapeDtypeStruct((m, D_MODEL), F32),
        compiler_params=_cparams(1), name="mixer_out_ffn",
    )(x2d, o2d, w_o, g1, b1, w_in, w_out, g2, b2)


def _run_stream(x, past, wts):
    n_seq, t, _ = x.shape
    m = n_seq * t
    blk = ATTN_BLK
    x2d = x.reshape(m, D_MODEL)
    lam_init = 0.8 - 0.6 * math.exp(-0.3 * 1)

    groups = n_seq if t >= blk else 1
    qkvh, fkt, fvt, vt, logf, logf_pad, logft = _qkv_proj(
        x2d, groups, wts["fox_wqkv"], wts["fox_wvt"], FOX_HD ** -0.5 * LOG2E, forget=wts["fox_forget"])
    if past is None:
        kaug, f_start = _fox_keys(logf_pad, qkvh, n_seq, t, blk)
        o = _fox_attention(qkvh, kaug, vt, f_start[:, 0, :FOX_HEADS].reshape(-1), n_seq, t, blk)
    else:
        past_k, past_v, past_lf = past[0], past[1], past[2]
        past_len = past_k.shape[1]
        dblk = min(DECODE_BLK, past_len)
        lf_all = jnp.concatenate(
            [jnp.transpose(past_lf, (0, 2, 1)),
             jnp.transpose(logft.reshape(FOX_HEADS, n_seq, t), (1, 0, 2)),
             jnp.zeros((n_seq, FOX_HEADS, dblk - t), F32)], axis=2)
        fcum = _cumsum_time(lf_all.reshape(n_seq * FOX_HEADS, past_len + dblk), dblk)
        cache_t = lambda c: jnp.transpose(c, (0, 2, 3, 1)).reshape(n_seq, HEAD_PAIRS, LANES, past_len)
        new_t = lambda a: jnp.transpose(a[0].reshape(D_MODEL, n_seq, t), (1, 0, 2))
        o = _decode_fox_attention(qkvh, new_t(fkt), new_t(fvt), cache_t(past_k), cache_t(past_v), fcum,
                                  n_seq, t, dblk)
    x2d = _mixer_out_ffn(x2d, o, wts["fox_wout"], wts["ln_g"][0][0], wts["ln_b"][0][0],
                         wts["ffn_win"][0], wts["ffn_wout"][0], wts["ln_g"][0][1], wts["ln_b"][0][1])

    qkvh, dk, dv, vt, kaug = _qkv_proj(x2d, n_seq, wts["diff_wqkv"], wts["diff_wvt"], DIFF_HD ** -0.5 * LOG2E)
    if past is None:
        o = _diff_attention(qkvh, kaug, vt, wts["diff_lambda"], wts["diff_g"], n_seq, t, blk, lam_init)
    else:
        o = _decode_diff_attention(qkvh, past[3], past[4], wts["diff_lambda"], wts["diff_g"], n_seq, t,
                                   min(DECODE_BLK, past[3].shape[1]), lam_init)
    x2d = _mixer_out_ffn(x2d, o, wts["diff_wout"], wts["ln_g"][1][0], wts["ln_b"][1][0],
                         wts["ffn_win"][1], wts["ffn_wout"][1], wts["ln_g"][1][1], wts["ln_b"][1][1])

    def untranspose(a):
        a = a.reshape(groups, FOX_HEADS, FOX_HD, n_seq // groups, t)
        return jnp.transpose(a, (0, 3, 4, 1, 2)).reshape(1, n_seq, t, FOX_HEADS, FOX_HD)

    return (x2d.reshape(n_seq, t, D_MODEL),
            untranspose(fkt), untranspose(fvt),
            logf.reshape(1, n_seq, t, FOX_HEADS),
            dk.reshape(1, n_seq, t, DIFF_HEADS, 2 * DIFF_HD), dv.reshape(1, n_seq, t, DIFF_HEADS, 2 * DIFF_HD))


def _prepare_weights(fox_w_in, fox_b_f, fox_w_out, diff_w_in, diff_lambda, diff_subln_g, diff_w_out,
                     ffn_w_in, ffn_w_out, ln_g, ln_b):
    wf = fox_w_in[0][:, 3 * D_MODEL:].astype(BF16)
    wf_pad = jnp.pad(wf, ((0, 0), (0, LANES - FOX_HEADS)))
    bf_pad = jnp.pad(fox_b_f[0].reshape(1, FOX_HEADS), ((0, 0), (0, LANES - FOX_HEADS)))
    return {
        "fox_wqkv": fox_w_in[0][:, :3 * D_MODEL].astype(BF16),
        "fox_wvt": fox_w_in[0][:, 2 * D_MODEL:3 * D_MODEL].T.astype(BF16),
        "fox_forget": (fox_w_in[0][:, D_MODEL:2 * D_MODEL].T.astype(BF16),
                       wf_pad, wf.T, bf_pad, fox_b_f[0].reshape(FOX_HEADS, 1)),
        "fox_wout": fox_w_out[0].astype(BF16),
        "diff_wqkv": diff_w_in[0].astype(BF16),
        "diff_wvt": diff_w_in[0][:, 2 * D_MODEL:].T.astype(BF16),
        "diff_lambda": diff_lambda[0],
        "diff_g": diff_subln_g[0].reshape(1, 2 * DIFF_HD),
        "diff_wout": diff_w_out[0].astype(BF16),
        "ffn_win": [ffn_w_in[i].astype(BF16) for i in range(DEPTH)],
        "ffn_wout": [ffn_w_out[i].astype(BF16) for i in range(DEPTH)],
        "ln_g": [[ln_g[i, j].reshape(1, D_MODEL) for j in range(2)] for i in range(DEPTH)],
        "ln_b": [[ln_b[i, j].reshape(1, D_MODEL) for j in range(2)] for i in range(DEPTH)],
    }


def kernel(x_prompt, x_sample, cache_fox_k, cache_fox_v, cache_fox_logf, cache_diff_k, cache_diff_v, fox_w_in, fox_b_f, fox_w_out, diff_w_in, diff_lambda, diff_subln_g, diff_w_out, ffn_w_in, ffn_w_out, ln_g, ln_b):
    wts = _prepare_weights(fox_w_in, fox_b_f, fox_w_out, diff_w_in, diff_lambda, diff_subln_g, diff_w_out,
                           ffn_w_in, ffn_w_out, ln_g, ln_b)
    y_p, fk_p, fv_p, lf_p, dk_p, dv_p = _run_stream(x_prompt, None, wts)
    past = (cache_fox_k[0], cache_fox_v[0], cache_fox_logf[0], cache_diff_k[0], cache_diff_v[0])
    y_s, fk_s, fv_s, lf_s, dk_s, dv_s = _run_stream(x_sample, past, wts)
    return (y_p, y_s, fk_p, fv_p, lf_p, dk_p, dv_p, fk_s, fv_s, lf_s, dk_s, dv_s)
```

```python
import functools
import math

import jax
import jax.numpy as jnp
import numpy as np
from jax import lax
from jax.experimental import pallas as pl
from jax.experimental.pallas import tpu as pltpu

F32 = jnp.float32
BF16 = jnp.bfloat16

D_MODEL = 1024
DEPTH = 2
FOX_HEADS = 16
FOX_HD = 64
DIFF_HEADS = 8
DIFF_HD = 64
D_FF = 2816
CHUNK = 64
DEEPNORM_ALPHA = (2.0 * DEPTH) ** 0.25
LN_EPS = 1e-5
NEG_INF = -1e30
LOG2E = 1.4426950408889634

LANES = 128
HEAD_PAIRS = D_MODEL // LANES
ATTN_BLK = 512
ONES_ROWS = 16
DECODE_BLK = 1024
VMEM_LIMIT = 56 * 1024 * 1024


def _cparams(n_axes):
    return pltpu.CompilerParams(dimension_semantics=("arbitrary",) * n_axes,
                                vmem_limit_bytes=VMEM_LIMIT)


def _log_sigmoid(z):
    return jnp.minimum(z, 0.0) - jnp.log1p(jnp.exp(-jnp.abs(z)))


def _layernorm(z, g, b):
    mu = jnp.mean(z, axis=-1, keepdims=True)
    zc = z - mu
    var = jnp.mean(zc * zc, axis=-1, keepdims=True)
    return zc * lax.rsqrt(var + LN_EPS) * g + b


def _dot(a, b):
    return jnp.dot(a, b, preferred_element_type=F32)


def _dot_nt(a, b):
    return lax.dot_general(a, b, (((1,), (1,)), ((), ())), preferred_element_type=F32)


def _vt_rows(group):
    return (LANES // group) * (group + ONES_ROWS)


def _split3(x):
    hi = x.astype(BF16).astype(F32)
    r = x - hi
    mid = r.astype(BF16).astype(F32)
    lo = (r - mid).astype(BF16).astype(F32)
    return hi, mid, lo


def _alibi_columns(rows):
    slope = np.asarray([2.0 ** -(h + 1) * LOG2E for h in range(DIFF_HEADS)], np.float32)
    x = np.arange(rows, dtype=np.float32)[None, :] * slope[:, None]
    cols = np.zeros((DIFF_HEADS, rows, LANES), np.float32)
    for term in range(3):
        part = x.astype(BF16).astype(np.float32)
        cols[:, :, term] = part
        x = x - part
    return jnp.asarray(cols.astype(BF16))


def _fox_key_columns(lfp, carry_ref, tri_ref, pick_ref, first_tile, first_of_seq):
    tm = lfp.shape[0]

    @pl.when(first_tile)
    def _():
        row = lax.broadcasted_iota(jnp.int32, (tm, tm), 0)
        col = lax.broadcasted_iota(jnp.int32, (tm, tm), 1)
        tri_ref[...] = (col <= row).astype(BF16)
        sr = lax.broadcasted_iota(jnp.int32, (LANES, D_MODEL), 0)
        sc = lax.broadcasted_iota(jnp.int32, (LANES, D_MODEL), 1)
        pair, lane = sc // LANES, sc % LANES
        for term in range(3):
            pick_ref[term] = (((sr == 2 * pair) & (lane == term))
                              | ((sr == 2 * pair + 1) & (lane == 3 + term))).astype(BF16)

    @pl.when(first_of_seq)
    def _():
        carry_ref[...] = jnp.zeros_like(carry_ref)

    tri = tri_ref[...]
    hi, mid, lo = _split3(lfp)
    f_loc = _dot(tri, hi.astype(BF16)) + _dot(tri, mid.astype(BF16)) + _dot(tri, lo.astype(BF16))
    f_before = carry_ref[...]
    carry_ref[...] = f_before + f_loc[tm - 1:tm, :]
    terms = _split3(f_loc * (-LOG2E))
    cols = sum(_dot(terms[t].astype(BF16), pick_ref[t]) for t in range(3)).astype(BF16)
    return f_before, cols


def _proj_body(*refs, q_scale, with_forget, tiles_per_seq):
    if with_forget:
        (x_ref, w_ref, wvt_ref, wkt_ref, wf_ref, wft_ref, bf_ref, bft_ref,
         qkv_ref, k_ref, v_ref, vt_ref, ka_ref, fs_ref, lf_ref, lft_ref, carry_ref, tri_ref, pick_ref) = refs
    else:
        x_ref, w_ref, wvt_ref, pos_ref, qkv_ref, k_ref, v_ref, vt_ref, ka_ref = refs
    i = pl.program_id(0)
    tm = x_ref.shape[0]
    xb = x_ref[...].astype(BF16)

    def project(part):
        return _dot(xb, w_ref[:, part * D_MODEL:(part + 1) * D_MODEL])

    def store_heads(part, val):
        vb = val.astype(BF16)
        for p in range(HEAD_PAIRS):
            qkv_ref[part, p] = vb[:, p * LANES:(p + 1) * LANES]
        return vb

    store_heads(0, project(0) * q_scale)
    k = project(1)
    vt32 = _dot_nt(wvt_ref[...], xb)
    if with_forget:
        lfp = _log_sigmoid(_dot(xb, wf_ref[...]) + bf_ref[...])
        lf_ref[...] = lfp[:, :FOX_HEADS]
        lft_ref[...] = _log_sigmoid(_dot_nt(wft_ref[...], xb) + bft_ref[...])
        k_ref[0] = _dot_nt(wkt_ref[...], xb)
        v_ref[0] = vt32
        fs_ref[0], cols = _fox_key_columns(lfp, carry_ref, tri_ref, pick_ref, i == 0, i % tiles_per_seq == 0)
        kb = k.astype(BF16)
        for p in range(HEAD_PAIRS):
            ka_ref[p, :, 0:LANES] = kb[:, p * LANES:(p + 1) * LANES]
            ka_ref[p, :, LANES:2 * LANES] = cols[:, p * LANES:(p + 1) * LANES]
    else:
        kb = store_heads(1, k)
        v = project(2)
        store_heads(2, v)
        for h in range(DIFF_HEADS):
            k_ref[pl.ds(h, tm, stride=DIFF_HEADS), :] = k[:, h * LANES:(h + 1) * LANES]
            v_ref[pl.ds(h, tm, stride=DIFF_HEADS), :] = v[:, h * LANES:(h + 1) * LANES]
            ka_ref[h, :, 0:LANES] = kb[:, h * LANES:(h + 1) * LANES]
            ka_ref[h, :, LANES:2 * LANES] = pos_ref[h]
    vt = vt32.astype(BF16)
    ones = jnp.ones((ONES_ROWS, tm), BF16)
    group = FOX_HD if with_forget else LANES
    for p in range(HEAD_PAIRS):
        for gi in range(LANES // group):
            src = p * LANES + gi * group
            dst = gi * (group + ONES_ROWS)
            vt_ref[p, 0, dst:dst + group, :] = vt[src:src + group, :]
            vt_ref[p, 0, dst + group:dst + group + ONES_ROWS, :] = ones


def _qkv_proj(x2d, n_seq, w_qkv, w_vt, q_scale, forget=None):
    m = x2d.shape[0]
    tm = min(ATTN_BLK, m)
    nblk = m // tm
    group = FOX_HD if forget is not None else LANES
    vt_rows = _vt_rows(group)
    row = pl.BlockSpec((tm, D_MODEL), lambda i: (i, 0))
    const = lambda shape: pl.BlockSpec(shape, lambda i: (0,) * len(shape))
    if forget is not None:
        n_parts, n_w = 1, 2
        per_seq = nblk // n_seq
        kv_shape = jax.ShapeDtypeStruct((n_seq, D_MODEL, m // n_seq), F32)
        kv_spec = pl.BlockSpec((1, D_MODEL, tm), lambda i: (i // per_seq, 0, i % per_seq))
    else:
        n_parts, n_w = 3, 3
        kv_shape = jax.ShapeDtypeStruct((m * DIFF_HEADS, LANES), F32)
        kv_spec = pl.BlockSpec((tm * DIFF_HEADS, LANES), lambda i: (i, 0))
    in_specs = [row, const((D_MODEL, n_w * D_MODEL)), const((D_MODEL, D_MODEL))]
    out_shape = [jax.ShapeDtypeStruct((n_parts, HEAD_PAIRS, m, LANES), BF16),
                 kv_shape, kv_shape,
                 jax.ShapeDtypeStruct((HEAD_PAIRS, nblk, vt_rows, tm), BF16)]
    out_specs = [pl.BlockSpec((n_parts, HEAD_PAIRS, tm, LANES), lambda i: (0, 0, i, 0)),
                 kv_spec, kv_spec,
                 pl.BlockSpec((HEAD_PAIRS, 1, vt_rows, tm), lambda i: (0, i, 0, 0))]
    args = [x2d, w_qkv[:, :n_w * D_MODEL], w_vt]
    ka_shape = jax.ShapeDtypeStruct((HEAD_PAIRS, m, 2 * LANES), BF16)
    ka_spec = pl.BlockSpec((HEAD_PAIRS, tm, 2 * LANES), lambda i: (0, i, 0))
    scratch = []
    tiles_per_seq = 1
    if forget is not None:
        w_kt, wf, wft, bf, bft = forget
        tiles_per_seq = per_seq
        in_specs.append(const((D_MODEL, D_MODEL)))
        args.append(w_kt)
        in_specs += [const((D_MODEL, LANES)), const((FOX_HEADS, D_MODEL)), const((1, LANES)), const((FOX_HEADS, 1))]
        out_shape += [ka_shape,
                      jax.ShapeDtypeStruct((nblk, 1, LANES), F32),
                      jax.ShapeDtypeStruct((m, FOX_HEADS), F32),
                      jax.ShapeDtypeStruct((FOX_HEADS, m), F32)]
        out_specs += [ka_spec,
                      pl.BlockSpec((1, 1, LANES), lambda i: (i, 0, 0)),
                      pl.BlockSpec((tm, FOX_HEADS), lambda i: (i, 0)),
                      pl.BlockSpec((FOX_HEADS, tm), lambda i: (0, i))]
        args += [wf, wft, bf, bft]
        scratch = [pltpu.VMEM((1, LANES), F32), pltpu.VMEM((tm, tm), BF16), pltpu.VMEM((3, LANES, D_MODEL), BF16)]
    else:
        in_specs.append(const((DIFF_HEADS, tm, LANES)))
        args.append(_alibi_columns(tm))
        out_shape.append(ka_shape)
        out_specs.append(ka_spec)
    return pl.pallas_call(
        functools.partial(_proj_body, q_scale=q_scale, with_forget=forget is not None,
                          tiles_per_seq=tiles_per_seq),
        grid=(nblk,), in_specs=in_specs, out_specs=out_specs, out_shape=out_shape, scratch_shapes=scratch,
        compiler_params=_cparams(1), name="qkv_proj_f" if forget is not None else "qkv_proj",
    )(*args)


def _cumsum_body(x_ref, o_ref, carry_ref, tri_ref, *, blk):
    @pl.when(pl.program_id(0) == 0)
    def _():
        row = lax.broadcasted_iota(jnp.int32, (blk, blk), 0)
        col = lax.broadcasted_iota(jnp.int32, (blk, blk), 1)
        tri_ref[...] = (row <= col).astype(BF16)
        carry_ref[...] = jnp.zeros_like(carry_ref)

    hi, mid, lo = _split3(x_ref[...])
    tri = tri_ref[...]
    cs = _dot(hi.astype(BF16), tri) + _dot(mid.astype(BF16), tri) + _dot(lo.astype(BF16), tri) + carry_ref[...]
    o_ref[0] = cs
    carry_ref[...] = cs[:, blk - 1:blk]


def _cumsum_time(lf, blk):
    rows, r = lf.shape
    nblk = r // blk
    return pl.pallas_call(
        functools.partial(_cumsum_body, blk=blk),
        grid=(nblk,),
        in_specs=[pl.BlockSpec((rows, blk), lambda i: (0, i))],
        out_specs=pl.BlockSpec((1, rows, blk), lambda i: (i, 0, 0)),
        out_shape=jax.ShapeDtypeStruct((nblk, rows, blk), F32),
        scratch_shapes=[pltpu.VMEM((rows, 1), F32), pltpu.VMEM((blk, blk), BF16)],
        compiler_params=_cparams(1), name="cumsum_time",
    )(lf)


def _augment_queries(q, blk, ones_first_lane):
    lane = lax.broadcasted_iota(jnp.int32, (blk, LANES), 1)
    lo_mask = lane < LANES // 2
    zero = jnp.zeros_like(q)
    out = []
    for u, part in enumerate((jnp.where(lo_mask, q, zero), jnp.where(lo_mask, zero, q))):
        first = ones_first_lane[u]
        ones = jnp.where((lane >= first) & (lane < first + 3), 1.0, 0.0).astype(BF16)
        out.append(jnp.concatenate([part, ones], axis=1))
    return out


def _softmax_step(s, bmax, c, vt, m_ref, l_ref, u):
    group = vt.shape[0] - ONES_ROWS
    m_old = m_ref[u]
    m_new = jnp.maximum(m_old, bmax + c)
    p = jnp.exp2(s - (m_new - c))
    alpha = jnp.exp2(m_old - m_new)
    pv = _dot(vt, p.astype(BF16))
    l_ref[u] = alpha * l_ref[u] + pv[group:group + 1, :]
    m_ref[u] = m_new
    return alpha, pv[:group, :]


def _sweep_key_tiles(qi, produce, consume):
    @pl.when(qi == 0)
    def _():
        produce(0, 0, True)

    @pl.when(qi > 0)
    def _():
        n_pairs = lax.shift_right_logical(qi - 1, 1)
        odd = (qi - 1) & 1

        @pl.when(odd == 1)
        def _():
            produce(0, 0, False)
            produce(1, 1, False)
            consume(0, 0)

        @pl.when(odd == 0)
        def _():
            produce(0, 1, False)

        def two_tiles(j):
            produce(j + 1, 0, False)
            consume(j, 1)
            produce(j + 2, 1, False)
            consume(j + 1, 0)

        def body2(i, carry):
            two_tiles(odd + 2 * i)
            return carry

        n_quads = lax.shift_right_logical(n_pairs, 1)

        def body4(i, carry):
            j = odd + 2 * (n_pairs & 1)
            two_tiles(j)
            two_tiles(j + 2)
            return carry

        def body8(i, carry):
            j = odd + 2 * (n_pairs & 1) + 4 * (n_quads & 1) + 8 * i
            for step in range(4):
                two_tiles(j + 2 * step)
            return carry

        lax.fori_loop(0, n_pairs & 1, body2, 0)
        lax.fori_loop(0, n_quads & 1, body4, 0)
        lax.fori_loop(0, lax.shift_right_logical(n_quads, 1), body8, 0)
        produce(qi, 0, True)
        consume(qi - 1, 1)

    consume(qi, 0)


def _fox_attn_body(fs_ref, q_ref, ka_ref, vt_ref, o_ref, acc_ref, m_ref, l_ref, s_ref, bm_ref, *, blk, nq):
    seq = pl.program_id(0)
    pair = pl.program_id(1)
    qi = pl.program_id(2)
    qa = _augment_queries(q_ref[0, 0], blk, (0, 3))
    acc_ref[...] = jnp.zeros_like(acc_ref)
    m_ref[...] = jnp.full_like(m_ref, NEG_INF)
    l_ref[...] = jnp.zeros_like(l_ref)
    half = FOX_HD

    def produce(kj, slot, diagonal):
        start = pl.multiple_of(kj * blk, blk)
        ka = ka_ref[0, pl.ds(start, blk), :]
        if diagonal:
            key = lax.broadcasted_iota(jnp.int32, (blk, blk), 0)
            query = lax.broadcasted_iota(jnp.int32, (blk, blk), 1)
            causal = key <= query
        for h in range(2):
            s = _dot_nt(ka, qa[h])
            if diagonal:
                s = jnp.where(causal, s, NEG_INF)
            s_ref[slot, h] = s
            bm_ref[slot, h] = jnp.max(s, axis=0, keepdims=True)

    def consume(kj, slot):
        vt = vt_ref[0, kj]
        for h in range(2):
            head = 2 * pair + h
            f_q = jnp.full((1, blk), fs_ref[(seq * nq + qi) * FOX_HEADS + head], F32)
            f_k = jnp.full((1, blk), fs_ref[(seq * nq + kj) * FOX_HEADS + head], F32)
            c = (f_q - f_k) * LOG2E
            vt_h = vt[h * (half + ONES_ROWS):(h + 1) * (half + ONES_ROWS), :]
            alpha, pv = _softmax_step(s_ref[slot, h], bm_ref[slot, h], c, vt_h, m_ref, l_ref, h)
            rows = slice(h * half, (h + 1) * half)
            acc_ref[rows, :] = alpha * acc_ref[rows, :] + pv

    _sweep_key_tiles(qi, produce, consume)
    sub = lax.broadcasted_iota(jnp.int32, (LANES, blk), 0)
    o_t = acc_ref[...] / jnp.where(sub < half, l_ref[0], l_ref[1])
    o_ref[...] = o_t.T.astype(BF16)


def _fox_attention(qkvh, kaug, vt, f_start, n_seq, t, blk):
    m = n_seq * t
    nq = t // blk
    grid_spec = pltpu.PrefetchScalarGridSpec(
        num_scalar_prefetch=1,
        grid=(n_seq, HEAD_PAIRS, nq),
        in_specs=[pl.BlockSpec((1, 1, blk, LANES), lambda b, p, i, fs: (0, p, b * nq + i, 0)),
                  pl.BlockSpec((1, t, 2 * LANES), lambda b, p, i, fs: (p, b, 0)),
                  pl.BlockSpec((1, nq, _vt_rows(FOX_HD), blk), lambda b, p, i, fs: (p, b, 0, 0))],
        out_specs=pl.BlockSpec((blk, LANES), lambda b, p, i, fs: (b * nq + i, p)),
        scratch_shapes=[pltpu.VMEM((LANES, blk), F32),
                        pltpu.VMEM((2, 1, blk), F32),
                        pltpu.VMEM((2, 1, blk), F32),
                        pltpu.VMEM((2, 2, blk, blk), F32),
                        pltpu.VMEM((2, 2, 1, blk), F32)])
    return pl.pallas_call(
        functools.partial(_fox_attn_body, blk=blk, nq=nq),
        grid_spec=grid_spec,
        out_shape=jax.ShapeDtypeStruct((m, D_MODEL), BF16),
        compiler_params=_cparams(3), name="fox_attention",
    )(f_start, qkvh, kaug, vt)


def _diff_lambda(lam_ref, lam_init):
    lp = lam_ref[...]
    a = jnp.sum(lp[0:1] * lp[1:2], axis=1, keepdims=True)
    b = jnp.sum(lp[2:3] * lp[3:4], axis=1, keepdims=True)
    return jnp.exp(a) - jnp.exp(b) + lam_init


def _subln(o, g, lam_init):
    o = o * lax.rsqrt(jnp.mean(o * o, axis=-1, keepdims=True) + LN_EPS)
    return o * g * (1.0 - lam_init)


def _diff_attn_body(q_ref, ka_ref, vt_ref, lam_ref, g_ref, o_ref, acc_ref, m_ref, l_ref, s_ref, bm_ref, *,
                    blk, lam_init):
    head = pl.program_id(1)
    qi = pl.program_id(2)
    qa = _augment_queries(q_ref[0, 0], blk, (0, 0))
    acc_ref[...] = jnp.zeros_like(acc_ref)
    m_ref[...] = jnp.full_like(m_ref, NEG_INF)
    l_ref[...] = jnp.zeros_like(l_ref)
    slope = lax.bitcast_convert_type(jnp.full((1, blk), 126 - head, jnp.int32) << 23, F32) * LOG2E

    def produce(kj, slot, diagonal):
        start = pl.multiple_of(kj * blk, blk)
        ka = ka_ref[0, pl.ds(start, blk), :]
        if diagonal:
            key = lax.broadcasted_iota(jnp.int32, (blk, blk), 0)
            query = lax.broadcasted_iota(jnp.int32, (blk, blk), 1)
            visible = (key // CHUNK) <= (query // CHUNK)
            ahead = slope * (-2.0 * jnp.maximum(key - query, 0).astype(F32))
        for u in range(2):
            s = _dot_nt(ka, qa[u])
            if diagonal:
                s = jnp.where(visible, s + ahead, NEG_INF)
            s_ref[slot, u] = s
            bm_ref[slot, u] = jnp.max(s, axis=0, keepdims=True)

    def consume(kj, slot):
        vt = vt_ref[0, kj]
        c = slope * jnp.full((1, blk), (kj - qi) * blk, jnp.int32).astype(F32)
        for u in range(2):
            alpha, pv = _softmax_step(s_ref[slot, u], bm_ref[slot, u], c, vt, m_ref, l_ref, u)
            acc_ref[u] = alpha * acc_ref[u] + pv

    _sweep_key_tiles(qi, produce, consume)
    lam = _diff_lambda(lam_ref, lam_init)
    o_t = acc_ref[0] / l_ref[0] - lam * (acc_ref[1] / l_ref[1])
    o_ref[...] = _subln(o_t.T, g_ref[...], lam_init).astype(BF16)


def _diff_attention(qkvh, kaug, vt, lam_params, subln_g, n_seq, t, blk, lam_init):
    m = n_seq * t
    nq = t // blk
    return pl.pallas_call(
        functools.partial(_diff_attn_body, blk=blk, lam_init=lam_init),
        grid=(n_seq, DIFF_HEADS, nq),
        in_specs=[pl.BlockSpec((1, 1, blk, LANES), lambda b, p, i: (0, p, b * nq + i, 0)),
                  pl.BlockSpec((1, t, 2 * LANES), lambda b, p, i: (p, b, 0)),
                  pl.BlockSpec((1, nq, _vt_rows(LANES), blk), lambda b, p, i: (p, b, 0, 0)),
                  pl.BlockSpec((4, DIFF_HD), lambda b, p, i: (0, 0)),
                  pl.BlockSpec((1, LANES), lambda b, p, i: (0, 0))],
        out_specs=pl.BlockSpec((blk, LANES), lambda b, p, i: (b * nq + i, p)),
        out_shape=jax.ShapeDtypeStruct((m, D_MODEL), BF16),
        scratch_shapes=[pltpu.VMEM((2, LANES, blk), F32),
                        pltpu.VMEM((2, 1, blk), F32),
                        pltpu.VMEM((2, 1, blk), F32),
                        pltpu.VMEM((2, 2, blk, blk), F32),
                        pltpu.VMEM((2, 2, 1, blk), F32)],
        compiler_params=_cparams(3), name="diff_attention",
    )(qkvh, kaug, vt, lam_params, subln_g)


def _decode_diff_body(q_ref, kn_ref, vn_ref, kc_ref, vc_ref, lam_ref, g_ref, o_ref, acc_ref, m_ref, l_ref, *,
                      t_new, past_len, blk, nk, lam_init):
    kj = pl.program_id(1)
    rows = 2 * t_new
    lane = lax.broadcasted_iota(jnp.int32, (t_new, LANES), 1)
    lo_mask = lane < DIFF_HD

    @pl.when(kj == 0)
    def _():
        acc_ref[...] = jnp.zeros_like(acc_ref)
        m_ref[...] = jnp.full_like(m_ref, NEG_INF)
        l_ref[...] = jnp.zeros_like(l_ref)

    def attend(keys, values, bias, visible):
        heads = range(DIFF_HEADS)
        scores = []
        for h in heads:
            q = q_ref[0, h]
            zero = jnp.zeros_like(q)
            q_maps = jnp.concatenate([jnp.where(lo_mask, q, zero), jnp.where(lo_mask, zero, q)], axis=0)
            s = _dot_nt(q_maps, keys(h)) + bias(h)
            scores.append(s if visible is None else jnp.where(visible, s, NEG_INF))
        m_old = [m_ref[h] for h in heads]
        m_new = [jnp.maximum(m_old[h], jnp.max(scores[h], axis=1, keepdims=True)) for h in heads]
        probs = [jnp.exp2(scores[h] - m_new[h]) for h in heads]
        alpha = [jnp.exp2(m_old[h] - m_new[h]) for h in heads]
        for h in heads:
            l_ref[h] = alpha[h] * l_ref[h] + jnp.sum(probs[h], axis=1, keepdims=True)
            m_ref[h] = m_new[h]
            acc_ref[h] = alpha[h] * acc_ref[h] + _dot(probs[h].astype(BF16), values(h))

    def slope(h):
        return 2.0 ** -(h + 1) * LOG2E

    col = lax.broadcasted_iota(jnp.int32, (1, blk), 1)
    dist = (col + (kj * blk - past_len)).astype(F32)
    attend(lambda h: kc_ref[0, pl.ds(h, blk, stride=DIFF_HEADS), :].astype(BF16),
           lambda h: vc_ref[0, pl.ds(h, blk, stride=DIFF_HEADS), :].astype(BF16),
           lambda h: slope(h) * dist, None)

    @pl.when(kj == nk - 1)
    def _():
        tq = lax.broadcasted_iota(jnp.int32, (rows, t_new), 0) % t_new
        tk = lax.broadcasted_iota(jnp.int32, (rows, t_new), 1)
        near = (tq - jnp.abs(tq - tk)).astype(F32)
        visible = ((tk + past_len) // CHUNK) <= ((tq + past_len) // CHUNK)
        attend(lambda h: kn_ref[0, h], lambda h: vn_ref[0, h], lambda h: slope(h) * near, visible)
        lam = _diff_lambda(lam_ref, lam_init)
        g = g_ref[...]
        outs = []
        for h in range(DIFF_HEADS):
            a = acc_ref[h] / l_ref[h]
            outs.append(_subln(a[:t_new] - lam * a[t_new:], g, lam_init))
        o_ref[0] = jnp.concatenate(outs, axis=1).astype(BF16)


def _decode_diff_attention(qkvh, cache_k, cache_v, lam_params, subln_g, n_seq, t_new, blk, lam_init):
    past_len, heads, hd = cache_k.shape[1:]
    nk = past_len // blk
    rows = 2 * t_new
    new_spec = lambda s: pl.BlockSpec((1, DIFF_HEADS, t_new, LANES), lambda b, j: (s, 0, b, 0))
    cache_k = cache_k.reshape(n_seq, past_len * heads, hd)
    cache_v = cache_v.reshape(n_seq, past_len * heads, hd)
    cache_spec = pl.BlockSpec((1, blk * heads, hd), lambda b, j: (b, j, 0))
    out = pl.pallas_call(
        functools.partial(_decode_diff_body, t_new=t_new, past_len=past_len, blk=blk, nk=nk, lam_init=lam_init),
        grid=(n_seq, nk),
        in_specs=[new_spec(0), new_spec(1), new_spec(2), cache_spec, cache_spec,
                  pl.BlockSpec((4, DIFF_HD), lambda b, j: (0, 0)),
                  pl.BlockSpec((1, LANES), lambda b, j: (0, 0))],
        out_specs=pl.BlockSpec((1, t_new, D_MODEL), lambda b, j: (b, 0, 0)),
        out_shape=jax.ShapeDtypeStruct((n_seq, t_new, D_MODEL), BF16),
        scratch_shapes=[pltpu.VMEM((DIFF_HEADS, rows, LANES), F32),
                        pltpu.VMEM((DIFF_HEADS, rows, 1), F32),
                        pltpu.VMEM((DIFF_HEADS, rows, 1), F32)],
        compiler_params=_cparams(2), name="decode_attention_diff",
    )(qkvh, qkvh, qkvh, cache_k, cache_v, lam_params, subln_g)
    return out.reshape(n_seq * t_new, D_MODEL)


def _decode_fox_body(q_ref, knt_ref, vnt_ref, kt_ref, vt_ref, f_ref, o_ref, acc_ref, m_ref, l_ref, *,
                     t_new, blk, nk):
    kj = pl.program_id(1)
    lane = lax.broadcasted_iota(jnp.int32, (t_new, LANES), 1)
    lo_mask = lane < FOX_HD

    @pl.when(kj == 0)
    def _():
        acc_ref[...] = jnp.zeros_like(acc_ref)
        m_ref[...] = jnp.full_like(m_ref, NEG_INF)
        l_ref[...] = jnp.zeros_like(l_ref)

    f_last = f_ref[nk - 1][:, blk - 1:blk]

    def attend(kt_pairs, vt_pairs, bias, visible):
        heads = range(FOX_HEADS)
        scores = []
        for p in range(HEAD_PAIRS):
            q = q_ref[0, p]
            zero = jnp.zeros_like(q)
            kt = kt_pairs(p)
            for qh in (jnp.where(lo_mask, q, zero), jnp.where(lo_mask, zero, q)):
                h = len(scores)
                s = _dot(qh, kt) + bias[h:h + 1, :]
                scores.append(s if visible is None else jnp.where(visible, s, NEG_INF))
        m_old = [m_ref[h] for h in heads]
        m_new = [jnp.maximum(m_old[h], jnp.max(scores[h], axis=1, keepdims=True)) for h in heads]
        probs = [jnp.exp2(scores[h] - m_new[h]) for h in heads]
        alpha = [jnp.exp2(m_old[h] - m_new[h]) for h in heads]
        for h in heads:
            l_ref[h] = alpha[h] * l_ref[h] + jnp.sum(probs[h], axis=1, keepdims=True)
            m_ref[h] = m_new[h]
        for p in range(HEAD_PAIRS):
            vt = vt_pairs(p)
            pv = [_dot_nt(probs[2 * p + e].astype(BF16), vt) for e in range(2)]
            acc = acc_ref[p]
            acc_ref[p] = jnp.where(lo_mask, alpha[2 * p] * acc + pv[0], alpha[2 * p + 1] * acc + pv[1])

    attend(lambda p: kt_ref[0, p].astype(BF16), lambda p: vt_ref[0, p].astype(BF16),
           (f_last - f_ref[kj]) * LOG2E, None)

    @pl.when(kj == nk - 1)
    def _():
        tq = lax.broadcasted_iota(jnp.int32, (t_new, t_new), 0)
        tk = lax.broadcasted_iota(jnp.int32, (t_new, t_new), 1)
        attend(lambda p: knt_ref[0, p * LANES:(p + 1) * LANES, :].astype(BF16),
               lambda p: vnt_ref[0, p * LANES:(p + 1) * LANES, :].astype(BF16),
               (f_last - f_ref[nk][:, :t_new]) * LOG2E, tk <= tq)
        o_ref[0] = jnp.concatenate(
            [acc_ref[p] / jnp.where(lo_mask, l_ref[2 * p], l_ref[2 * p + 1]) for p in range(HEAD_PAIRS)],
            axis=1).astype(BF16)


def _decode_fox_attention(qkvh, knt, vnt, cache_kt, cache_vt, fcum, n_seq, t_new, blk):
    past_len = cache_kt.shape[3]
    nk = past_len // blk
    cache_spec = pl.BlockSpec((1, HEAD_PAIRS, LANES, blk), lambda b, j: (b, 0, 0, j))
    new_spec = pl.BlockSpec((1, D_MODEL, t_new), lambda b, j: (b, 0, 0))
    out = pl.pallas_call(
        functools.partial(_decode_fox_body, t_new=t_new, blk=blk, nk=nk),
        grid=(n_seq, nk),
        in_specs=[pl.BlockSpec((1, HEAD_PAIRS, t_new, LANES), lambda b, j: (0, 0, b, 0)),
                  new_spec, new_spec, cache_spec, cache_spec,
                  pl.BlockSpec((nk + 1, FOX_HEADS, blk), lambda b, j: (0, b, 0))],
        out_specs=pl.BlockSpec((1, t_new, D_MODEL), lambda b, j: (b, 0, 0)),
        out_shape=jax.ShapeDtypeStruct((n_seq, t_new, D_MODEL), BF16),
        scratch_shapes=[pltpu.VMEM((HEAD_PAIRS, t_new, LANES), F32),
                        pltpu.VMEM((FOX_HEADS, t_new, 1), F32),
                        pltpu.VMEM((FOX_HEADS, t_new, 1), F32)],
        compiler_params=_cparams(2), name="decode_attention_fox",
    )(qkvh, knt, vnt, cache_kt, cache_vt, fcum)
    return out.reshape(n_seq * t_new, D_MODEL)


def _mixer_out_ffn_body(x_ref, o_ref, wo_ref, g1_ref, b1_ref, win_ref, wout_ref, g2_ref, b2_ref, y_ref):
    half = x_ref.shape[0] // 2
    halves = [slice(0, half), slice(half, 2 * half)]
    mix = [_dot(o_ref[r, :], wo_ref[...]) for r in halves]
    x = [_layernorm(DEEPNORM_ALPHA * x_ref[r, :] + mix[i], g1_ref[...], b1_ref[...]) for i, r in enumerate(halves)]
    h = [_dot(xi.astype(BF16), win_ref[...]) for xi in x]
    act = [(hi[:, :D_FF] * jax.nn.sigmoid(hi[:, :D_FF]) * hi[:, D_FF:]).astype(BF16) for hi in h]
    y = [_dot(a, wout_ref[...]) for a in act]
    for i, r in enumerate(halves):
        y_ref[r, :] = _layernorm(DEEPNORM_ALPHA * x[i] + y[i], g2_ref[...], b2_ref[...])


def _mixer_out_ffn(x2d, o2d, w_o, g1, b1, w_in, w_out, g2, b2, tm=512):
    m = x2d.shape[0]
    tm = min(tm, m)
    row = pl.BlockSpec((tm, D_MODEL), lambda i: (i, 0))
    vec = pl.BlockSpec((1, D_MODEL), lambda i: (0, 0))
    resident = lambda shape: pl.BlockSpec(shape, lambda i: (0, 0), pipeline_mode=pl.Buffered(1))
    return pl.pallas_call(
        _mixer_out_ffn_body, grid=(m // tm,),
        in_specs=[row, row, resident((D_MODEL, D_MODEL)), vec, vec,
                  resident((D_MODEL, 2 * D_FF)),
                  resident((D_FF, D_MODEL)),
                  vec, vec],
        out_specs=row, out_shape=jax.ShapeDtypeStruct((m, D_MODEL), F32),
        compiler_params=_cparams(1), name="mixer_out_ffn",
    )(x2d, o2d, w_o, g1, b1, w_in, w_out, g2, b2)


def _run_stream(x, past, wts):
    n_seq, t, _ = x.shape
    m = n_seq * t
    blk = ATTN_BLK
    x2d = x.reshape(m, D_MODEL)
    lam_init = 0.8 - 0.6 * math.exp(-0.3 * 1)

    groups = n_seq if t >= blk else 1
    qkvh, fkt, fvt, vt, kaug, f_start, logf, logft = _qkv_proj(
        x2d, groups, wts["fox_wqkv"], wts["fox_wvt"], FOX_HD ** -0.5 * LOG2E, forget=wts["fox_forget"])
    if past is None:
        o = _fox_attention(qkvh, kaug, vt, f_start[:, 0, :FOX_HEADS].reshape(-1), n_seq, t, blk)
    else:
        past_k, past_v, past_lf = past[0], past[1], past[2]
        past_len = past_k.shape[1]
        dblk = min(DECODE_BLK, past_len)
        lf_all = jnp.concatenate(
            [jnp.transpose(past_lf, (0, 2, 1)),
             jnp.transpose(logft.reshape(FOX_HEADS, n_seq, t), (1, 0, 2)),
             jnp.zeros((n_seq, FOX_HEADS, dblk - t), F32)], axis=2)
        fcum = _cumsum_time(lf_all.reshape(n_seq * FOX_HEADS, past_len + dblk), dblk)
        cache_t = lambda c: jnp.transpose(c, (0, 2, 3, 1)).reshape(n_seq, HEAD_PAIRS, LANES, past_len)
        new_t = lambda a: jnp.transpose(a[0].reshape(D_MODEL, n_seq, t), (1, 0, 2))
        o = _decode_fox_attention(qkvh, new_t(fkt), new_t(fvt), cache_t(past_k), cache_t(past_v), fcum,
                                  n_seq, t, dblk)
    x2d = _mixer_out_ffn(x2d, o, wts["fox_wout"], wts["ln_g"][0][0], wts["ln_b"][0][0],
                         wts["ffn_win"][0], wts["ffn_wout"][0], wts["ln_g"][0][1], wts["ln_b"][0][1])

    qkvh, dk, dv, vt, kaug = _qkv_proj(x2d, n_seq, wts["diff_wqkv"], wts["diff_wvt"], DIFF_HD ** -0.5 * LOG2E)
    if past is None:
        o = _diff_attention(qkvh, kaug, vt, wts["diff_lambda"], wts["diff_g"], n_seq, t, blk, lam_init)
    else:
        o = _decode_diff_attention(qkvh, past[3], past[4], wts["diff_lambda"], wts["diff_g"], n_seq, t,
                                   min(DECODE_BLK, past[3].shape[1]), lam_init)
    x2d = _mixer_out_ffn(x2d, o, wts["diff_wout"], wts["ln_g"][1][0], wts["ln_b"][1][0],
                         wts["ffn_win"][1], wts["ffn_wout"][1], wts["ln_g"][1][1], wts["ln_b"][1][1])

    def untranspose(a):
        a = a.reshape(groups, FOX_HEADS, FOX_HD, n_seq // groups, t)
        return jnp.transpose(a, (0, 3, 4, 1, 2)).reshape(1, n_seq, t, FOX_HEADS, FOX_HD)

    return (x2d.reshape(n_seq, t, D_MODEL),
            untranspose(fkt), untranspose(fvt),
            logf.reshape(1, n_seq, t, FOX_HEADS),
            dk.reshape(1, n_seq, t, DIFF_HEADS, 2 * DIFF_HD), dv.reshape(1, n_seq, t, DIFF_HEADS, 2 * DIFF_HD))


def _prepare_weights(fox_w_in, fox_b_f, fox_w_out, diff_w_in, diff_lambda, diff_subln_g, diff_w_out,
                     ffn_w_in, ffn_w_out, ln_g, ln_b):
    wf = fox_w_in[0][:, 3 * D_MODEL:].astype(BF16)
    wf_pad = jnp.pad(wf, ((0, 0), (0, LANES - FOX_HEADS)))
    bf_pad = jnp.pad(fox_b_f[0].reshape(1, FOX_HEADS), ((0, 0), (0, LANES - FOX_HEADS)))
    return {
        "fox_wqkv": fox_w_in[0][:, :3 * D_MODEL].astype(BF16),
        "fox_wvt": fox_w_in[0][:, 2 * D_MODEL:3 * D_MODEL].T.astype(BF16),
        "fox_forget": (fox_w_in[0][:, D_MODEL:2 * D_MODEL].T.astype(BF16),
                       wf_pad, wf.T, bf_pad, fox_b_f[0].reshape(FOX_HEADS, 1)),
        "fox_wout": fox_w_out[0].astype(BF16),
        "diff_wqkv": diff_w_in[0].astype(BF16),
        "diff_wvt": diff_w_in[0][:, 2 * D_MODEL:].T.astype(BF16),
        "diff_lambda": diff_lambda[0],
        "diff_g": diff_subln_g[0].reshape(1, 2 * DIFF_HD),
        "diff_wout": diff_w_out[0].astype(BF16),
        "ffn_win": [ffn_w_in[i].astype(BF16) for i in range(DEPTH)],
        "ffn_wout": [ffn_w_out[i].astype(BF16) for i in range(DEPTH)],
        "ln_g": [[ln_g[i, j].reshape(1, D_MODEL) for j in range(2)] for i in range(DEPTH)],
        "ln_b": [[ln_b[i, j].reshape(1, D_MODEL) for j in range(2)] for i in range(DEPTH)],
    }


def kernel(x_prompt, x_sample, cache_fox_k, cache_fox_v, cache_fox_logf, cache_diff_k, cache_diff_v, fox_w_in, fox_b_f, fox_w_out, diff_w_in, diff_lambda, diff_subln_g, diff_w_out, ffn_w_in, ffn_w_out, ln_g, ln_b):
    wts = _prepare_weights(fox_w_in, fox_b_f, fox_w_out, diff_w_in, diff_lambda, diff_subln_g, diff_w_out,
                           ffn_w_in, ffn_w_out, ln_g, ln_b)
    y_p, fk_p, fv_p, lf_p, dk_p, dv_p = _run_stream(x_prompt, None, wts)
    past = (cache_fox_k[0], cache_fox_v[0], cache_fox_logf[0], cache_diff_k[0], cache_diff_v[0])
    y_s, fk_s, fv_s, lf_s, dk_s, dv_s = _run_stream(x_sample, past, wts)
    return (y_p, y_s, fk_p, fv_p, lf_p, dk_p, dv_p, fk_s, fv_s, lf_s, dk_s, dv_s)
```

```python
import functools
import math

import jax
import jax.numpy as jnp
import numpy as np
from jax import lax
from jax.experimental import pallas as pl
from jax.experimental.pallas import tpu as pltpu

F32 = jnp.float32
BF16 = jnp.bfloat16

D_MODEL = 1024
DEPTH = 2
FOX_HEADS = 16
FOX_HD = 64
DIFF_HEADS = 8
DIFF_HD = 64
D_FF = 2816
CHUNK = 64
DEEPNORM_ALPHA = (2.0 * DEPTH) ** 0.25
LN_EPS = 1e-5
NEG_INF = -1e30
LOG2E = 1.4426950408889634

LANES = 128
HEAD_PAIRS = D_MODEL // LANES
ATTN_BLK = 512
ONES_ROWS = 16
DECODE_BLK = 2048
VMEM_LIMIT = 56 * 1024 * 1024


def _cparams(n_axes):
    return pltpu.CompilerParams(dimension_semantics=("arbitrary",) * n_axes,
                                vmem_limit_bytes=VMEM_LIMIT)


def _log_sigmoid(z):
    return jnp.minimum(z, 0.0) - jnp.log1p(jnp.exp(-jnp.abs(z)))


def _layernorm(z, g, b):
    mu = jnp.mean(z, axis=-1, keepdims=True)
    zc = z - mu
    var = jnp.mean(zc * zc, axis=-1, keepdims=True)
    return zc * lax.rsqrt(var + LN_EPS) * g + b


def _dot(a, b):
    return jnp.dot(a, b, preferred_element_type=F32)


def _dot_nt(a, b):
    return lax.dot_general(a, b, (((1,), (1,)), ((), ())), preferred_element_type=F32)


def _vt_rows(group):
    return (LANES // group) * (group + ONES_ROWS)


def _split3(x):
    hi = x.astype(BF16).astype(F32)
    r = x - hi
    mid = r.astype(BF16).astype(F32)
    lo = (r - mid).astype(BF16).astype(F32)
    return hi, mid, lo


def _alibi_columns(rows):
    slope = np.asarray([2.0 ** -(h + 1) * LOG2E for h in range(DIFF_HEADS)], np.float32)
    x = np.arange(rows, dtype=np.float32)[None, :] * slope[:, None]
    cols = np.zeros((DIFF_HEADS, rows, LANES), np.float32)
    for term in range(3):
        part = x.astype(BF16).astype(np.float32)
        cols[:, :, term] = part
        x = x - part
    return jnp.asarray(cols.astype(BF16))


def _fox_key_columns(lfp, carry_ref, tri_ref, pick_ref, first_tile, first_of_seq):
    tm = lfp.shape[0]

    @pl.when(first_tile)
    def _():
        row = lax.broadcasted_iota(jnp.int32, (tm, tm), 0)
        col = lax.broadcasted_iota(jnp.int32, (tm, tm), 1)
        tri_ref[...] = (col <= row).astype(BF16)
        sr = lax.broadcasted_iota(jnp.int32, (LANES, D_MODEL), 0)
        sc = lax.broadcasted_iota(jnp.int32, (LANES, D_MODEL), 1)
        pair, lane = sc // LANES, sc % LANES
        for term in range(3):
            pick_ref[term] = (((sr == 2 * pair) & (lane == term))
                              | ((sr == 2 * pair + 1) & (lane == 3 + term))).astype(BF16)

    @pl.when(first_of_seq)
    def _():
        carry_ref[...] = jnp.zeros_like(carry_ref)

    tri = tri_ref[...]
    hi, mid, lo = _split3(lfp)
    f_loc = _dot(tri, hi.astype(BF16)) + _dot(tri, mid.astype(BF16)) + _dot(tri, lo.astype(BF16))
    f_before = carry_ref[...]
    carry_ref[...] = f_before + f_loc[tm - 1:tm, :]
    terms = _split3(f_loc * (-LOG2E))
    cols = sum(_dot(terms[t].astype(BF16), pick_ref[t]) for t in range(3)).astype(BF16)
    return f_before, cols


def _proj_body(*refs, q_scale, with_forget, tiles_per_seq):
    if with_forget:
        (x_ref, w_ref, wvt_ref, wkt_ref, wf_ref, wft_ref, bf_ref, bft_ref,
         qkv_ref, k_ref, v_ref, vt_ref, ka_ref, fs_ref, lf_ref, lft_ref, carry_ref, tri_ref, pick_ref) = refs
    else:
        x_ref, w_ref, wvt_ref, pos_ref, qkv_ref, k_ref, v_ref, vt_ref, ka_ref = refs
    i = pl.program_id(0)
    tm = x_ref.shape[0]
    xb = x_ref[...].astype(BF16)

    def project(part):
        return _dot(xb, w_ref[:, part * D_MODEL:(part + 1) * D_MODEL])

    def store_heads(part, val):
        vb = val.astype(BF16)
        for p in range(HEAD_PAIRS):
            qkv_ref[part, p] = vb[:, p * LANES:(p + 1) * LANES]
        return vb

    store_heads(0, project(0) * q_scale)
    k = project(1)
    vt32 = _dot_nt(wvt_ref[...], xb)
    if with_forget:
        lfp = _log_sigmoid(_dot(xb, wf_ref[...]) + bf_ref[...])
        lf_ref[...] = lfp[:, :FOX_HEADS]
        lft_ref[...] = _log_sigmoid(_dot_nt(wft_ref[...], xb) + bft_ref[...])
        k_ref[0] = _dot_nt(wkt_ref[...], xb)
        v_ref[0] = vt32
        fs_ref[0], cols = _fox_key_columns(lfp, carry_ref, tri_ref, pick_ref, i == 0, i % tiles_per_seq == 0)
        kb = k.astype(BF16)
        for p in range(HEAD_PAIRS):
            ka_ref[p, :, 0:LANES] = kb[:, p * LANES:(p + 1) * LANES]
            ka_ref[p, :, LANES:2 * LANES] = cols[:, p * LANES:(p + 1) * LANES]
    else:
        kb = store_heads(1, k)
        v = project(2)
        store_heads(2, v)
        for h in range(DIFF_HEADS):
            k_ref[pl.ds(h, tm, stride=DIFF_HEADS), :] = k[:, h * LANES:(h + 1) * LANES]
            v_ref[pl.ds(h, tm, stride=DIFF_HEADS), :] = v[:, h * LANES:(h + 1) * LANES]
            ka_ref[h, :, 0:LANES] = kb[:, h * LANES:(h + 1) * LANES]
            ka_ref[h, :, LANES:2 * LANES] = pos_ref[h]
    vt = vt32.astype(BF16)
    ones = jnp.ones((ONES_ROWS, tm), BF16)
    group = FOX_HD if with_forget else LANES
    for p in range(HEAD_PAIRS):
        for gi in range(LANES // group):
            src = p * LANES + gi * group
            dst = gi * (group + ONES_ROWS)
            vt_ref[p, 0, dst:dst + group, :] = vt[src:src + group, :]
            vt_ref[p, 0, dst + group:dst + group + ONES_ROWS, :] = ones


def _qkv_proj(x2d, n_seq, w_qkv, w_vt, q_scale, forget=None):
    m = x2d.shape[0]
    tm = min(ATTN_BLK, m)
    nblk = m // tm
    group = FOX_HD if forget is not None else LANES
    vt_rows = _vt_rows(group)
    row = pl.BlockSpec((tm, D_MODEL), lambda i: (i, 0))
    const = lambda shape: pl.BlockSpec(shape, lambda i: (0,) * len(shape))
    if forget is not None:
        n_parts, n_w = 1, 2
        per_seq = nblk // n_seq
        kv_shape = jax.ShapeDtypeStruct((n_seq, D_MODEL, m // n_seq), F32)
        kv_spec = pl.BlockSpec((1, D_MODEL, tm), lambda i: (i // per_seq, 0, i % per_seq))
    else:
        n_parts, n_w = 3, 3
        kv_shape = jax.ShapeDtypeStruct((m * DIFF_HEADS, LANES), F32)
        kv_spec = pl.BlockSpec((tm * DIFF_HEADS, LANES), lambda i: (i, 0))
    in_specs = [row, const((D_MODEL, n_w * D_MODEL)), const((D_MODEL, D_MODEL))]
    out_shape = [jax.ShapeDtypeStruct((n_parts, HEAD_PAIRS, m, LANES), BF16),
                 kv_shape, kv_shape,
                 jax.ShapeDtypeStruct((HEAD_PAIRS, nblk, vt_rows, tm), BF16)]
    out_specs = [pl.BlockSpec((n_parts, HEAD_PAIRS, tm, LANES), lambda i: (0, 0, i, 0)),
                 kv_spec, kv_spec,
                 pl.BlockSpec((HEAD_PAIRS, 1, vt_rows, tm), lambda i: (0, i, 0, 0))]
    args = [x2d, w_qkv[:, :n_w * D_MODEL], w_vt]
    ka_shape = jax.ShapeDtypeStruct((HEAD_PAIRS, m, 2 * LANES), BF16)
    ka_spec = pl.BlockSpec((HEAD_PAIRS, tm, 2 * LANES), lambda i: (0, i, 0))
    scratch = []
    tiles_per_seq = 1
    if forget is not None:
        w_kt, wf, wft, bf, bft = forget
        tiles_per_seq = per_seq
        in_specs.append(const((D_MODEL, D_MODEL)))
        args.append(w_kt)
        in_specs += [const((D_MODEL, LANES)), const((FOX_HEADS, D_MODEL)), const((1, LANES)), const((FOX_HEADS, 1))]
        out_shape += [ka_shape,
                      jax.ShapeDtypeStruct((nblk, 1, LANES), F32),
                      jax.ShapeDtypeStruct((m, FOX_HEADS), F32),
                      jax.ShapeDtypeStruct((FOX_HEADS, m), F32)]
        out_specs += [ka_spec,
                      pl.BlockSpec((1, 1, LANES), lambda i: (i, 0, 0)),
                      pl.BlockSpec((tm, FOX_HEADS), lambda i: (i, 0)),
                      pl.BlockSpec((FOX_HEADS, tm), lambda i: (0, i))]
        args += [wf, wft, bf, bft]
        scratch = [pltpu.VMEM((1, LANES), F32), pltpu.VMEM((tm, tm), BF16), pltpu.VMEM((3, LANES, D_MODEL), BF16)]
    else:
        in_specs.append(const((DIFF_HEADS, tm, LANES)))
        args.append(_alibi_columns(tm))
        out_shape.append(ka_shape)
        out_specs.append(ka_spec)
    return pl.pallas_call(
        functools.partial(_proj_body, q_scale=q_scale, with_forget=forget is not None,
                          tiles_per_seq=tiles_per_seq),
        grid=(nblk,), in_specs=in_specs, out_specs=out_specs, out_shape=out_shape, scratch_shapes=scratch,
        compiler_params=_cparams(1), name="qkv_proj_f" if forget is not None else "qkv_proj",
    )(*args)


def _cumsum_body(x_ref, o_ref, carry_ref, tri_ref, *, blk):
    @pl.when(pl.program_id(0) == 0)
    def _():
        row = lax.broadcasted_iota(jnp.int32, (blk, blk), 0)
        col = lax.broadcasted_iota(jnp.int32, (blk, blk), 1)
        tri_ref[...] = (row <= col).astype(BF16)
        carry_ref[...] = jnp.zeros_like(carry_ref)

    hi, mid, lo = _split3(x_ref[...])
    tri = tri_ref[...]
    cs = _dot(hi.astype(BF16), tri) + _dot(mid.astype(BF16), tri) + _dot(lo.astype(BF16), tri) + carry_ref[...]
    o_ref[0] = cs
    carry_ref[...] = cs[:, blk - 1:blk]


def _cumsum_time(lf, blk):
    rows, r = lf.shape
    nblk = r // blk
    return pl.pallas_call(
        functools.partial(_cumsum_body, blk=blk),
        grid=(nblk,),
        in_specs=[pl.BlockSpec((rows, blk), lambda i: (0, i))],
        out_specs=pl.BlockSpec((1, rows, blk), lambda i: (i, 0, 0)),
        out_shape=jax.ShapeDtypeStruct((nblk, rows, blk), F32),
        scratch_shapes=[pltpu.VMEM((rows, 1), F32), pltpu.VMEM((blk, blk), BF16)],
        compiler_params=_cparams(1), name="cumsum_time",
    )(lf)


def _augment_queries(q, blk, ones_first_lane):
    lane = lax.broadcasted_iota(jnp.int32, (blk, LANES), 1)
    lo_mask = lane < LANES // 2
    zero = jnp.zeros_like(q)
    out = []
    for u, part in enumerate((jnp.where(lo_mask, q, zero), jnp.where(lo_mask, zero, q))):
        first = ones_first_lane[u]
        ones = jnp.where((lane >= first) & (lane < first + 3), 1.0, 0.0).astype(BF16)
        out.append(jnp.concatenate([part, ones], axis=1))
    return out


def _softmax_step(s, bmax, c, vt, m_ref, l_ref, u):
    group = vt.shape[0] - ONES_ROWS
    m_old = m_ref[u]
    m_new = jnp.maximum(m_old, bmax + c)
    p = jnp.exp2(s - (m_new - c))
    alpha = jnp.exp2(m_old - m_new)
    pv = _dot(vt, p.astype(BF16))
    l_ref[u] = alpha * l_ref[u] + pv[group:group + 1, :]
    m_ref[u] = m_new
    return alpha, pv[:group, :]


def _sweep_key_tiles(qi, produce, consume):
    @pl.when(qi == 0)
    def _():
        produce(0, 0, True)

    @pl.when(qi > 0)
    def _():
        n_pairs = lax.shift_right_logical(qi - 1, 1)
        odd = (qi - 1) & 1

        @pl.when(odd == 1)
        def _():
            produce(0, 0, False)
            produce(1, 1, False)
            consume(0, 0)

        @pl.when(odd == 0)
        def _():
            produce(0, 1, False)

        def two_tiles(j):
            produce(j + 1, 0, False)
            consume(j, 1)
            produce(j + 2, 1, False)
            consume(j + 1, 0)

        def body2(i, carry):
            two_tiles(odd + 2 * i)
            return carry

        n_quads = lax.shift_right_logical(n_pairs, 1)

        def body4(i, carry):
            j = odd + 2 * (n_pairs & 1)
            two_tiles(j)
            two_tiles(j + 2)
            return carry

        def body8(i, carry):
            j = odd + 2 * (n_pairs & 1) + 4 * (n_quads & 1) + 8 * i
            for step in range(4):
                two_tiles(j + 2 * step)
            return carry

        lax.fori_loop(0, n_pairs & 1, body2, 0)
        lax.fori_loop(0, n_quads & 1, body4, 0)
        lax.fori_loop(0, lax.shift_right_logical(n_quads, 1), body8, 0)
        produce(qi, 0, True)
        consume(qi - 1, 1)

    consume(qi, 0)


def _fox_attn_body(fs_ref, q_ref, ka_ref, vt_ref, o_ref, acc_ref, m_ref, l_ref, s_ref, bm_ref, *, blk, nq):
    seq = pl.program_id(0)
    pair = pl.program_id(1)
    qi = pl.program_id(2)
    qa = _augment_queries(q_ref[0, 0], blk, (0, 3))
    acc_ref[...] = jnp.zeros_like(acc_ref)
    m_ref[...] = jnp.full_like(m_ref, NEG_INF)
    l_ref[...] = jnp.zeros_like(l_ref)
    half = FOX_HD

    def produce(kj, slot, diagonal):
        start = pl.multiple_of(kj * blk, blk)
        ka = ka_ref[0, pl.ds(start, blk), :]
        if diagonal:
            key = lax.broadcasted_iota(jnp.int32, (blk, blk), 0)
            query = lax.broadcasted_iota(jnp.int32, (blk, blk), 1)
            causal = key <= query
        for h in range(2):
            s = _dot_nt(ka, qa[h])
            if diagonal:
                s = jnp.where(causal, s, NEG_INF)
            s_ref[slot, h] = s
            bm_ref[slot, h] = jnp.max(s, axis=0, keepdims=True)

    def consume(kj, slot):
        vt = vt_ref[0, kj]
        for h in range(2):
            head = 2 * pair + h
            f_q = jnp.full((1, blk), fs_ref[(seq * nq + qi) * FOX_HEADS + head], F32)
            f_k = jnp.full((1, blk), fs_ref[(seq * nq + kj) * FOX_HEADS + head], F32)
            c = (f_q - f_k) * LOG2E
            vt_h = vt[h * (half + ONES_ROWS):(h + 1) * (half + ONES_ROWS), :]
            alpha, pv = _softmax_step(s_ref[slot, h], bm_ref[slot, h], c, vt_h, m_ref, l_ref, h)
            rows = slice(h * half, (h + 1) * half)
            acc_ref[rows, :] = alpha * acc_ref[rows, :] + pv

    _sweep_key_tiles(qi, produce, consume)
    sub = lax.broadcasted_iota(jnp.int32, (LANES, blk), 0)
    o_t = acc_ref[...] / jnp.where(sub < half, l_ref[0], l_ref[1])
    o_ref[...] = o_t.T.astype(BF16)


def _fox_attention(qkvh, kaug, vt, f_start, n_seq, t, blk):
    m = n_seq * t
    nq = t // blk
    grid_spec = pltpu.PrefetchScalarGridSpec(
        num_scalar_prefetch=1,
        grid=(n_seq, HEAD_PAIRS, nq),
        in_specs=[pl.BlockSpec((1, 1, blk, LANES), lambda b, p, i, fs: (0, p, b * nq + i, 0)),
                  pl.BlockSpec((1, t, 2 * LANES), lambda b, p, i, fs: (p, b, 0)),
                  pl.BlockSpec((1, nq, _vt_rows(FOX_HD), blk), lambda b, p, i, fs: (p, b, 0, 0))],
        out_specs=pl.BlockSpec((blk, LANES), lambda b, p, i, fs: (b * nq + i, p)),
        scratch_shapes=[pltpu.VMEM((LANES, blk), F32),
                        pltpu.VMEM((2, 1, blk), F32),
                        pltpu.VMEM((2, 1, blk), F32),
                        pltpu.VMEM((2, 2, blk, blk), F32),
                        pltpu.VMEM((2, 2, 1, blk), F32)])
    return pl.pallas_call(
        functools.partial(_fox_attn_body, blk=blk, nq=nq),
        grid_spec=grid_spec,
        out_shape=jax.ShapeDtypeStruct((m, D_MODEL), BF16),
        compiler_params=_cparams(3), name="fox_attention",
    )(f_start, qkvh, kaug, vt)


def _diff_lambda(lam_ref, lam_init):
    lp = lam_ref[...]
    a = jnp.sum(lp[0:1] * lp[1:2], axis=1, keepdims=True)
    b = jnp.sum(lp[2:3] * lp[3:4], axis=1, keepdims=True)
    return jnp.exp(a) - jnp.exp(b) + lam_init


def _subln(o, g, lam_init):
    o = o * lax.rsqrt(jnp.mean(o * o, axis=-1, keepdims=True) + LN_EPS)
    return o * g * (1.0 - lam_init)


def _diff_attn_body(q_ref, ka_ref, vt_ref, lam_ref, g_ref, o_ref, acc_ref, m_ref, l_ref, s_ref, bm_ref, *,
                    blk, lam_init):
    head = pl.program_id(1)
    qi = pl.program_id(2)
    qa = _augment_queries(q_ref[0, 0], blk, (0, 0))
    acc_ref[...] = jnp.zeros_like(acc_ref)
    m_ref[...] = jnp.full_like(m_ref, NEG_INF)
    l_ref[...] = jnp.zeros_like(l_ref)
    slope = lax.bitcast_convert_type(jnp.full((1, blk), 126 - head, jnp.int32) << 23, F32) * LOG2E

    def produce(kj, slot, diagonal):
        start = pl.multiple_of(kj * blk, blk)
        ka = ka_ref[0, pl.ds(start, blk), :]
        if diagonal:
            key = lax.broadcasted_iota(jnp.int32, (blk, blk), 0)
            query = lax.broadcasted_iota(jnp.int32, (blk, blk), 1)
            visible = (key // CHUNK) <= (query // CHUNK)
            ahead = slope * (-2.0 * jnp.maximum(key - query, 0).astype(F32))
        for u in range(2):
            s = _dot_nt(ka, qa[u])
            if diagonal:
                s = jnp.where(visible, s + ahead, NEG_INF)
            s_ref[slot, u] = s
            bm_ref[slot, u] = jnp.max(s, axis=0, keepdims=True)

    def consume(kj, slot):
        vt = vt_ref[0, kj]
        c = slope * jnp.full((1, blk), (kj - qi) * blk, jnp.int32).astype(F32)
        for u in range(2):
            alpha, pv = _softmax_step(s_ref[slot, u], bm_ref[slot, u], c, vt, m_ref, l_ref, u)
            acc_ref[u] = alpha * acc_ref[u] + pv

    _sweep_key_tiles(qi, produce, consume)
    lam = _diff_lambda(lam_ref, lam_init)
    o_t = acc_ref[0] / l_ref[0] - lam * (acc_ref[1] / l_ref[1])
    o_ref[...] = _subln(o_t.T, g_ref[...], lam_init).astype(BF16)


def _diff_attention(qkvh, kaug, vt, lam_params, subln_g, n_seq, t, blk, lam_init):
    m = n_seq * t
    nq = t // blk
    return pl.pallas_call(
        functools.partial(_diff_attn_body, blk=blk, lam_init=lam_init),
        grid=(n_seq, DIFF_HEADS, nq),
        in_specs=[pl.BlockSpec((1, 1, blk, LANES), lambda b, p, i: (0, p, b * nq + i, 0)),
                  pl.BlockSpec((1, t, 2 * LANES), lambda b, p, i: (p, b, 0)),
                  pl.BlockSpec((1, nq, _vt_rows(LANES), blk), lambda b, p, i: (p, b, 0, 0)),
                  pl.BlockSpec((4, DIFF_HD), lambda b, p, i: (0, 0)),
                  pl.BlockSpec((1, LANES), lambda b, p, i: (0, 0))],
        out_specs=pl.BlockSpec((blk, LANES), lambda b, p, i: (b * nq + i, p)),
        out_shape=jax.ShapeDtypeStruct((m, D_MODEL), BF16),
        scratch_shapes=[pltpu.VMEM((2, LANES, blk), F32),
                        pltpu.VMEM((2, 1, blk), F32),
                        pltpu.VMEM((2, 1, blk), F32),
                        pltpu.VMEM((2, 2, blk, blk), F32),
                        pltpu.VMEM((2, 2, 1, blk), F32)],
        compiler_params=_cparams(3), name="diff_attention",
    )(qkvh, kaug, vt, lam_params, subln_g)


def _decode_diff_body(q_ref, kn_ref, vn_ref, kc_ref, vc_ref, lam_ref, g_ref, o_ref, acc_ref, m_ref, l_ref, *,
                      t_new, past_len, blk, nk, lam_init):
    kj = pl.program_id(1)
    rows = 2 * t_new
    lane = lax.broadcasted_iota(jnp.int32, (t_new, LANES), 1)
    lo_mask = lane < DIFF_HD

    @pl.when(kj == 0)
    def _():
        acc_ref[...] = jnp.zeros_like(acc_ref)
        m_ref[...] = jnp.full_like(m_ref, NEG_INF)
        l_ref[...] = jnp.zeros_like(l_ref)

    def attend(keys, values, bias, visible):
        heads = range(DIFF_HEADS)
        scores = []
        for h in heads:
            q = q_ref[0, h]
            zero = jnp.zeros_like(q)
            q_maps = jnp.concatenate([jnp.where(lo_mask, q, zero), jnp.where(lo_mask, zero, q)], axis=0)
            s = _dot_nt(q_maps, keys(h)) + bias(h)
            scores.append(s if visible is None else jnp.where(visible, s, NEG_INF))
        m_old = [m_ref[h] for h in heads]
        m_new = [jnp.maximum(m_old[h], jnp.max(scores[h], axis=1, keepdims=True)) for h in heads]
        probs = [jnp.exp2(scores[h] - m_new[h]) for h in heads]
        alpha = [jnp.exp2(m_old[h] - m_new[h]) for h in heads]
        for h in heads:
            l_ref[h] = alpha[h] * l_ref[h] + jnp.sum(probs[h], axis=1, keepdims=True)
            m_ref[h] = m_new[h]
            acc_ref[h] = alpha[h] * acc_ref[h] + _dot(probs[h].astype(BF16), values(h))

    def slope(h):
        return 2.0 ** -(h + 1) * LOG2E

    col = lax.broadcasted_iota(jnp.int32, (1, blk), 1)
    dist = (col + (kj * blk - past_len)).astype(F32)
    attend(lambda h: kc_ref[0, pl.ds(h, blk, stride=DIFF_HEADS), :].astype(BF16),
           lambda h: vc_ref[0, pl.ds(h, blk, stride=DIFF_HEADS), :].astype(BF16),
           lambda h: slope(h) * dist, None)

    @pl.when(kj == nk - 1)
    def _():
        tq = lax.broadcasted_iota(jnp.int32, (rows, t_new), 0) % t_new
        tk = lax.broadcasted_iota(jnp.int32, (rows, t_new), 1)
        near = (tq - jnp.abs(tq - tk)).astype(F32)
        visible = ((tk + past_len) // CHUNK) <= ((tq + past_len) // CHUNK)
        attend(lambda h: kn_ref[0, h], lambda h: vn_ref[0, h], lambda h: slope(h) * near, visible)
        lam = _diff_lambda(lam_ref, lam_init)
        g = g_ref[...]
        outs = []
        for h in range(DIFF_HEADS):
            a = acc_ref[h] / l_ref[h]
            outs.append(_subln(a[:t_new] - lam * a[t_new:], g, lam_init))
        o_ref[0] = jnp.concatenate(outs, axis=1).astype(BF16)


def _decode_diff_attention(qkvh, cache_k, cache_v, lam_params, subln_g, n_seq, t_new, blk, lam_init):
    past_len, heads, hd = cache_k.shape[1:]
    nk = past_len // blk
    rows = 2 * t_new
    new_spec = lambda s: pl.BlockSpec((1, DIFF_HEADS, t_new, LANES), lambda b, j: (s, 0, b, 0))
    cache_k = cache_k.reshape(n_seq, past_len * heads, hd)
    cache_v = cache_v.reshape(n_seq, past_len * heads, hd)
    cache_spec = pl.BlockSpec((1, blk * heads, hd), lambda b, j: (b, j, 0))
    out = pl.pallas_call(
        functools.partial(_decode_diff_body, t_new=t_new, past_len=past_len, blk=blk, nk=nk, lam_init=lam_init),
        grid=(n_seq, nk),
        in_specs=[new_spec(0), new_spec(1), new_spec(2), cache_spec, cache_spec,
                  pl.BlockSpec((4, DIFF_HD), lambda b, j: (0, 0)),
                  pl.BlockSpec((1, LANES), lambda b, j: (0, 0))],
        out_specs=pl.BlockSpec((1, t_new, D_MODEL), lambda b, j: (b, 0, 0)),
        out_shape=jax.ShapeDtypeStruct((n_seq, t_new, D_MODEL), BF16),
        scratch_shapes=[pltpu.VMEM((DIFF_HEADS, rows, LANES), F32),
                        pltpu.VMEM((DIFF_HEADS, rows, 1), F32),
                        pltpu.VMEM((DIFF_HEADS, rows, 1), F32)],
        compiler_params=_cparams(2), name="decode_attention_diff",
    )(qkvh, qkvh, qkvh, cache_k, cache_v, lam_params, subln_g)
    return out.reshape(n_seq * t_new, D_MODEL)


def _decode_fox_body(q_ref, knt_ref, vnt_ref, kt_ref, vt_ref, f_ref, o_ref, acc_ref, m_ref, l_ref, *,
                     t_new, blk, nk):
    kj = pl.program_id(1)
    lane = lax.broadcasted_iota(jnp.int32, (t_new, LANES), 1)
    lo_mask = lane < FOX_HD

    @pl.when(kj == 0)
    def _():
        acc_ref[...] = jnp.zeros_like(acc_ref)
        m_ref[...] = jnp.full_like(m_ref, NEG_INF)
        l_ref[...] = jnp.zeros_like(l_ref)

    f_last = f_ref[nk - 1][:, blk - 1:blk]

    def attend(kt_pairs, vt_pairs, bias, visible):
        heads = range(FOX_HEADS)
        scores = []
        for p in range(HEAD_PAIRS):
            q = q_ref[0, p]
            zero = jnp.zeros_like(q)
            kt = kt_pairs(p)
            for qh in (jnp.where(lo_mask, q, zero), jnp.where(lo_mask, zero, q)):
                h = len(scores)
                s = _dot(qh, kt) + bias[h:h + 1, :]
                scores.append(s if visible is None else jnp.where(visible, s, NEG_INF))
        m_old = [m_ref[h] for h in heads]
        m_new = [jnp.maximum(m_old[h], jnp.max(scores[h], axis=1, keepdims=True)) for h in heads]
        probs = [jnp.exp2(scores[h] - m_new[h]) for h in heads]
        alpha = [jnp.exp2(m_old[h] - m_new[h]) for h in heads]
        for h in heads:
            l_ref[h] = alpha[h] * l_ref[h] + jnp.sum(probs[h], axis=1, keepdims=True)
            m_ref[h] = m_new[h]
        for p in range(HEAD_PAIRS):
            vt = vt_pairs(p)
            pv = [_dot_nt(probs[2 * p + e].astype(BF16), vt) for e in range(2)]
            acc = acc_ref[p]
            acc_ref[p] = jnp.where(lo_mask, alpha[2 * p] * acc + pv[0], alpha[2 * p + 1] * acc + pv[1])

    attend(lambda p: kt_ref[0, p].astype(BF16), lambda p: vt_ref[0, p].astype(BF16),
           (f_last - f_ref[kj]) * LOG2E, None)

    @pl.when(kj == nk - 1)
    def _():
        tq = lax.broadcasted_iota(jnp.int32, (t_new, t_new), 0)
        tk = lax.broadcasted_iota(jnp.int32, (t_new, t_new), 1)
        attend(lambda p: knt_ref[0, p * LANES:(p + 1) * LANES, :].astype(BF16),
               lambda p: vnt_ref[0, p * LANES:(p + 1) * LANES, :].astype(BF16),
               (f_last - f_ref[nk][:, :t_new]) * LOG2E, tk <= tq)
        o_ref[0] = jnp.concatenate(
            [acc_ref[p] / jnp.where(lo_mask, l_ref[2 * p], l_ref[2 * p + 1]) for p in range(HEAD_PAIRS)],
            axis=1).astype(BF16)


def _decode_fox_attention(qkvh, knt, vnt, cache_kt, cache_vt, fcum, n_seq, t_new, blk):
    past_len = cache_kt.shape[3]
    nk = past_len // blk
    cache_spec = pl.BlockSpec((1, HEAD_PAIRS, LANES, blk), lambda b, j: (b, 0, 0, j))
    new_spec = pl.BlockSpec((1, D_MODEL, t_new), lambda b, j: (b, 0, 0))
    out = pl.pallas_call(
        functools.partial(_decode_fox_body, t_new=t_new, blk=blk, nk=nk),
        grid=(n_seq, nk),
        in_specs=[pl.BlockSpec((1, HEAD_PAIRS, t_new, LANES), lambda b, j: (0, 0, b, 0)),
                  new_spec, new_spec, cache_spec, cache_spec,
                  pl.BlockSpec((nk + 1, FOX_HEADS, blk), lambda b, j: (0, b, 0))],
        out_specs=pl.BlockSpec((1, t_new, D_MODEL), lambda b, j: (b, 0, 0)),
        out_shape=jax.ShapeDtypeStruct((n_seq, t_new, D_MODEL), BF16),
        scratch_shapes=[pltpu.VMEM((HEAD_PAIRS, t_new, LANES), F32),
                        pltpu.VMEM((FOX_HEADS, t_new, 1), F32),
                        pltpu.VMEM((FOX_HEADS, t_new, 1), F32)],
        compiler_params=_cparams(2), name="decode_attention_fox",
    )(qkvh, knt, vnt, cache_kt, cache_vt, fcum)
    return out.reshape(n_seq * t_new, D_MODEL)


def _mixer_out_ffn_body(x_ref, o_ref, wo_ref, g1_ref, b1_ref, win_ref, wout_ref, g2_ref, b2_ref, y_ref):
    half = x_ref.shape[0] // 2
    halves = [slice(0, half), slice(half, 2 * half)]
    mix = [_dot(o_ref[r, :], wo_ref[...]) for r in halves]
    x = [_layernorm(DEEPNORM_ALPHA * x_ref[r, :] + mix[i], g1_ref[...], b1_ref[...]) for i, r in enumerate(halves)]
    h = [_dot(xi.astype(BF16), win_ref[...]) for xi in x]
    act = [(hi[:, :D_FF] * jax.nn.sigmoid(hi[:, :D_FF]) * hi[:, D_FF:]).astype(BF16) for hi in h]
    y = [_dot(a, wout_ref[...]) for a in act]
    for i, r in enumerate(halves):
        y_ref[r, :] = _layernorm(DEEPNORM_ALPHA * x[i] + y[i], g2_ref[...], b2_ref[...])


def _mixer_out_ffn(x2d, o2d, w_o, g1, b1, w_in, w_out, g2, b2, tm=512):
    m = x2d.shape[0]
    tm = min(tm, m)
    row = pl.BlockSpec((tm, D_MODEL), lambda i: (i, 0))
    vec = pl.BlockSpec((1, D_MODEL), lambda i: (0, 0))
    resident = lambda shape: pl.BlockSpec(shape, lambda i: (0, 0), pipeline_mode=pl.Buffered(1))
    return pl.pallas_call(
        _mixer_out_ffn_body, grid=(m // tm,),
        in_specs=[row, row, resident((D_MODEL, D_MODEL)), vec, vec,
                  resident((D_MODEL, 2 * D_FF)),
                  resident((D_FF, D_MODEL)),
                  vec, vec],
        out_specs=row, out_shape=jax.ShapeDtypeStruct((m, D_MODEL), F32),
        compiler_params=_cparams(1), name="mixer_out_ffn",
    )(x2d, o2d, w_o, g1, b1, w_in, w_out, g2, b2)


def _run_stream(x, past, wts):
    n_seq, t, _ = x.shape
    m = n_seq * t
    blk = ATTN_BLK
    x2d = x.reshape(m, D_MODEL)
    lam_init = 0.8 - 0.6 * math.exp(-0.3 * 1)

    groups = n_seq if t >= blk else 1
    qkvh, fkt, fvt, vt, kaug, f_start, logf, logft = _qkv_proj(
        x2d, groups, wts["fox_wqkv"], wts["fox_wvt"], FOX_HD ** -0.5 * LOG2E, forget=wts["fox_forget"])
    if past is None:
        o = _fox_attention(qkvh, kaug, vt, f_start[:, 0, :FOX_HEADS].reshape(-1), n_seq, t, blk)
    else:
        past_k, past_v, past_lf = past[0], past[1], past[2]
        past_len = past_k.shape[1]
        dblk = min(DECODE_BLK, past_len)
        lf_all = jnp.concatenate(
            [jnp.transpose(past_lf, (0, 2, 1)),
             jnp.transpose(logft.reshape(FOX_HEADS, n_seq, t), (1, 0, 2)),
             jnp.zeros((n_seq, FOX_HEADS, dblk - t), F32)], axis=2)
        fcum = _cumsum_time(lf_all.reshape(n_seq * FOX_HEADS, past_len + dblk), dblk)
        cache_t = lambda c: jnp.transpose(c, (0, 2, 3, 1)).reshape(n_seq, HEAD_PAIRS, LANES, past_len)
        new_t = lambda a: jnp.transpose(a[0].reshape(D_MODEL, n_seq, t), (1, 0, 2))
        o = _decode_fox_attention(qkvh, new_t(fkt), new_t(fvt), cache_t(past_k), cache_t(past_v), fcum,
                                  n_seq, t, dblk)
    x2d = _mixer_out_ffn(x2d, o, wts["fox_wout"], wts["ln_g"][0][0], wts["ln_b"][0][0],
                         wts["ffn_win"][0], wts["ffn_wout"][0], wts["ln_g"][0][1], wts["ln_b"][0][1])

    qkvh, dk, dv, vt, kaug = _qkv_proj(x2d, n_seq, wts["diff_wqkv"], wts["diff_wvt"], DIFF_HD ** -0.5 * LOG2E)
    if past is None:
        o = _diff_attention(qkvh, kaug, vt, wts["diff_lambda"], wts["diff_g"], n_seq, t, blk, lam_init)
    else:
        o = _decode_diff_attention(qkvh, past[3], past[4], wts["diff_lambda"], wts["diff_g"], n_seq, t,
                                   min(DECODE_BLK, past[3].shape[1]), lam_init)
    x2d = _mixer_out_ffn(x2d, o, wts["diff_wout"], wts["ln_g"][1][0], wts["ln_b"][1][0],
                         wts["ffn_win"][1], wts["ffn_wout"][1], wts["ln_g"][1][1], wts["ln_b"][1][1])

    def untranspose(a):
        a = a.reshape(groups, FOX_HEADS, FOX_HD, n_seq // groups, t)
        return jnp.transpose(a, (0, 3, 4, 1, 2)).reshape(1, n_seq, t, FOX_HEADS, FOX_HD)

    return (x2d.reshape(n_seq, t, D_MODEL),
            untranspose(fkt), untranspose(fvt),
            logf.reshape(1, n_seq, t, FOX_HEADS),
            dk.reshape(1, n_seq, t, DIFF_HEADS, 2 * DIFF_HD), dv.reshape(1, n_seq, t, DIFF_HEADS, 2 * DIFF_HD))


def _prepare_weights(fox_w_in, fox_b_f, fox_w_out, diff_w_in, diff_lambda, diff_subln_g, diff_w_out,
                     ffn_w_in, ffn_w_out, ln_g, ln_b):
    wf = fox_w_in[0][:, 3 * D_MODEL:].astype(BF16)
    wf_pad = jnp.pad(wf, ((0, 0), (0, LANES - FOX_HEADS)))
    bf_pad = jnp.pad(fox_b_f[0].reshape(1, FOX_HEADS), ((0, 0), (0, LANES - FOX_HEADS)))
    return {
        "fox_wqkv": fox_w_in[0][:, :3 * D_MODEL].astype(BF16),
        "fox_wvt": fox_w_in[0][:, 2 * D_MODEL:3 * D_MODEL].T.astype(BF16),
        "fox_forget": (fox_w_in[0][:, D_MODEL:2 * D_MODEL].T.astype(BF16),
                       wf_pad, wf.T, bf_pad, fox_b_f[0].reshape(FOX_HEADS, 1)),
        "fox_wout": fox_w_out[0].astype(BF16),
        "diff_wqkv": diff_w_in[0].astype(BF16),
        "diff_wvt": diff_w_in[0][:, 2 * D_MODEL:].T.astype(BF16),
        "diff_lambda": diff_lambda[0],
        "diff_g": diff_subln_g[0].reshape(1, 2 * DIFF_HD),
        "diff_wout": diff_w_out[0].astype(BF16),
        "ffn_win": [ffn_w_in[i].astype(BF16) for i in range(DEPTH)],
        "ffn_wout": [ffn_w_out[i].astype(BF16) for i in range(DEPTH)],
        "ln_g": [[ln_g[i, j].reshape(1, D_MODEL) for j in range(2)] for i in range(DEPTH)],
        "ln_b": [[ln_b[i, j].reshape(1, D_MODEL) for j in range(2)] for i in range(DEPTH)],
    }


def kernel(x_prompt, x_sample, cache_fox_k, cache_fox_v, cache_fox_logf, cache_diff_k, cache_diff_v, fox_w_in, fox_b_f, fox_w_out, diff_w_in, diff_lambda, diff_subln_g, diff_w_out, ffn_w_in, ffn_w_out, ln_g, ln_b):
    wts = _prepare_weights(fox_w_in, fox_b_f, fox_w_out, diff_w_in, diff_lambda, diff_subln_g, diff_w_out,
                           ffn_w_in, ffn_w_out, ln_g, ln_b)
    y_p, fk_p, fv_p, lf_p, dk_p, dv_p = _run_stream(x_prompt, None, wts)
    past = (cache_fox_k[0], cache_fox_v[0], cache_fox_logf[0], cache_diff_k[0], cache_diff_v[0])
    y_s, fk_s, fv_s, lf_s, dk_s, dv_s = _run_stream(x_sample, past, wts)
    return (y_p, y_s, fk_p, fv_p, lf_p, dk_p, dv_p, fk_s, fv_s, lf_s, dk_s, dv_s)
```
